```python
import math
import jax, jax.numpy as jnp
from jax import lax
import numpy as np

D_MODEL = 1024
BATCH = 4
SEQ = 4096
DEPTH = 1

N_META = 16
MIX_WIDTH = D_MODEL
ATTN_WIDTH = MIX_WIDTH // 2
POOL_WIDTH = MIX_WIDTH - ATTN_WIDTH
ATTN_HEADS = 4
ATTN_VDIM = ATTN_WIDTH // ATTN_HEADS
ATTN_QKDIM = ATTN_VDIM // 2
POOL_WINDOWS = (2, 4, 8, 16)
POOL_GROUPS = len(POOL_WINDOWS)
POOL_GDIM = POOL_WIDTH // POOL_GROUPS
N_EXPERTS = 16
EC_CAPACITY_FACTOR = 2
D_FF_EXPERT = 2816
Q_BLOCK = 128
EPS = 1e-6
IN_WIDTH = 4 * ATTN_HEADS * ATTN_QKDIM + ATTN_WIDTH + POOL_WIDTH

kernel_name = "hybrid_diffattn_pool_ecmoe_encoder"


def rmsnorm(x, g):
    xf = x.astype(jnp.float32)
    y = xf * lax.rsqrt(jnp.mean(xf * xf, axis=-1, keepdims=True) + EPS)
    return (y * g.astype(jnp.float32)).astype(x.dtype)


def alibi_slopes(n_heads):
    return jnp.array([2.0 ** (-8.0 * (i + 1) / n_heads) for i in range(n_heads)], dtype=jnp.float32)


def diff_attention(q1, q2, k1, k2, v, lam):
    B, L, H, d = q1.shape
    scale = 1.0 / math.sqrt(d)
    slopes = alibi_slopes(H)
    n_blk = -(-L // Q_BLOCK)
    Lp = n_blk * Q_BLOCK
    tr = lambda t: jnp.transpose(t, (0, 2, 1, 3))
    q1, q2, k1, k2, v = tr(q1), tr(q2), tr(k1), tr(k2), tr(v)
    pad = ((0, 0), (0, 0), (0, Lp - L), (0, 0))
    to_blocks = lambda q: jnp.moveaxis(jnp.pad(q, pad).reshape(B, H, n_blk, Q_BLOCK, d), 2, 0)
    q1b, q2b = to_blocks(q1), to_blocks(q2)
    starts = jnp.arange(n_blk, dtype=jnp.int32) * Q_BLOCK
    kpos = jnp.arange(L, dtype=jnp.float32)

    def block(args):
        qa, qb, start = args
        qpos = (start + jnp.arange(Q_BLOCK, dtype=jnp.int32)).astype(jnp.float32)
        dist = jnp.abs(qpos[:, None] - kpos[None, :])
        bias = -slopes[:, None, None] * dist[None]
        s1 = jnp.einsum('bhqd,bhkd->bhqk', qa, k1).astype(jnp.float32) * scale + bias
        s2 = jnp.einsum('bhqd,bhkd->bhqk', qb, k2).astype(jnp.float32) * scale + bias
        att = jax.nn.softmax(s1, axis=-1) - lam * jax.nn.softmax(s2, axis=-1)
        return jnp.einsum('bhqk,bhkv->bhqv', att.astype(v.dtype), v)

    out = lax.map(block, (q1b, q2b, starts))
    out = jnp.moveaxis(out, 0, 2).reshape(B, H, Lp, v.shape[-1])[:, :, :L]
    return jnp.transpose(out, (0, 2, 1, 3))


def multiscale_pool(u, pool_w, pool_scale):
    B, L, _ = u.shape
    uf = u.astype(jnp.float32)
    cs = jnp.concatenate([jnp.zeros((B, 1, POOL_WIDTH), jnp.float32), jnp.cumsum(uf, axis=1)], axis=1)
    t = jnp.arange(L, dtype=jnp.int32)
    parts = []
    for gi, w in enumerate(POOL_WINDOWS):
        sl = slice(gi * POOL_GDIM, (gi + 1) * POOL_GDIM)
        lo = jnp.clip(t - w // 2, 0, L)
        hi = jnp.clip(t + w // 2, 0, L)
        csg = cs[:, :, sl]
        cnt = (hi - lo).astype(jnp.float32)[None, :, None]
        parts.append((csg[:, hi] - csg[:, lo]) / cnt)
    pooled = (jnp.concatenate(parts, axis=-1) - uf).astype(u.dtype)
    pg = pooled.reshape(B, L, POOL_GROUPS, POOL_GDIM)
    y = jnp.einsum('blgc,gcd->blgd', pg, pool_w).reshape(B, L, POOL_WIDTH)
    return y * pool_scale


def expert_choice_moe(h, router_w, w_gate, w_up, w_down):
    B, L, D = h.shape
    C = EC_CAPACITY_FACTOR * L // N_EXPERTS
    logits = jnp.einsum('bld,de->ble', h, router_w).astype(jnp.float32)
    aff = jax.nn.softmax(logits, axis=-1)
    gate, idx = lax.top_k(jnp.swapaxes(aff, 1, 2), C)
    xg = jax.vmap(lambda hb, ib: hb[ib])(h, idx)
    hid = jax.nn.silu(jnp.einsum('becd,edf->becf', xg, w_gate)) * jnp.einsum('becd,edf->becf', xg, w_up)
    y = jnp.einsum('becf,efd->becd', hid, w_down) * gate[..., None].astype(h.dtype)
    return jax.vmap(lambda yb, ib: jnp.zeros((L, D), yb.dtype).at[ib.reshape(-1)].add(yb.reshape(-1, D)))(y, idx)


def setup_inputs(seed: int = 0) -> dict:
    key = jax.random.key(seed)
    ks = jax.random.split(key, 16)
    f32 = jnp.float32
    nrm = lambda k, s, sc: jax.random.normal(k, s, f32) * sc
    return {
        "x": jax.random.normal(ks[0], (BATCH, SEQ, D_MODEL), f32),
        "meta_tokens": nrm(ks[1], (N_META, D_MODEL), 1.0),
        "g_mix": 1.0 + nrm(ks[2], (DEPTH, D_MODEL), 0.02),
        "w_in": nrm(ks[3], (DEPTH, D_MODEL, IN_WIDTH), D_MODEL ** -0.5),
        "lam_vecs": nrm(ks[4], (DEPTH, 4, ATTN_QKDIM), 0.1),
        "subln_gain": 1.0 + nrm(ks[5], (DEPTH, ATTN_VDIM), 0.02),
        "pool_w": nrm(ks[6], (DEPTH, POOL_GROUPS, POOL_GDIM, POOL_GDIM), POOL_GDIM ** -0.5),
        "pool_scale": 1.0 + nrm(ks[7], (DEPTH, POOL_WIDTH), 0.02),
        "w_o": nrm(ks[8], (DEPTH, MIX_WIDTH, D_MODEL), MIX_WIDTH ** -0.5),
        "g_ffn": 1.0 + nrm(ks[9], (DEPTH, D_MODEL), 0.02),
        "router_w": nrm(ks[10], (DEPTH, D_MODEL, N_EXPERTS), D_MODEL ** -0.5),
        "w_gate": nrm(ks[11], (DEPTH, N_EXPERTS, D_MODEL, D_FF_EXPERT), D_MODEL ** -0.5),
        "w_up": nrm(ks[12], (DEPTH, N_EXPERTS, D_MODEL, D_FF_EXPERT), D_MODEL ** -0.5),
        "w_down": nrm(ks[13], (DEPTH, N_EXPERTS, D_FF_EXPERT, D_MODEL), D_FF_EXPERT ** -0.5),
        "g_final": 1.0 + nrm(ks[14], (D_MODEL,), 0.02),
    }


def reference(x, meta_tokens, g_mix, w_in, lam_vecs, subln_gain, pool_w, pool_scale, w_o,
              g_ffn, router_w, w_gate, w_up, w_down, g_final):
    B = x.shape[0]
    meta = jnp.broadcast_to(meta_tokens[None].astype(x.dtype), (B, N_META, D_MODEL))
    h = jnp.concatenate([meta, x], axis=1)
    L = h.shape[1]
    H, d = ATTN_HEADS, ATTN_QKDIM
    qk = H * d
    for l in range(DEPTH):
        lam_init = 0.8 - 0.6 * math.exp(-0.3 * l)
        n = rmsnorm(h, g_mix[l])
        proj = jnp.einsum('bld,de->ble', n, w_in[l])
        q1 = proj[..., 0 * qk:1 * qk].reshape(B, L, H, d)
        q2 = proj[..., 1 * qk:2 * qk].reshape(B, L, H, d)
        k1 = proj[..., 2 * qk:3 * qk].reshape(B, L, H, d)
        k2 = proj[..., 3 * qk:4 * qk].reshape(B, L, H, d)
        v = proj[..., 4 * qk:4 * qk + ATTN_WIDTH].reshape(B, L, H, ATTN_VDIM)
        u = proj[..., 4 * qk + ATTN_WIDTH:]
        lv = lam_vecs[l].astype(jnp.float32)
        lam = jnp.exp(jnp.sum(lv[0] * lv[1])) - jnp.exp(jnp.sum(lv[2] * lv[3])) + lam_init
        a = diff_attention(q1, q2, k1, k2, v, lam)
        a = (rmsnorm(a, subln_gain[l]) * (1.0 - lam_init)).reshape(B, L, ATTN_WIDTH)
        p = multiscale_pool(u, pool_w[l], pool_scale[l])
        mix = jnp.concatenate([a, p], axis=-1)
        h = h + jnp.einsum('blm,md->bld', mix, w_o[l])
        h = h + expert_choice_moe(rmsnorm(h, g_ffn[l]), router_w[l], w_gate[l], w_up[l], w_down[l])
    return rmsnorm(h, g_final)[:, N_META:]
```

```python
import functools
import math

import jax
import jax.numpy as jnp
from jax import lax
from jax.experimental import pallas as pl
from jax.experimental.pallas import tpu as pltpu

N_META = 16
HEADS = 4
D_QK = 64
D_V = 128
POOL_WINDOWS = (2, 4, 8, 16)
POOL_HALO = max(POOL_WINDOWS) // 2
N_EXPERTS = 16
CAPACITY_FACTOR = 2
EPS = 1e-6

LANES = 128
SUBLANES = 8
BF16_ROWS = 16

ROW_TILE = 512
ATTN_TQ = 256
ATTN_TK = 512
POOL_CHUNK = 256
FF_TILE = 256
OUT_CHUNK = 512
MIB = 1024 * 1024

f32 = jnp.float32
bf16 = jnp.bfloat16


def _rms(x, g):
    return x * lax.rsqrt(jnp.mean(x * x, axis=-1, keepdims=True) + EPS) * g


def _norm_inproj_kernel(x_ref, g_ref, w_ref, qkv_ref, u_ref):
    n = _rms(x_ref[...], g_ref[...])
    proj = jnp.dot(n.astype(bf16), w_ref[...], preferred_element_type=f32)
    n_qkv = qkv_ref.shape[-1]
    qkv_ref[...] = proj[:, :n_qkv].astype(bf16)
    u_ref[...] = proj[:, n_qkv:]


def _norm_inproj(h, g, w, n_qkv):
    m, d = h.shape
    n_out = w.shape[1]
    return pl.pallas_call(
        _norm_inproj_kernel,
        grid=(pl.cdiv(m, ROW_TILE),),
        in_specs=[
            pl.BlockSpec((ROW_TILE, d), lambda i: (i, 0)),
            pl.BlockSpec((1, d), lambda i: (0, 0)),
            pl.BlockSpec((d, n_out), lambda i: (0, 0)),
        ],
        out_specs=[
            pl.BlockSpec((ROW_TILE, n_qkv), lambda i: (i, 0)),
            pl.BlockSpec((ROW_TILE, n_out - n_qkv), lambda i: (i, 0)),
        ],
        out_shape=[
            jax.ShapeDtypeStruct((m, n_qkv), bf16),
            jax.ShapeDtypeStruct((m, n_out - n_qkv), f32),
        ],
        compiler_params=pltpu.CompilerParams(
            dimension_semantics=("arbitrary",), vmem_limit_bytes=40 * MIB),
        name="norm_inproj",
    )(h, g, w)


def _attn_kernel(lam_ref, gain_ref, slope_ref, q_ref, k_ref, v_ref, o_ref, m_scr, l_scr, acc_scr,
                 *, n_body, lam_init):
    tq = q_ref.shape[1]
    i = pl.program_id(2)
    n_q = pl.num_programs(2)
    slope = slope_ref[0][:, :1]

    q = q_ref[0]
    lane = lax.broadcasted_iota(jnp.int32, q.shape, 1)
    zero = jnp.zeros_like(q)
    qs = jnp.concatenate([jnp.where(lane < D_QK, q, zero), jnp.where(lane >= D_QK, q, zero)], axis=0)

    q_pos0 = jnp.where(i == n_q - 1, 0, N_META + i * tq)
    q_pos = (q_pos0 + lax.broadcasted_iota(jnp.int32, (tq, 1), 0)).astype(f32)

    m_scr[...] = jnp.full(m_scr.shape, -jnp.inf, f32)
    l_scr[...] = jnp.zeros(l_scr.shape, f32)
    acc_scr[...] = jnp.zeros(acc_scr.shape, f32)

    def step(k, v, k_pos0):
        n = k.shape[0]
        k_pos = (k_pos0 + lax.broadcasted_iota(jnp.int32, (1, n), 1)).astype(f32)
        bias = -slope * jnp.abs(q_pos - k_pos)
        s = lax.dot_general(qs, k, (((1,), (1,)), ((), ())), preferred_element_type=f32)
        s = s + jnp.concatenate([bias, bias], axis=0)
        m_old = m_scr[...]
        m_new = jnp.maximum(m_old, jnp.max(s, axis=-1, keepdims=True))
        p = jnp.exp(s - m_new)
        alpha = jnp.exp(m_old - m_new)
        l_scr[...] = alpha * l_scr[...] + jnp.sum(p, axis=-1, keepdims=True)
        acc_scr[...] = alpha * acc_scr[...] + jnp.dot(p.astype(bf16), v, preferred_element_type=f32)
        m_scr[...] = m_new

    def body(c, carry):
        r0 = pl.multiple_of(c * ATTN_TK, ATTN_TK)
        step(k_ref[0, pl.ds(r0, ATTN_TK), :], v_ref[0, pl.ds(r0, ATTN_TK), :], N_META + c * ATTN_TK)
        return carry

    lax.fori_loop(0, n_body // ATTN_TK, body, 0)
    step(k_ref[0, n_body:n_body + N_META, :], v_ref[0, n_body:n_body + N_META, :], 0)

    lv = lam_ref[...]
    lam = (jnp.exp(jnp.sum(lv[0:1] * lv[1:2], axis=-1, keepdims=True))
           - jnp.exp(jnp.sum(lv[2:3] * lv[3:4], axis=-1, keepdims=True)) + lam_init)
    o = acc_scr[...] / l_scr[...]
    att = o[:tq] - lam * o[tq:]
    o_ref[0] = (_rms(att, gain_ref[...]) * (1.0 - lam_init)).astype(bf16)


def _diff_attention(qkv, lam_vecs, gain, slopes, n_body, lam_init):
    b, l, _ = qkv.shape
    kernel = functools.partial(_attn_kernel, n_body=n_body, lam_init=lam_init)
    return pl.pallas_call(
        kernel,
        grid=(b, HEADS, pl.cdiv(l, ATTN_TQ)),
        in_specs=[
            pl.BlockSpec(lam_vecs.shape, lambda bi, h, i: (0, 0)),
            pl.BlockSpec((1, D_V), lambda bi, h, i: (0, 0)),
            pl.BlockSpec((1, 1, LANES), lambda bi, h, i: (h, 0, 0)),
            pl.BlockSpec((1, ATTN_TQ, 2 * D_QK), lambda bi, h, i: (bi, i, h)),
            pl.BlockSpec((1, l, 2 * D_QK), lambda bi, h, i: (bi, 0, HEADS + h)),
            pl.BlockSpec((1, l, D_V), lambda bi, h, i: (bi, 0, 2 * HEADS + h)),
        ],
        out_specs=pl.BlockSpec((1, ATTN_TQ, D_V), lambda bi, h, i: (bi, i, h)),
        out_shape=jax.ShapeDtypeStruct((b, l, HEADS * D_V), bf16),
        scratch_shapes=[
            pltpu.VMEM((2 * ATTN_TQ, 1), f32),
            pltpu.VMEM((2 * ATTN_TQ, 1), f32),
            pltpu.VMEM((2 * ATTN_TQ, D_V), f32),
        ],
        compiler_params=pltpu.CompilerParams(
            dimension_semantics=("arbitrary", "arbitrary", "arbitrary"), vmem_limit_bytes=40 * MIB),
        name="diff_attention",
    )(lam_vecs, gain, slopes, qkv, qkv, qkv)


def _pool_kernel(u_ref, pw_ref, ps_ref, o_ref, upad, *, n_body):
    l = n_body + N_META
    width = u_ref.shape[-1]
    upad[0:POOL_HALO, :] = jnp.zeros((POOL_HALO, width), f32)
    upad[POOL_HALO:POOL_HALO + N_META, :] = u_ref[0, n_body:l, :]
    upad[POOL_HALO + N_META:POOL_HALO + l, :] = u_ref[0, 0:n_body, :]
    upad[POOL_HALO + l:2 * POOL_HALO + l, :] = jnp.zeros((POOL_HALO, width), f32)

    def rows(pos0, n, out_row0):
        slab = upad[pl.ds(pos0, n + 2 * POOL_HALO), :]
        t = pos0 + lax.broadcasted_iota(jnp.int32, (n, 1), 0)
        for gi, w in enumerate(POOL_WINDOWS):
            cols = slice(gi * LANES, (gi + 1) * LANES)
            total = slab[POOL_HALO - w // 2:POOL_HALO - w // 2 + n, cols]
            for j in range(1, w):
                total = total + slab[POOL_HALO - w // 2 + j:POOL_HALO - w // 2 + j + n, cols]
            cnt = (jnp.minimum(t + w // 2, l) - jnp.maximum(t - w // 2, 0)).astype(f32)
            pooled = total / cnt - slab[POOL_HALO:POOL_HALO + n, cols]
            y = jnp.dot(pooled.astype(bf16), pw_ref[gi], preferred_element_type=f32) * ps_ref[:, cols]
            o_ref[0, pl.ds(out_row0, n), cols] = y.astype(bf16)

    rows(0, N_META, n_body)

    def body(c, carry):
        r0 = pl.multiple_of(c * POOL_CHUNK, POOL_CHUNK)
        rows(N_META + r0, POOL_CHUNK, r0)
        return carry

    lax.fori_loop(0, n_body // POOL_CHUNK, body, 0)


def _multiscale_pool(u, pool_w, pool_scale, n_body):
    b, l, width = u.shape
    kernel = functools.partial(_pool_kernel, n_body=n_body)
    return pl.pallas_call(
        kernel,
        grid=(b,),
        in_specs=[
            pl.BlockSpec((1, l, width), lambda bi: (bi, 0, 0)),
            pl.BlockSpec(pool_w.shape, lambda bi: (0, 0, 0)),
            pl.BlockSpec((1, width), lambda bi: (0, 0)),
        ],
        out_specs=pl.BlockSpec((1, l, width), lambda bi: (bi, 0, 0)),
        out_shape=jax.ShapeDtypeStruct((b, l, width), bf16),
        scratch_shapes=[pltpu.VMEM((l + 2 * POOL_HALO, width), f32)],
        compiler_params=pltpu.CompilerParams(
            dimension_semantics=("arbitrary",), vmem_limit_bytes=48 * MIB),
        name="multiscale_pool",
    )(u, pool_w, pool_scale)


def _outproj_router_kernel(h_ref, a_ref, p_ref, woa_ref, wop_ref, g_ref, rw_ref, h2_ref, hn_ref, aff_ref):
    tm = h_ref.shape[0]
    h2 = (h_ref[...]
          + jnp.dot(a_ref[...], woa_ref[...], preferred_element_type=f32)
          + jnp.dot(p_ref[...], wop_ref[...], preferred_element_type=f32))
    h2_ref[...] = h2
    hn = _rms(h2, g_ref[...])
    logits = jnp.dot(hn.astype(bf16), rw_ref[...], preferred_element_type=f32)
    z = jnp.exp(logits - jnp.max(logits, axis=-1, keepdims=True))
    aff_ref[...] = z / jnp.sum(z, axis=-1, keepdims=True)
    for s in range(hn.shape[1] // LANES):
        hn_ref[pl.ds(s, tm, stride=SUBLANES), :] = hn[:, s * LANES:(s + 1) * LANES]


def _outproj_router(h, a, p, wo_a, wo_p, g, rw):
    m, d = h.shape
    half = a.shape[1]
    n_e = rw.shape[1]
    row = lambda i: (i, 0)
    fixed = lambda i: (0, 0)
    return pl.pallas_call(
        _outproj_router_kernel,
        grid=(pl.cdiv(m, ROW_TILE),),
        in_specs=[
            pl.BlockSpec((ROW_TILE, d), row),
            pl.BlockSpec((ROW_TILE, half), row),
            pl.BlockSpec((ROW_TILE, half), row),
            pl.BlockSpec((half, d), fixed),
            pl.BlockSpec((half, d), fixed),
            pl.BlockSpec((1, d), fixed),
            pl.BlockSpec((d, n_e), fixed),
        ],
        out_specs=[
            pl.BlockSpec((ROW_TILE, d), row),
            pl.BlockSpec((ROW_TILE * SUBLANES, LANES), row),
            pl.BlockSpec((ROW_TILE, n_e), row),
        ],
        out_shape=[
            jax.ShapeDtypeStruct((m, d), f32),
            jax.ShapeDtypeStruct((m * SUBLANES, LANES), f32),
            jax.ShapeDtypeStruct((m, n_e), f32),
        ],
        compiler_params=pltpu.CompilerParams(
            dimension_semantics=("arbitrary",), vmem_limit_bytes=40 * MIB),
        name="outproj_router",
    )(h, a, p, wo_a, wo_p, g, rw)


def _gather_kernel(idx_ref, src_ref, o_ref, tile):
    cp = o_ref.shape[1]

    def group(g, carry):
        for j in range(SUBLANES):
            slot = g * SUBLANES + j
            r = idx_ref[0, 0, slot]
            tile[pl.ds(pl.multiple_of(slot * SUBLANES, SUBLANES), SUBLANES), :] = (
                src_ref[0, pl.ds(pl.multiple_of(r * SUBLANES, SUBLANES), SUBLANES), :])
        return carry

    lax.fori_loop(0, cp // SUBLANES, group, 0)
    for s in range(o_ref.shape[2] // LANES):
        o_ref[0, :, s * LANES:(s + 1) * LANES] = tile[pl.ds(s, cp, stride=SUBLANES), :].astype(bf16)


def _gather_tokens(idx, hn_tiles, d):
    b, n_e, cp = idx.shape
    rows = hn_tiles.shape[1]
    return pl.pallas_call(
        _gather_kernel,
        grid=(b, n_e),
        in_specs=[
            pl.BlockSpec((1, 1, cp), lambda bi, e: (bi * n_e + e, 0, 0), memory_space=pltpu.SMEM),
            pl.BlockSpec((1, rows, LANES), lambda bi, e: (bi, 0, 0)),
        ],
        out_specs=pl.BlockSpec((1, cp, d), lambda bi, e: (e, bi, 0)),
        out_shape=jax.ShapeDtypeStruct((n_e, b * cp, d), bf16),
        scratch_shapes=[pltpu.VMEM((cp * SUBLANES, LANES), f32)],
        compiler_params=pltpu.CompilerParams(
            dimension_semantics=("arbitrary", "arbitrary"), vmem_limit_bytes=48 * MIB),
        name="gather_tokens",
    )(idx.reshape(b * n_e, 1, cp), hn_tiles)


def _moe_kernel(x_ref, gate_ref, wg_ref, wu_ref, wd_ref, y_ref, wg_b, wu_b, wd_b, acc, *, cp):
    f = pl.program_id(1)
    n_f = pl.num_programs(1)
    wg_b[...] = wg_ref[0].astype(bf16)
    wu_b[...] = wu_ref[0].astype(bf16)
    wd_b[...] = wd_ref[0].astype(bf16)

    @pl.when(f == 0)
    def _():
        acc[...] = jnp.zeros(acc.shape, f32)

    n_chunks = acc.shape[0] // cp
    for c in range(n_chunks):
        rows = slice(c * cp, (c + 1) * cp)
        x = x_ref[0, rows, :]
        g = jnp.dot(x, wg_b[...], preferred_element_type=f32)
        u = jnp.dot(x, wu_b[...], preferred_element_type=f32)
        hid = (g * jax.nn.sigmoid(g) * u).astype(bf16)
        acc[rows, :] += jnp.dot(hid, wd_b[...], preferred_element_type=f32)

    @pl.when(f == n_f - 1)
    def _():
        for c in range(n_chunks):
            rows = slice(c * cp, (c + 1) * cp)
            y = acc[rows, :] * gate_ref[0, rows, :]
            for s in range(y.shape[1] // LANES):
                y_ref[0, pl.ds(c * cp * SUBLANES + s, cp, stride=SUBLANES), :] = (
                    y[:, s * LANES:(s + 1) * LANES])


def _expert_ffn(xg, gate, w_gate, w_up, w_down, cp):
    n_e, m, d = xg.shape
    d_ff = w_gate.shape[2]
    kernel = functools.partial(_moe_kernel, cp=cp)
    return pl.pallas_call(
        kernel,
        grid=(n_e, d_ff // FF_TILE),
        in_specs=[
            pl.BlockSpec((1, m, d), lambda e, f: (e, 0, 0)),
            pl.BlockSpec((1, m, 1), lambda e, f: (e, 0, 0)),
            pl.BlockSpec((1, d, FF_TILE), lambda e, f: (e, 0, f)),
            pl.BlockSpec((1, d, FF_TILE), lambda e, f: (e, 0, f)),
            pl.BlockSpec((1, FF_TILE, d), lambda e, f: (e, f, 0)),
        ],
        out_specs=pl.BlockSpec((1, m * SUBLANES, LANES), lambda e, f: (e, 0, 0)),
        out_shape=jax.ShapeDtypeStruct((n_e, m * SUBLANES, LANES), f32),
        scratch_shapes=[
            pltpu.VMEM((d, FF_TILE), bf16),
            pltpu.VMEM((d, FF_TILE), bf16),
            pltpu.VMEM((FF_TILE, d), bf16),
            pltpu.VMEM((m, d), f32),
        ],
        compiler_params=pltpu.CompilerParams(
            dimension_semantics=("arbitrary", "arbitrary"), vmem_limit_bytes=56 * MIB),
        name="expert_ffn",
    )(xg, gate, w_gate, w_up, w_down)


def _combine_kernel(idx_ref, y_ref, h2_ref, g_ref, o_ref, acc, *, n_experts):
    step = pl.program_id(1)
    cp = idx_ref.shape[2]
    tc = o_ref.shape[1]

    @pl.when(step == 0)
    def _():
        acc[...] = jnp.zeros(acc.shape, f32)

    @pl.when(step < n_experts)
    def _():
        def group(g, carry):
            dst = []
            val = []
            for j in range(SUBLANES):
                slot = g * SUBLANES + j
                r = pl.multiple_of(idx_ref[0, 0, slot] * SUBLANES, SUBLANES)
                dst.append(r)
                val.append(acc[pl.ds(r, SUBLANES), :]
                           + y_ref[0, pl.ds(pl.multiple_of(slot * SUBLANES, SUBLANES), SUBLANES), :])
            for r, v in zip(dst, val):
                acc[pl.ds(r, SUBLANES), :] = v
            return carry

        lax.fori_loop(0, cp // SUBLANES, group, 0)

    @pl.when(step >= n_experts)
    def _():
        r0 = pl.multiple_of((step - n_experts) * tc * SUBLANES, SUBLANES)
        moe = jnp.concatenate(
            [acc[pl.ds(r0 + s, tc, stride=SUBLANES), :] for s in range(o_ref.shape[2] // LANES)], axis=-1)
        o_ref[0] = _rms(h2_ref[0] + moe, g_ref[...])


def _combine(idx, y, h2, g, n_body):
    b, n_e, cp = idx.shape
    _, l, d = h2.shape
    n_out = n_body // OUT_CHUNK
    kernel = functools.partial(_combine_kernel, n_experts=n_e)
    expert = lambda s: jnp.minimum(s, n_e - 1)
    chunk = lambda s: jnp.maximum(s - n_e, 0)
    return pl.pallas_call(
        kernel,
        grid=(b, n_e + n_out),
        in_specs=[
            pl.BlockSpec((1, 1, cp), lambda bi, s: (bi * n_e + expert(s), 0, 0), memory_space=pltpu.SMEM),
            pl.BlockSpec((1, cp * SUBLANES, LANES), lambda bi, s: (expert(s), bi, 0)),
            pl.BlockSpec((1, OUT_CHUNK, d), lambda bi, s: (bi, chunk(s), 0)),
            pl.BlockSpec((1, d), lambda bi, s: (0, 0)),
        ],
        out_specs=pl.BlockSpec((1, OUT_CHUNK, d), lambda bi, s: (bi, chunk(s), 0)),
        out_shape=jax.ShapeDtypeStruct((b, n_body, d), f32),
        scratch_shapes=[pltpu.VMEM(((l + 1) * SUBLANES, LANES), f32)],
        compiler_params=pltpu.CompilerParams(
            dimension_semantics=("arbitrary", "arbitrary"), vmem_limit_bytes=48 * MIB),
        name="combine",
    )(idx.reshape(b * n_e, 1, cp), y, h2, g)


def _round_up(n, k):
    return -(-n // k) * k


def kernel(x, meta_tokens, g_mix, w_in, lam_vecs, subln_gain, pool_w, pool_scale, w_o, g_ffn, router_w,
           w_gate, w_up, w_down, g_final):
    b, n_body, d = x.shape
    l = n_body + N_META
    qk = HEADS * D_QK
    attn_width = HEADS * D_V
    depth = g_mix.shape[0]
    cap = CAPACITY_FACTOR * l // N_EXPERTS
    cp = _round_up(cap, BF16_ROWS)

    meta = jnp.broadcast_to(meta_tokens[None].astype(x.dtype), (b, N_META, d))
    h = jnp.concatenate([x, meta], axis=1).reshape(b * l, d)
    slopes = jnp.broadcast_to(
        jnp.array([2.0 ** (-8.0 * (i + 1) / HEADS) for i in range(HEADS)], f32)[:, None, None],
        (HEADS, 1, LANES))
    cols = jnp.arange(HEADS)[:, None] * D_QK + jnp.arange(D_QK)[None, :]
    q_cols = jnp.concatenate([cols, qk + cols], axis=1).reshape(-1)
    k_cols = q_cols + 2 * qk
    rest = jnp.arange(4 * qk, w_in.shape[2])
    col_scale = jnp.concatenate([jnp.full((2 * qk,), 1.0 / math.sqrt(D_QK), f32),
                                 jnp.ones((w_in.shape[2] - 2 * qk,), f32)])
    perm = jnp.concatenate([q_cols, k_cols, rest])

    out = None
    for layer in range(depth):
        lam_init = 0.8 - 0.6 * math.exp(-0.3 * layer)
        w_in_l = (w_in[layer][:, perm] * col_scale[None, :]).astype(bf16)
        qkv, u = _norm_inproj(h, g_mix[layer][None, :], w_in_l, 4 * qk + attn_width)
        a = _diff_attention(qkv.reshape(b, l, -1), lam_vecs[layer].astype(f32), subln_gain[layer][None, :],
                            slopes, n_body, lam_init)
        p = _multiscale_pool(u.reshape(b, l, -1), pool_w[layer].astype(bf16), pool_scale[layer][None, :],
                             n_body)
        wo = w_o[layer].astype(bf16)
        h2, hn_tiles, aff = _outproj_router(
            h, a.reshape(b * l, -1), p.reshape(b * l, -1), wo[:attn_width], wo[attn_width:],
            g_ffn[layer][None, :], router_w[layer].astype(bf16))

        aff = aff.reshape(b, l, N_EXPERTS)
        aff_pos = jnp.concatenate([aff[:, n_body:], aff[:, :n_body]], axis=1)
        gate, idx_pos = lax.top_k(jnp.swapaxes(aff_pos, 1, 2), cap)
        idx = jnp.where(idx_pos < N_META, idx_pos + n_body, idx_pos - N_META).astype(jnp.int32)
        pad = ((0, 0), (0, 0), (0, cp - cap))
        idx_gather = jnp.pad(idx, pad)
        idx_scatter = jnp.pad(idx, pad, constant_values=l)
        gate = jnp.pad(gate, pad)
        gate = jnp.transpose(gate, (1, 0, 2)).reshape(N_EXPERTS, b * cp, 1)

        xg = _gather_tokens(idx_gather, hn_tiles.reshape(b, l * SUBLANES, LANES), d)
        y = _expert_ffn(xg, gate, w_gate[layer], w_up[layer], w_down[layer], cp)
        last = layer == depth - 1
        g_out = g_final[None, :] if last else jnp.ones((1, d), f32)
        out = _combine(idx_scatter, y, h2.reshape(b, l, d), g_out, n_body)
    return out
```

```python
import functools
import math

import jax
import jax.numpy as jnp
from jax import lax
from jax.experimental import pallas as pl
from jax.experimental.pallas import tpu as pltpu

N_META = 16
HEADS = 4
D_QK = 64
D_V = 128
POOL_WINDOWS = (2, 4, 8, 16)
POOL_HALO = max(POOL_WINDOWS) // 2
N_EXPERTS = 16
CAPACITY_FACTOR = 2
EPS = 1e-6

LANES = 128
SUBLANES = 8
BF16_ROWS = 16

ROW_TILE = 512
ATTN_TQ = 256
ATTN_TK = 512
POOL_CHUNK = 256
FF_TILE = 256
OUT_CHUNK = 512
MIB = 1024 * 1024

f32 = jnp.float32
bf16 = jnp.bfloat16


def _rms(x, g):
    return x * lax.rsqrt(jnp.mean(x * x, axis=-1, keepdims=True) + EPS) * g


def _norm_inproj_kernel(x_ref, g_ref, wt_ref, w_ref, qvt_ref, k_ref, u_ref):
    n = _rms(x_ref[0], g_ref[...]).astype(bf16)
    qvt = lax.dot_general(wt_ref[...], n, (((1,), (1,)), ((), ())), preferred_element_type=f32)
    qvt_ref[0, 0] = qvt.astype(bf16)
    ku = jnp.dot(n, w_ref[...], preferred_element_type=f32)
    n_k = k_ref.shape[-1]
    k_ref[0] = ku[:, :n_k].astype(bf16)
    u_ref[0] = ku[:, n_k:]


def _norm_inproj(h, g, w_qv_t, w_ku, n_k):
    b, l, d = h.shape
    n_t = pl.cdiv(l, ATTN_TK)
    n_qv = w_qv_t.shape[0]
    n_u = w_ku.shape[1] - n_k
    return pl.pallas_call(
        _norm_inproj_kernel,
        grid=(b, n_t),
        in_specs=[
            pl.BlockSpec((1, ATTN_TK, d), lambda bi, i: (bi, i, 0)),
            pl.BlockSpec((1, d), lambda bi, i: (0, 0)),
            pl.BlockSpec(w_qv_t.shape, lambda bi, i: (0, 0)),
            pl.BlockSpec(w_ku.shape, lambda bi, i: (0, 0)),
        ],
        out_specs=[
            pl.BlockSpec((1, 1, n_qv, ATTN_TK), lambda bi, i: (bi, i, 0, 0)),
            pl.BlockSpec((1, ATTN_TK, n_k), lambda bi, i: (bi, i, 0)),
            pl.BlockSpec((1, ATTN_TK, n_u), lambda bi, i: (bi, i, 0)),
        ],
        out_shape=[
            jax.ShapeDtypeStruct((b, n_t, n_qv, ATTN_TK), bf16),
            jax.ShapeDtypeStruct((b, l, n_k), bf16),
            jax.ShapeDtypeStruct((b, l, n_u), f32),
        ],
        compiler_params=pltpu.CompilerParams(
            dimension_semantics=("arbitrary", "arbitrary"), vmem_limit_bytes=40 * MIB),
        name="norm_inproj",
    )(h, g, w_qv_t, w_ku)


def _attn_kernel(lam_ref, gain_ref, slope_ref, q_ref, k_ref, vt_ref, o_ref,
                 qs_scr, bias_scr, s_scr, p_scr, m_scr, l_scr, acc_scr, *, n_body, lam_init):
    tq = q_ref.shape[3]
    tk = ATTN_TK
    n_diag = tk // tq
    i = pl.program_id(2)
    n_q = pl.num_programs(2)
    is_meta = i == n_q - 1
    slope = slope_ref[0][:, :1]

    @pl.when(i == 0)
    def _():
        d = (lax.broadcasted_iota(jnp.int32, (tk, tq), 0)
             - lax.broadcasted_iota(jnp.int32, (tk, tq), 1)).astype(f32)
        bias_scr[0] = slope * d
        for v in range(n_diag):
            bias_scr[1 + v] = -slope * jnp.abs(d - float(v * tq))
        bias_scr[1 + n_diag] = -slope * d

    q = q_ref[0, 0]
    zero = jnp.zeros((D_QK, tq), bf16)
    qs_scr[0:D_QK, 0:tq] = q[0:D_QK]
    qs_scr[D_QK:, 0:tq] = zero
    qs_scr[0:D_QK, tq:] = zero
    qs_scr[D_QK:, tq:] = q[D_QK:]

    m_scr[...] = jnp.full(m_scr.shape, -jnp.inf, f32)
    l_scr[...] = jnp.zeros(l_scr.shape, f32)
    acc_scr[...] = jnp.zeros(acc_scr.shape, f32)

    q_pos0 = jnp.where(is_meta, 0, N_META + i * tq)
    q_valid = jnp.where(is_meta, N_META, tq)
    n_slab = 2 * tq // LANES
    piece = BF16_ROWS

    def chunk(k, vt, k_pos0):
        n = k.shape[0]
        before = k_pos0 + n <= q_pos0
        after = k_pos0 >= q_pos0 + q_valid
        sel = jnp.where(before, 0, jnp.where(after, 1 + n_diag, 1 + (q_pos0 - k_pos0) // tq))
        shift = jnp.where(before, k_pos0 - q_pos0, jnp.where(after, q_pos0 - k_pos0, 0))
        const = slope * jnp.full((1, 1), shift, jnp.int32).astype(f32)

        s_scr[0:n, :] = jnp.dot(k, qs_scr[...], preferred_element_type=f32)

        def tile(r, slab):
            cols = slice(slab * LANES, (slab + 1) * LANES)
            qcols = slice((slab % (tq // LANES)) * LANES, (slab % (tq // LANES) + 1) * LANES)
            return s_scr[r:r + piece, cols] + bias_scr[sel, r:r + piece, qcols]

        m_old = m_scr[...]
        maxes = []
        for slab in range(n_slab):
            mx = tile(0, slab)
            for r in range(piece, n, piece):
                mx = jnp.maximum(mx, tile(r, slab))
            maxes.append(jnp.max(mx, axis=0, keepdims=True))
        m_new = jnp.maximum(m_old, jnp.concatenate(maxes, axis=1) + const)
        sub = m_new - const
        alpha = jnp.exp(m_old - m_new)

        sums = []
        for slab in range(n_slab):
            cols = slice(slab * LANES, (slab + 1) * LANES)
            sub_b = jnp.broadcast_to(sub[:, cols], (piece, LANES))
            tot = None
            for r in range(0, n, piece):
                p = jnp.exp(tile(r, slab) - sub_b)
                tot = p if tot is None else tot + p
                p_scr[r:r + piece, cols] = p.astype(bf16)
            sums.append(jnp.sum(tot, axis=0, keepdims=True))
        l_scr[...] = alpha * l_scr[...] + jnp.concatenate(sums, axis=1)
        acc_scr[...] = alpha * acc_scr[...] + jnp.dot(vt, p_scr[0:n, :], preferred_element_type=f32)
        m_scr[...] = m_new

    def body(c, carry):
        r0 = pl.multiple_of(c * tk, tk)
        chunk(k_ref[0, pl.ds(r0, tk), :], vt_ref[0, c], N_META + c * tk)
        return carry

    n_chunks = n_body // tk
    lax.fori_loop(0, n_chunks, body, 0)
    chunk(k_ref[0, n_body:n_body + N_META, :], vt_ref[0, n_chunks, :, 0:N_META], 0)

    lv = lam_ref[...]
    lam = (jnp.exp(jnp.sum(lv[0:1] * lv[1:2], axis=-1, keepdims=True))
           - jnp.exp(jnp.sum(lv[2:3] * lv[3:4], axis=-1, keepdims=True)) + lam_init)
    o = acc_scr[...] / l_scr[...]
    att = (o[:, :tq] - lam * o[:, tq:]).T
    o_ref[0] = (_rms(att, gain_ref[...]) * (1.0 - lam_init)).astype(bf16)


def _diff_attention(qvt, k, lam_vecs, gain, slopes, n_body, lam_init):
    b, l, _ = k.shape
    n_t = qvt.shape[1]
    per_chunk = ATTN_TK // ATTN_TQ
    assert ATTN_TK % ATTN_TQ == 0 and n_body % ATTN_TK == 0
    kernel = functools.partial(_attn_kernel, n_body=n_body, lam_init=lam_init)
    return pl.pallas_call(
        kernel,
        grid=(b, HEADS, pl.cdiv(l, ATTN_TQ)),
        in_specs=[
            pl.BlockSpec(lam_vecs.shape, lambda bi, h, i: (0, 0)),
            pl.BlockSpec((1, D_V), lambda bi, h, i: (0, 0)),
            pl.BlockSpec((1, 1, LANES), lambda bi, h, i: (h, 0, 0)),
            pl.BlockSpec((1, 1, 2 * D_QK, ATTN_TQ), lambda bi, h, i: (bi, i // per_chunk, h, i % per_chunk)),
            pl.BlockSpec((1, l, 2 * D_QK), lambda bi, h, i: (bi, 0, h)),
            pl.BlockSpec((1, n_t, D_V, ATTN_TK), lambda bi, h, i: (bi, 0, HEADS + h, 0)),
        ],
        out_specs=pl.BlockSpec((1, ATTN_TQ, D_V), lambda bi, h, i: (bi, i, h)),
        out_shape=jax.ShapeDtypeStruct((b, l, HEADS * D_V), bf16),
        scratch_shapes=[
            pltpu.VMEM((2 * D_QK, 2 * ATTN_TQ), bf16),
            pltpu.VMEM((2 + per_chunk, ATTN_TK, ATTN_TQ), f32),
            pltpu.VMEM((ATTN_TK, 2 * ATTN_TQ), f32),
            pltpu.VMEM((ATTN_TK, 2 * ATTN_TQ), bf16),
            pltpu.VMEM((1, 2 * ATTN_TQ), f32),
            pltpu.VMEM((1, 2 * ATTN_TQ), f32),
            pltpu.VMEM((D_V, 2 * ATTN_TQ), f32),
        ],
        compiler_params=pltpu.CompilerParams(
            dimension_semantics=("arbitrary", "arbitrary", "arbitrary"), vmem_limit_bytes=40 * MIB),
        name="diff_attention",
    )(lam_vecs, gain, slopes, qvt, k, qvt)


def _pool_kernel(u_ref, pw_ref, ps_ref, o_ref, upad, *, n_body):
    l = n_body + N_META
    width = u_ref.shape[-1]
    upad[0:POOL_HALO, :] = jnp.zeros((POOL_HALO, width), f32)
    upad[POOL_HALO:POOL_HALO + N_META, :] = u_ref[0, n_body:l, :]
    upad[POOL_HALO + N_META:POOL_HALO + l, :] = u_ref[0, 0:n_body, :]
    upad[POOL_HALO + l:2 * POOL_HALO + l, :] = jnp.zeros((POOL_HALO, width), f32)

    def rows(pos0, n, out_row0):
        slab = upad[pl.ds(pos0, n + 2 * POOL_HALO), :]
        t = pos0 + lax.broadcasted_iota(jnp.int32, (n, 1), 0)
        for gi, w in enumerate(POOL_WINDOWS):
            cols = slice(gi * LANES, (gi + 1) * LANES)
            total = slab[POOL_HALO - w // 2:POOL_HALO - w // 2 + n, cols]
            for j in range(1, w):
                total = total + slab[POOL_HALO - w // 2 + j:POOL_HALO - w // 2 + j + n, cols]
            cnt = (jnp.minimum(t + w // 2, l) - jnp.maximum(t - w // 2, 0)).astype(f32)
            pooled = total / cnt - slab[POOL_HALO:POOL_HALO + n, cols]
            y = jnp.dot(pooled.astype(bf16), pw_ref[gi], preferred_element_type=f32) * ps_ref[:, cols]
            o_ref[0, pl.ds(out_row0, n), cols] = y.astype(bf16)

    rows(0, N_META, n_body)

    def body(c, carry):
        r0 = pl.multiple_of(c * POOL_CHUNK, POOL_CHUNK)
        rows(N_META + r0, POOL_CHUNK, r0)
        return carry

    lax.fori_loop(0, n_body // POOL_CHUNK, body, 0)


def _multiscale_pool(u, pool_w, pool_scale, n_body):
    b, l, width = u.shape
    kernel = functools.partial(_pool_kernel, n_body=n_body)
    return pl.pallas_call(
        kernel,
        grid=(b,),
        in_specs=[
            pl.BlockSpec((1, l, width), lambda bi: (bi, 0, 0)),
            pl.BlockSpec(pool_w.shape, lambda bi: (0, 0, 0)),
            pl.BlockSpec((1, width), lambda bi: (0, 0)),
        ],
        out_specs=pl.BlockSpec((1, l, width), lambda bi: (bi, 0, 0)),
        out_shape=jax.ShapeDtypeStruct((b, l, width), bf16),
        scratch_shapes=[pltpu.VMEM((l + 2 * POOL_HALO, width), f32)],
        compiler_params=pltpu.CompilerParams(
            dimension_semantics=("arbitrary",), vmem_limit_bytes=48 * MIB),
        name="multiscale_pool",
    )(u, pool_w, pool_scale)


def _outproj_router_kernel(h_ref, a_ref, p_ref, woa_ref, wop_ref, g_ref, rw_ref, h2_ref, hn_ref, aff_ref):
    tm = h_ref.shape[0]
    h2 = (h_ref[...]
          + jnp.dot(a_ref[...], woa_ref[...], preferred_element_type=f32)
          + jnp.dot(p_ref[...], wop_ref[...], preferred_element_type=f32))
    h2_ref[...] = h2
    hn = _rms(h2, g_ref[...])
    logits = jnp.dot(hn.astype(bf16), rw_ref[...], preferred_element_type=f32)
    z = jnp.exp(logits - jnp.max(logits, axis=-1, keepdims=True))
    aff_ref[...] = z / jnp.sum(z, axis=-1, keepdims=True)
    for s in range(hn.shape[1] // LANES):
        hn_ref[pl.ds(s, tm, stride=SUBLANES), :] = hn[:, s * LANES:(s + 1) * LANES]


def _outproj_router(h, a, p, wo_a, wo_p, g, rw):
    m, d = h.shape
    half = a.shape[1]
    n_e = rw.shape[1]
    row = lambda i: (i, 0)
    fixed = lambda i: (0, 0)
    return pl.pallas_call(
        _outproj_router_kernel,
        grid=(pl.cdiv(m, ROW_TILE),),
        in_specs=[
            pl.BlockSpec((ROW_TILE, d), row),
            pl.BlockSpec((ROW_TILE, half), row),
            pl.BlockSpec((ROW_TILE, half), row),
            pl.BlockSpec((half, d), fixed),
            pl.BlockSpec((half, d), fixed),
            pl.BlockSpec((1, d), fixed),
            pl.BlockSpec((d, n_e), fixed),
        ],
        out_specs=[
            pl.BlockSpec((ROW_TILE, d), row),
            pl.BlockSpec((ROW_TILE * SUBLANES, LANES), row),
            pl.BlockSpec((ROW_TILE, n_e), row),
        ],
        out_shape=[
            jax.ShapeDtypeStruct((m, d), f32),
            jax.ShapeDtypeStruct((m * SUBLANES, LANES), f32),
            jax.ShapeDtypeStruct((m, n_e), f32),
        ],
        compiler_params=pltpu.CompilerParams(
            dimension_semantics=("arbitrary",), vmem_limit_bytes=40 * MIB),
        name="outproj_router",
    )(h, a, p, wo_a, wo_p, g, rw)


def _gather_kernel(idx_ref, src_ref, o_ref, tile):
    cp = o_ref.shape[1]

    def group(g, carry):
        for j in range(SUBLANES):
            slot = g * SUBLANES + j
            r = idx_ref[0, 0, slot]
            tile[pl.ds(pl.multiple_of(slot * SUBLANES, SUBLANES), SUBLANES), :] = (
                src_ref[0, pl.ds(pl.multiple_of(r * SUBLANES, SUBLANES), SUBLANES), :])
        return carry

    lax.fori_loop(0, cp // SUBLANES, group, 0)
    for s in range(o_ref.shape[2] // LANES):
        o_ref[0, :, s * LANES:(s + 1) * LANES] = tile[pl.ds(s, cp, stride=SUBLANES), :].astype(bf16)


def _gather_tokens(idx, hn_tiles, d):
    b, n_e, cp = idx.shape
    rows = hn_tiles.shape[1]
    return pl.pallas_call(
        _gather_kernel,
        grid=(b, n_e),
        in_specs=[
            pl.BlockSpec((1, 1, cp), lambda bi, e: (bi * n_e + e, 0, 0), memory_space=pltpu.SMEM),
            pl.BlockSpec((1, rows, LANES), lambda bi, e: (bi, 0, 0)),
        ],
        out_specs=pl.BlockSpec((1, cp, d), lambda bi, e: (e, bi, 0)),
        out_shape=jax.ShapeDtypeStruct((n_e, b * cp, d), bf16),
        scratch_shapes=[pltpu.VMEM((cp * SUBLANES, LANES), f32)],
        compiler_params=pltpu.CompilerParams(
            dimension_semantics=("arbitrary", "arbitrary"), vmem_limit_bytes=48 * MIB),
        name="gather_tokens",
    )(idx.reshape(b * n_e, 1, cp), hn_tiles)


def _moe_kernel(x_ref, gate_ref, wg_ref, wu_ref, wd_ref, y_ref, wg_b, wu_b, wd_b, acc, *, cp):
    f = pl.program_id(1)
    n_f = pl.num_programs(1)
    wg_b[...] = wg_ref[0].astype(bf16)
    wu_b[...] = wu_ref[0].astype(bf16)
    wd_b[...] = wd_ref[0].astype(bf16)

    @pl.when(f == 0)
    def _():
        acc[...] = jnp.zeros(acc.shape, f32)

    n_chunks = acc.shape[0] // cp
    for c in range(n_chunks):
        rows = slice(c * cp, (c + 1) * cp)
        x = x_ref[0, rows, :]
        g = jnp.dot(x, wg_b[...], preferred_element_type=f32)
        u = jnp.dot(x, wu_b[...], preferred_element_type=f32)
        hid = (g * jax.nn.sigmoid(g) * u).astype(bf16)
        acc[rows, :] += jnp.dot(hid, wd_b[...], preferred_element_type=f32)

    @pl.when(f == n_f - 1)
    def _():
        for c in range(n_chunks):
            rows = slice(c * cp, (c + 1) * cp)
            y = acc[rows, :] * gate_ref[0, rows, :]
            for s in range(y.shape[1] // LANES):
                y_ref[0, pl.ds(c * cp * SUBLANES + s, cp, stride=SUBLANES), :] = (
                    y[:, s * LANES:(s + 1) * LANES])


def _expert_ffn(xg, gate, w_gate, w_up, w_down, cp):
    n_e, m, d = xg.shape
    d_ff = w_gate.shape[2]
    kernel = functools.partial(_moe_kernel, cp=cp)
    return pl.pallas_call(
        kernel,
        grid=(n_e, d_ff // FF_TILE),
        in_specs=[
            pl.BlockSpec((1, m, d), lambda e, f: (e, 0, 0)),
            pl.BlockSpec((1, m, 1), lambda e, f: (e, 0, 0)),
            pl.BlockSpec((1, d, FF_TILE), lambda e, f: (e, 0, f)),
            pl.BlockSpec((1, d, FF_TILE), lambda e, f: (e, 0, f)),
            pl.BlockSpec((1, FF_TILE, d), lambda e, f: (e, f, 0)),
        ],
        out_specs=pl.BlockSpec((1, m * SUBLANES, LANES), lambda e, f: (e, 0, 0)),
        out_shape=jax.ShapeDtypeStruct((n_e, m * SUBLANES, LANES), f32),
        scratch_shapes=[
            pltpu.VMEM((d, FF_TILE), bf16),
            pltpu.VMEM((d, FF_TILE), bf16),
            pltpu.VMEM((FF_TILE, d), bf16),
            pltpu.VMEM((m, d), f32),
        ],
        compiler_params=pltpu.CompilerParams(
            dimension_semantics=("arbitrary", "arbitrary"), vmem_limit_bytes=56 * MIB),
        name="expert_ffn",
    )(xg, gate, w_gate, w_up, w_down)


def _combine_kernel(idx_ref, y_ref, h2_ref, g_ref, o_ref, acc, *, n_experts):
    step = pl.program_id(1)
    cp = idx_ref.shape[2]
    tc = o_ref.shape[1]

    @pl.when(step == 0)
    def _():
        acc[...] = jnp.zeros(acc.shape, f32)

    @pl.when(step < n_experts)
    def _():
        def group(g, carry):
            dst = []
            val = []
            for j in range(SUBLANES):
                slot = g * SUBLANES + j
                r = pl.multiple_of(idx_ref[0, 0, slot] * SUBLANES, SUBLANES)
                dst.append(r)
                val.append(acc[pl.ds(r, SUBLANES), :]
                           + y_ref[0, pl.ds(pl.multiple_of(slot * SUBLANES, SUBLANES), SUBLANES), :])
            for r, v in zip(dst, val):
                acc[pl.ds(r, SUBLANES), :] = v
            return carry

        lax.fori_loop(0, cp // SUBLANES, group, 0)

    @pl.when(step >= n_experts)
    def _():
        r0 = pl.multiple_of((step - n_experts) * tc * SUBLANES, SUBLANES)
        moe = jnp.concatenate(
            [acc[pl.ds(r0 + s, tc, stride=SUBLANES), :] for s in range(o_ref.shape[2] // LANES)], axis=-1)
        o_ref[0] = _rms(h2_ref[0] + moe, g_ref[...])


def _combine(idx, y, h2, g, n_body):
    b, n_e, cp = idx.shape
    _, l, d = h2.shape
    n_out = n_body // OUT_CHUNK
    kernel = functools.partial(_combine_kernel, n_experts=n_e)
    expert = lambda s: jnp.minimum(s, n_e - 1)
    chunk = lambda s: jnp.maximum(s - n_e, 0)
    return pl.pallas_call(
        kernel,
        grid=(b, n_e + n_out),
        in_specs=[
            pl.BlockSpec((1, 1, cp), lambda bi, s: (bi * n_e + expert(s), 0, 0), memory_space=pltpu.SMEM),
            pl.BlockSpec((1, cp * SUBLANES, LANES), lambda bi, s: (expert(s), bi, 0)),
            pl.BlockSpec((1, OUT_CHUNK, d), lambda bi, s: (bi, chunk(s), 0)),
            pl.BlockSpec((1, d), lambda bi, s: (0, 0)),
        ],
        out_specs=pl.BlockSpec((1, OUT_CHUNK, d), lambda bi, s: (bi, chunk(s), 0)),
        out_shape=jax.ShapeDtypeStruct((b, n_body, d), f32),
        scratch_shapes=[pltpu.VMEM(((l + 1) * SUBLANES, LANES), f32)],
        compiler_params=pltpu.CompilerParams(
            dimension_semantics=("arbitrary", "arbitrary"), vmem_limit_bytes=48 * MIB),
        name="combine",
    )(idx.reshape(b * n_e, 1, cp), y, h2, g)


def _round_up(n, k):
    return -(-n // k) * k


def kernel(x, meta_tokens, g_mix, w_in, lam_vecs, subln_gain, pool_w, pool_scale, w_o, g_ffn, router_w,
           w_gate, w_up, w_down, g_final):
    b, n_body, d = x.shape
    l = n_body + N_META
    qk = HEADS * D_QK
    attn_width = HEADS * D_V
    depth = g_mix.shape[0]
    cap = CAPACITY_FACTOR * l // N_EXPERTS
    cp = _round_up(cap, BF16_ROWS)

    meta = jnp.broadcast_to(meta_tokens[None].astype(x.dtype), (b, N_META, d))
    h = jnp.concatenate([x, meta], axis=1)
    slopes = jnp.broadcast_to(
        jnp.array([2.0 ** (-8.0 * (i + 1) / HEADS) for i in range(HEADS)], f32)[:, None, None],
        (HEADS, 1, LANES))
    cols = jnp.arange(HEADS)[:, None] * D_QK + jnp.arange(D_QK)[None, :]
    q_cols = jnp.concatenate([cols, qk + cols], axis=1).reshape(-1)
    k_cols = q_cols + 2 * qk
    v_cols = 4 * qk + jnp.arange(attn_width)
    u_cols = jnp.arange(4 * qk + attn_width, w_in.shape[2])

    out = None
    for layer in range(depth):
        lam_init = 0.8 - 0.6 * math.exp(-0.3 * layer)
        w = w_in[layer]
        w_qv_t = jnp.concatenate([w[:, q_cols] * (1.0 / math.sqrt(D_QK)), w[:, v_cols]], axis=1).T.astype(bf16)
        w_ku = jnp.concatenate([w[:, k_cols], w[:, u_cols]], axis=1).astype(bf16)
        qvt, k, u = _norm_inproj(h, g_mix[layer][None, :], w_qv_t, w_ku, 2 * qk)
        a = _diff_attention(qvt, k, lam_vecs[layer].astype(f32), subln_gain[layer][None, :],
                            slopes, n_body, lam_init)
        p = _multiscale_pool(u, pool_w[layer].astype(bf16), pool_scale[layer][None, :], n_body)
        wo = w_o[layer].astype(bf16)
        h2, hn_tiles, aff = _outproj_router(
            h.reshape(b * l, d), a.reshape(b * l, -1), p.reshape(b * l, -1), wo[:attn_width], wo[attn_width:],
            g_ffn[layer][None, :], router_w[layer].astype(bf16))

        aff = aff.reshape(b, l, N_EXPERTS)
        aff_pos = jnp.concatenate([aff[:, n_body:], aff[:, :n_body]], axis=1)
        gate, idx_pos = lax.top_k(jnp.swapaxes(aff_pos, 1, 2), cap)
        idx = jnp.where(idx_pos < N_META, idx_pos + n_body, idx_pos - N_META).astype(jnp.int32)
        pad = ((0, 0), (0, 0), (0, cp - cap))
        idx_gather = jnp.pad(idx, pad)
        idx_scatter = jnp.pad(idx, pad, constant_values=l)
        gate = jnp.pad(gate, pad)
        gate = jnp.transpose(gate, (1, 0, 2)).reshape(N_EXPERTS, b * cp, 1)

        xg = _gather_tokens(idx_gather, hn_tiles.reshape(b, l * SUBLANES, LANES), d)
        y = _expert_ffn(xg, gate, w_gate[layer], w_up[layer], w_down[layer], cp)
        last = layer == depth - 1
        g_out = g_final[None, :] if last else jnp.ones((1, d), f32)
        out = _combine(idx_scatter, y, h2.reshape(b, l, d), g_out, n_body)
    return out
```

```python
import functools
import math

import jax
import jax.numpy as jnp
from jax import lax
from jax.experimental import pallas as pl
from jax.experimental.pallas import tpu as pltpu

N_META = 16
HEADS = 4
D_QK = 64
D_V = 128
POOL_WINDOWS = (2, 4, 8, 16)
POOL_HALO = max(POOL_WINDOWS) // 2
N_EXPERTS = 16
CAPACITY_FACTOR = 2
EPS = 1e-6

LANES = 128
SUBLANES = 8
BF16_ROWS = 16

ROW_TILE = 512
ATTN_TQ = 256
ATTN_TK = 512
POOL_CHUNK = 256
FF_TILE = 256
OUT_CHUNK = 512
MIB = 1024 * 1024

f32 = jnp.float32
bf16 = jnp.bfloat16


def _rms(x, g):
    return x * lax.rsqrt(jnp.mean(x * x, axis=-1, keepdims=True) + EPS) * g


def _norm_inproj_kernel(x_ref, g_ref, wt_ref, w_ref, qvt_ref, k_ref, u_ref, *, seq_len):
    x = x_ref[0]
    valid = seq_len - pl.program_id(1) * x.shape[0]
    x = jnp.where(lax.broadcasted_iota(jnp.int32, x.shape, 0) < valid, x, 0.0)
    n = _rms(x, g_ref[...]).astype(bf16)
    qvt = lax.dot_general(wt_ref[...], n, (((1,), (1,)), ((), ())), preferred_element_type=f32)
    qvt_ref[0, 0] = qvt.astype(bf16)
    ku = jnp.dot(n, w_ref[...], preferred_element_type=f32)
    n_k = k_ref.shape[-1]
    k_ref[0] = ku[:, :n_k].astype(bf16)
    u_ref[0] = ku[:, n_k:]


def _norm_inproj(h, g, w_qv_t, w_ku, n_k):
    b, l, d = h.shape
    n_t = pl.cdiv(l, ATTN_TK)
    n_qv = w_qv_t.shape[0]
    n_u = w_ku.shape[1] - n_k
    return pl.pallas_call(
        functools.partial(_norm_inproj_kernel, seq_len=l),
        grid=(b, n_t),
        in_specs=[
            pl.BlockSpec((1, ATTN_TK, d), lambda bi, i: (bi, i, 0)),
            pl.BlockSpec((1, d), lambda bi, i: (0, 0)),
            pl.BlockSpec(w_qv_t.shape, lambda bi, i: (0, 0)),
            pl.BlockSpec(w_ku.shape, lambda bi, i: (0, 0)),
        ],
        out_specs=[
            pl.BlockSpec((1, 1, n_qv, ATTN_TK), lambda bi, i: (bi, i, 0, 0)),
            pl.BlockSpec((1, ATTN_TK, n_k), lambda bi, i: (bi, i, 0)),
            pl.BlockSpec((1, ATTN_TK, n_u), lambda bi, i: (bi, i, 0)),
        ],
        out_shape=[
            jax.ShapeDtypeStruct((b, n_t, n_qv, ATTN_TK), bf16),
            jax.ShapeDtypeStruct((b, l, n_k), bf16),
            jax.ShapeDtypeStruct((b, l, n_u), f32),
        ],
        compiler_params=pltpu.CompilerParams(
            dimension_semantics=("arbitrary", "arbitrary"), vmem_limit_bytes=40 * MIB),
        name="norm_inproj",
    )(h, g, w_qv_t, w_ku)


def _attn_kernel(lam_ref, gain_ref, slope_ref, q_ref, k_ref, vt_ref, o_ref,
                 qs_scr, bias_scr, s_scr, p_scr, m_scr, l_scr, acc_scr, *, n_body, lam_init):
    tq = q_ref.shape[3]
    tk = ATTN_TK
    n_diag = tk // tq
    i = pl.program_id(2)
    n_q = pl.num_programs(2)
    is_meta = i == n_q - 1
    slope = slope_ref[0][:, :1]

    @pl.when(i == 0)
    def _():
        d = (lax.broadcasted_iota(jnp.int32, (tk, tq), 0)
             - lax.broadcasted_iota(jnp.int32, (tk, tq), 1)).astype(f32)
        bias_scr[0] = slope * d
        for v in range(n_diag):
            bias_scr[1 + v] = -slope * jnp.abs(d - float(v * tq))
        bias_scr[1 + n_diag] = -slope * d

    q = q_ref[0, 0]
    zero = jnp.zeros((D_QK, tq), bf16)
    qs_scr[0:D_QK, 0:tq] = q[0:D_QK]
    qs_scr[D_QK:, 0:tq] = zero
    qs_scr[0:D_QK, tq:] = zero
    qs_scr[D_QK:, tq:] = q[D_QK:]

    m_scr[...] = jnp.full(m_scr.shape, -jnp.inf, f32)
    l_scr[...] = jnp.zeros(l_scr.shape, f32)
    acc_scr[...] = jnp.zeros(acc_scr.shape, f32)

    q_pos0 = jnp.where(is_meta, 0, N_META + i * tq)
    q_valid = jnp.where(is_meta, N_META, tq)
    n_slab = 2 * tq // LANES
    piece = BF16_ROWS

    def chunk(k, vt, k_pos0):
        n = k.shape[0]
        before = k_pos0 + n <= q_pos0
        after = k_pos0 >= q_pos0 + q_valid
        sel = jnp.where(before, 0, jnp.where(after, 1 + n_diag, 1 + (q_pos0 - k_pos0) // tq))
        shift = jnp.where(before, k_pos0 - q_pos0, jnp.where(after, q_pos0 - k_pos0, 0))
        const = slope * jnp.full((1, 1), shift, jnp.int32).astype(f32)

        s_scr[0:n, :] = jnp.dot(k, qs_scr[...], preferred_element_type=f32)

        def tile(r, slab):
            cols = slice(slab * LANES, (slab + 1) * LANES)
            qcols = slice((slab % (tq // LANES)) * LANES, (slab % (tq // LANES) + 1) * LANES)
            return s_scr[r:r + piece, cols] + bias_scr[sel, r:r + piece, qcols]

        m_old = m_scr[...]
        maxes = []
        for slab in range(n_slab):
            mx = tile(0, slab)
            for r in range(piece, n, piece):
                mx = jnp.maximum(mx, tile(r, slab))
            maxes.append(jnp.max(mx, axis=0, keepdims=True))
        m_new = jnp.maximum(m_old, jnp.concatenate(maxes, axis=1) + const)
        sub = m_new - const
        alpha = jnp.exp(m_old - m_new)

        sums = []
        for slab in range(n_slab):
            cols = slice(slab * LANES, (slab + 1) * LANES)
            sub_b = jnp.broadcast_to(sub[:, cols], (piece, LANES))
            tot = None
            for r in range(0, n, piece):
                p = jnp.exp(tile(r, slab) - sub_b)
                tot = p if tot is None else tot + p
                p_scr[r:r + piece, cols] = p.astype(bf16)
            sums.append(jnp.sum(tot, axis=0, keepdims=True))
        l_scr[...] = alpha * l_scr[...] + jnp.concatenate(sums, axis=1)
        acc_scr[...] = alpha * acc_scr[...] + jnp.dot(vt, p_scr[0:n, :], preferred_element_type=f32)
        m_scr[...] = m_new

    def body(c, carry):
        r0 = pl.multiple_of(c * tk, tk)
        chunk(k_ref[0, pl.ds(r0, tk), :], vt_ref[0, c], N_META + c * tk)
        return carry

    n_chunks = n_body // tk
    lax.fori_loop(0, n_chunks, body, 0)
    chunk(k_ref[0, n_body:n_body + N_META, :], vt_ref[0, n_chunks, :, 0:N_META], 0)

    lv = lam_ref[...]
    lam = (jnp.exp(jnp.sum(lv[0:1] * lv[1:2], axis=-1, keepdims=True))
           - jnp.exp(jnp.sum(lv[2:3] * lv[3:4], axis=-1, keepdims=True)) + lam_init)
    o = acc_scr[...] / l_scr[...]
    att = (o[:, :tq] - lam * o[:, tq:]).T
    o_ref[0] = (_rms(att, gain_ref[...]) * (1.0 - lam_init)).astype(bf16)


def _diff_attention(qvt, k, lam_vecs, gain, slopes, n_body, lam_init):
    b, l, _ = k.shape
    n_t = qvt.shape[1]
    per_chunk = ATTN_TK // ATTN_TQ
    assert ATTN_TK % ATTN_TQ == 0 and n_body % ATTN_TK == 0
    kernel = functools.partial(_attn_kernel, n_body=n_body, lam_init=lam_init)
    return pl.pallas_call(
        kernel,
        grid=(b, HEADS, pl.cdiv(l, ATTN_TQ)),
        in_specs=[
            pl.BlockSpec(lam_vecs.shape, lambda bi, h, i: (0, 0)),
            pl.BlockSpec((1, D_V), lambda bi, h, i: (0, 0)),
            pl.BlockSpec((1, 1, LANES), lambda bi, h, i: (h, 0, 0)),
            pl.BlockSpec((1, 1, 2 * D_QK, ATTN_TQ), lambda bi, h, i: (bi, i // per_chunk, h, i % per_chunk)),
            pl.BlockSpec((1, l, 2 * D_QK), lambda bi, h, i: (bi, 0, h)),
            pl.BlockSpec((1, n_t, D_V, ATTN_TK), lambda bi, h, i: (bi, 0, HEADS + h, 0)),
        ],
        out_specs=pl.BlockSpec((1, ATTN_TQ, D_V), lambda bi, h, i: (bi, i, h)),
        out_shape=jax.ShapeDtypeStruct((b, l, HEADS * D_V), bf16),
        scratch_shapes=[
            pltpu.VMEM((2 * D_QK, 2 * ATTN_TQ), bf16),
            pltpu.VMEM((2 + per_chunk, ATTN_TK, ATTN_TQ), f32),
            pltpu.VMEM((ATTN_TK, 2 * ATTN_TQ), f32),
            pltpu.VMEM((ATTN_TK, 2 * ATTN_TQ), bf16),
            pltpu.VMEM((1, 2 * ATTN_TQ), f32),
            pltpu.VMEM((1, 2 * ATTN_TQ), f32),
            pltpu.VMEM((D_V, 2 * ATTN_TQ), f32),
        ],
        compiler_params=pltpu.CompilerParams(
            dimension_semantics=("arbitrary", "arbitrary", "arbitrary"), vmem_limit_bytes=40 * MIB),
        name="diff_attention",
    )(lam_vecs, gain, slopes, qvt, k, qvt)


def _pool_kernel(u_ref, pw_ref, ps_ref, o_ref, upad, *, n_body):
    l = n_body + N_META
    width = u_ref.shape[-1]
    upad[0:POOL_HALO, :] = jnp.zeros((POOL_HALO, width), f32)
    upad[POOL_HALO:POOL_HALO + N_META, :] = u_ref[0, n_body:l, :]
    upad[POOL_HALO + N_META:POOL_HALO + l, :] = u_ref[0, 0:n_body, :]
    upad[POOL_HALO + l:2 * POOL_HALO + l, :] = jnp.zeros((POOL_HALO, width), f32)

    def rows(pos0, n, out_row0):
        slab = upad[pl.ds(pos0, n + 2 * POOL_HALO), :]
        t = pos0 + lax.broadcasted_iota(jnp.int32, (n, 1), 0)
        for gi, w in enumerate(POOL_WINDOWS):
            cols = slice(gi * LANES, (gi + 1) * LANES)
            total = slab[POOL_HALO - w // 2:POOL_HALO - w // 2 + n, cols]
            for j in range(1, w):
                total = total + slab[POOL_HALO - w // 2 + j:POOL_HALO - w // 2 + j + n, cols]
            cnt = (jnp.minimum(t + w // 2, l) - jnp.maximum(t - w // 2, 0)).astype(f32)
            pooled = total / cnt - slab[POOL_HALO:POOL_HALO + n, cols]
            y = jnp.dot(pooled.astype(bf16), pw_ref[gi], preferred_element_type=f32) * ps_ref[:, cols]
            o_ref[0, pl.ds(out_row0, n), cols] = y.astype(bf16)

    rows(0, N_META, n_body)

    def body(c, carry):
        r0 = pl.multiple_of(c * POOL_CHUNK, POOL_CHUNK)
        rows(N_META + r0, POOL_CHUNK, r0)
        return carry

    lax.fori_loop(0, n_body // POOL_CHUNK, body, 0)


def _multiscale_pool(u, pool_w, pool_scale, n_body):
    b, l, width = u.shape
    kernel = functools.partial(_pool_kernel, n_body=n_body)
    return pl.pallas_call(
        kernel,
        grid=(b,),
        in_specs=[
            pl.BlockSpec((1, l, width), lambda bi: (bi, 0, 0)),
            pl.BlockSpec(pool_w.shape, lambda bi: (0, 0, 0)),
            pl.BlockSpec((1, width), lambda bi: (0, 0)),
        ],
        out_specs=pl.BlockSpec((1, l, width), lambda bi: (bi, 0, 0)),
        out_shape=jax.ShapeDtypeStruct((b, l, width), bf16),
        scratch_shapes=[pltpu.VMEM((l + 2 * POOL_HALO, width), f32)],
        compiler_params=pltpu.CompilerParams(
            dimension_semantics=("arbitrary",), vmem_limit_bytes=48 * MIB),
        name="multiscale_pool",
    )(u, pool_w, pool_scale)


def _outproj_router_kernel(h_ref, a_ref, p_ref, woa_ref, wop_ref, g_ref, rw_ref, h2_ref, hn_ref, aff_ref):
    tm = h_ref.shape[0]
    h2 = (h_ref[...]
          + jnp.dot(a_ref[...], woa_ref[...], preferred_element_type=f32)
          + jnp.dot(p_ref[...], wop_ref[...], preferred_element_type=f32))
    h2_ref[...] = h2
    hn = _rms(h2, g_ref[...])
    logits = jnp.dot(hn.astype(bf16), rw_ref[...], preferred_element_type=f32)
    lane = lax.broadcasted_iota(jnp.int32, logits.shape, 1)
    logits = jnp.where(lane < N_EXPERTS, logits, -jnp.inf)
    z = jnp.exp(logits - jnp.max(logits, axis=-1, keepdims=True))
    aff_ref[...] = z / jnp.sum(z, axis=-1, keepdims=True)
    for s in range(hn.shape[1] // LANES):
        hn_ref[pl.ds(s, tm, stride=SUBLANES), :] = hn[:, s * LANES:(s + 1) * LANES]


def _outproj_router(h, a, p, wo_a, wo_p, g, rw):
    m, d = h.shape
    half = a.shape[1]
    n_e = rw.shape[1]
    row = lambda i: (i, 0)
    fixed = lambda i: (0, 0)
    return pl.pallas_call(
        _outproj_router_kernel,
        grid=(pl.cdiv(m, ROW_TILE),),
        in_specs=[
            pl.BlockSpec((ROW_TILE, d), row),
            pl.BlockSpec((ROW_TILE, half), row),
            pl.BlockSpec((ROW_TILE, half), row),
            pl.BlockSpec((half, d), fixed),
            pl.BlockSpec((half, d), fixed),
            pl.BlockSpec((1, d), fixed),
            pl.BlockSpec((d, n_e), fixed),
        ],
        out_specs=[
            pl.BlockSpec((ROW_TILE, d), row),
            pl.BlockSpec((ROW_TILE * SUBLANES, LANES), row),
            pl.BlockSpec((ROW_TILE, n_e), row),
        ],
        out_shape=[
            jax.ShapeDtypeStruct((m, d), f32),
            jax.ShapeDtypeStruct((m * SUBLANES, LANES), f32),
            jax.ShapeDtypeStruct((m, n_e), f32),
        ],
        compiler_params=pltpu.CompilerParams(
            dimension_semantics=("arbitrary",), vmem_limit_bytes=40 * MIB),
        name="outproj_router",
    )(h, a, p, wo_a, wo_p, g, rw)


def _prefix_count(flags, w_scan):
    out = []
    offset = jnp.zeros((flags.shape[0], LANES), f32)
    for blk in range(flags.shape[1] // LANES):
        r = jnp.dot(flags[:, blk * LANES:(blk + 1) * LANES].astype(bf16), w_scan, preferred_element_type=f32)
        out.append(r[:, :LANES] + offset)
        offset = offset + r[:, LANES:]
    return jnp.concatenate(out, axis=1)


def _topk_kernel(aff_ref, idx_ref, rank_scr, chosen_scr, *, cap, n_body):
    n_e, lp = aff_ref.shape[1], aff_ref.shape[2]
    cp = idx_ref.shape[1]
    n_blk = lp // LANES
    bits = lax.bitcast_convert_type(aff_ref[0], jnp.int32)

    def refine(step, thr):
        cand = thr | (1 << (30 - step))
        cnt = jnp.sum(jnp.where(bits >= cand, 1, 0), axis=1, keepdims=True)
        return jnp.where(cnt >= cap, cand, thr)

    thr = lax.fori_loop(0, 31, refine, jnp.zeros((n_e, 1), jnp.int32))
    aff = aff_ref[0]
    pivot = jnp.max(jnp.where(bits == thr, aff, -1.0), axis=1, keepdims=True)
    above = aff > pivot
    tie = aff == pivot
    need = (cap - jnp.sum(jnp.where(above, 1, 0), axis=1, keepdims=True)).astype(f32)

    row = lax.broadcasted_iota(jnp.int32, (LANES, 2 * LANES), 0)
    col = lax.broadcasted_iota(jnp.int32, (LANES, 2 * LANES), 1)
    w_scan = jnp.where((row <= col) | (col >= LANES), 1.0, 0.0).astype(bf16)

    tie_rank = _prefix_count(jnp.where(tie, 1.0, 0.0), w_scan)
    chosen_scr[...] = jnp.where(above | (tie & (tie_rank <= need)), 1.0, 0.0)
    pos_f = lax.broadcasted_iota(jnp.int32, (n_e, lp), 1).astype(f32)
    big = float(2 * lp)

    def excess(cnt):
        return jnp.sum(jnp.abs(cnt - float(cap))) > 0.0

    def repair(cnt):
        a = aff_ref[0]
        sel = chosen_scr[...] > 0.5
        lo = jnp.min(jnp.where(sel, a, jnp.inf), axis=1, keepdims=True)
        lo_at = jnp.max(jnp.where(sel & (a == lo), pos_f, -1.0), axis=1, keepdims=True)
        drop = (pos_f == lo_at) & (cnt > float(cap))
        free = (~sel) & (a >= 0.0)
        hi = jnp.max(jnp.where(free, a, -1.0), axis=1, keepdims=True)
        hi_at = jnp.min(jnp.where(free & (a == hi), pos_f, big), axis=1, keepdims=True)
        add = (pos_f == hi_at) & (cnt < float(cap))
        new = jnp.where(add, 1.0, jnp.where(drop, 0.0, chosen_scr[...]))
        chosen_scr[...] = new
        return jnp.sum(new, axis=1, keepdims=True)

    lax.while_loop(excess, repair, jnp.sum(chosen_scr[...], axis=1, keepdims=True))
    rank = _prefix_count(chosen_scr[...], w_scan)
    for e in range(n_e):
        rank_scr[e] = rank[e:e + 1, :]

    lane = lax.broadcasted_iota(jnp.int32, (SUBLANES, LANES), 1)
    sub = lax.broadcasted_iota(jnp.int32, (SUBLANES, LANES), 0)

    n_groups = cp // SUBLANES
    unroll = next(u for u in (11, 8, 6, 5, 4, 3, 2, 1) if n_groups % u == 0)

    def per_expert(e, carry):
        def per_groups(gi, carry2):
            slots = [((gi * unroll + u) * SUBLANES + sub).astype(f32) for u in range(unroll)]
            accs = [jnp.zeros((SUBLANES, LANES), f32) for _ in range(unroll)]
            for t in range(n_blk):
                r_blk = jnp.broadcast_to(rank_scr[e, :, t * LANES:(t + 1) * LANES], (SUBLANES, LANES))
                accs = [a + jnp.where(r_blk <= c, 1.0, 0.0) for a, c in zip(accs, slots)]
            for u in range(unroll):
                pos = jnp.sum(accs[u], axis=1, keepdims=True).astype(jnp.int32)
                rows = jnp.where(pos < N_META, pos + n_body, pos - N_META)
                at = pl.ds(pl.multiple_of((gi * unroll + u) * SUBLANES, SUBLANES), SUBLANES)
                idx_ref[0, at, :] = jnp.where(lane == e, rows, idx_ref[0, at, :])
            return carry2

        return lax.fori_loop(0, n_groups // unroll, per_groups, carry)

    idx_ref[0] = jnp.zeros((cp, LANES), jnp.int32)
    lax.fori_loop(0, n_e, per_expert, 0)


def _topk_rows(aff_pos, cap, cp, n_body):
    b, n_e, lp = aff_pos.shape
    kernel = functools.partial(_topk_kernel, cap=cap, n_body=n_body)
    return pl.pallas_call(
        kernel,
        grid=(b,),
        in_specs=[pl.BlockSpec((1, n_e, lp), lambda bi: (bi, 0, 0))],
        out_specs=pl.BlockSpec((1, cp, LANES), lambda bi: (bi, 0, 0)),
        out_shape=jax.ShapeDtypeStruct((b, cp, LANES), jnp.int32),
        scratch_shapes=[pltpu.VMEM((n_e, 1, lp), f32), pltpu.VMEM((n_e, lp), f32)],
        compiler_params=pltpu.CompilerParams(
            dimension_semantics=("arbitrary",), vmem_limit_bytes=32 * MIB),
        name="topk_rows",
    )(aff_pos)


def _gather_kernel(idx_ref, src_ref, aff_ref, o_ref, gate_ref, tile):
    cp = o_ref.shape[1]
    sub = lax.broadcasted_iota(jnp.int32, (SUBLANES, LANES), 0)

    def group(g, carry):
        gates = jnp.zeros((SUBLANES, LANES), f32)
        for j in range(SUBLANES):
            slot = g * SUBLANES + j
            r = idx_ref[0, 0, slot]
            tile[pl.ds(pl.multiple_of(slot * SUBLANES, SUBLANES), SUBLANES), :] = (
                src_ref[0, pl.ds(pl.multiple_of(r * SUBLANES, SUBLANES), SUBLANES), :])
            aff8 = aff_ref[0, pl.ds(pl.multiple_of((r >> 3) << 3, SUBLANES), SUBLANES), :]
            gates = jnp.where(sub == j, pltpu.roll(aff8, (j + SUBLANES - (r & 7)) & 7, axis=0), gates)
        gate_ref[0, pl.ds(pl.multiple_of(g * SUBLANES, SUBLANES), SUBLANES), :] = gates
        return carry

    lax.fori_loop(0, cp // SUBLANES, group, 0)
    for s in range(o_ref.shape[2] // LANES):
        o_ref[0, :, s * LANES:(s + 1) * LANES] = tile[pl.ds(s, cp, stride=SUBLANES), :].astype(bf16)


def _gather_tokens(idx, hn_tiles, aff, d):
    b, n_e, cp = idx.shape
    rows = hn_tiles.shape[1]
    l = aff.shape[1]
    return pl.pallas_call(
        _gather_kernel,
        grid=(b, n_e),
        in_specs=[
            pl.BlockSpec((1, 1, cp), lambda bi, e: (bi * n_e + e, 0, 0), memory_space=pltpu.SMEM),
            pl.BlockSpec((1, rows, LANES), lambda bi, e: (bi, 0, 0)),
            pl.BlockSpec((1, l, LANES), lambda bi, e: (bi, 0, 0)),
        ],
        out_specs=[
            pl.BlockSpec((1, cp, d), lambda bi, e: (e, bi, 0)),
            pl.BlockSpec((1, cp, LANES), lambda bi, e: (e, bi, 0)),
        ],
        out_shape=[
            jax.ShapeDtypeStruct((n_e, b * cp, d), bf16),
            jax.ShapeDtypeStruct((n_e, b * cp, LANES), f32),
        ],
        scratch_shapes=[pltpu.VMEM((cp * SUBLANES, LANES), f32)],
        compiler_params=pltpu.CompilerParams(
            dimension_semantics=("arbitrary", "arbitrary"), vmem_limit_bytes=48 * MIB),
        name="gather_tokens",
    )(idx.reshape(b * n_e, 1, cp), hn_tiles, aff)


def _moe_kernel(x_ref, gate_ref, wg_ref, wu_ref, wd_ref, y_ref, wg_b, wu_b, wd_b, acc, *, cp):
    f = pl.program_id(1)
    n_f = pl.num_programs(1)
    wg_b[...] = wg_ref[0].astype(bf16)
    wu_b[...] = wu_ref[0].astype(bf16)
    wd_b[...] = wd_ref[0].astype(bf16)

    @pl.when(f == 0)
    def _():
        acc[...] = jnp.zeros(acc.shape, f32)

    n_chunks = acc.shape[0] // cp
    for c in range(n_chunks):
        rows = slice(c * cp, (c + 1) * cp)
        x = x_ref[0, rows, :]
        g = jnp.dot(x, wg_b[...], preferred_element_type=f32)
        u = jnp.dot(x, wu_b[...], preferred_element_type=f32)
        hid = (g * jax.nn.sigmoid(g) * u).astype(bf16)
        acc[rows, :] += jnp.dot(hid, wd_b[...], preferred_element_type=f32)

    @pl.when(f == n_f - 1)
    def _():
        for c in range(n_chunks):
            rows = slice(c * cp, (c + 1) * cp)
            lane = lax.broadcasted_iota(jnp.int32, (cp, LANES), 1)
            gate = jnp.sum(jnp.where(lane == pl.program_id(0), gate_ref[0, rows, :], 0.0),
                           axis=1, keepdims=True)
            y = acc[rows, :] * gate
            for s in range(y.shape[1] // LANES):
                y_ref[0, pl.ds(c * cp * SUBLANES + s, cp, stride=SUBLANES), :] = (
                    y[:, s * LANES:(s + 1) * LANES])


def _expert_ffn(xg, gate, w_gate, w_up, w_down, cp):
    n_e, m, d = xg.shape
    d_ff = w_gate.shape[2]
    kernel = functools.partial(_moe_kernel, cp=cp)
    return pl.pallas_call(
        kernel,
        grid=(n_e, d_ff // FF_TILE),
        in_specs=[
            pl.BlockSpec((1, m, d), lambda e, f: (e, 0, 0)),
            pl.BlockSpec((1, m, LANES), lambda e, f: (e, 0, 0)),
            pl.BlockSpec((1, d, FF_TILE), lambda e, f: (e, 0, f)),
            pl.BlockSpec((1, d, FF_TILE), lambda e, f: (e, 0, f)),
            pl.BlockSpec((1, FF_TILE, d), lambda e, f: (e, f, 0)),
        ],
        out_specs=pl.BlockSpec((1, m * SUBLANES, LANES), lambda e, f: (e, 0, 0)),
        out_shape=jax.ShapeDtypeStruct((n_e, m * SUBLANES, LANES), f32),
        scratch_shapes=[
            pltpu.VMEM((d, FF_TILE), bf16),
            pltpu.VMEM((d, FF_TILE), bf16),
            pltpu.VMEM((FF_TILE, d), bf16),
            pltpu.VMEM((m, d), f32),
        ],
        compiler_params=pltpu.CompilerParams(
            dimension_semantics=("arbitrary", "arbitrary"), vmem_limit_bytes=56 * MIB),
        name="expert_ffn",
    )(xg, gate, w_gate, w_up, w_down)


def _combine_kernel(idx_ref, y_ref, h2_ref, g_ref, o_ref, acc, *, n_experts):
    step = pl.program_id(1)
    cp = idx_ref.shape[2]
    tc = o_ref.shape[1]

    @pl.when(step == 0)
    def _():
        acc[...] = jnp.zeros(acc.shape, f32)

    @pl.when(step < n_experts)
    def _():
        def group(g, carry):
            dst = []
            val = []
            for j in range(SUBLANES):
                slot = g * SUBLANES + j
                r = pl.multiple_of(idx_ref[0, 0, slot] * SUBLANES, SUBLANES)
                dst.append(r)
                val.append(acc[pl.ds(r, SUBLANES), :]
                           + y_ref[0, pl.ds(pl.multiple_of(slot * SUBLANES, SUBLANES), SUBLANES), :])
            for r, v in zip(dst, val):
                acc[pl.ds(r, SUBLANES), :] = v
            return carry

        lax.fori_loop(0, cp // SUBLANES, group, 0)

    @pl.when(step >= n_experts)
    def _():
        r0 = pl.multiple_of((step - n_experts) * tc * SUBLANES, SUBLANES)
        moe = jnp.concatenate(
            [acc[pl.ds(r0 + s, tc, stride=SUBLANES), :] for s in range(o_ref.shape[2] // LANES)], axis=-1)
        o_ref[0] = _rms(h2_ref[0] + moe, g_ref[...])


def _combine(idx, y, h2, g, n_body):
    b, n_e, cp = idx.shape
    _, l, d = h2.shape
    n_out = n_body // OUT_CHUNK
    kernel = functools.partial(_combine_kernel, n_experts=n_e)
    expert = lambda s: jnp.minimum(s, n_e - 1)
    chunk = lambda s: jnp.maximum(s - n_e, 0)
    return pl.pallas_call(
        kernel,
        grid=(b, n_e + n_out),
        in_specs=[
            pl.BlockSpec((1, 1, cp), lambda bi, s: (bi * n_e + expert(s), 0, 0), memory_space=pltpu.SMEM),
            pl.BlockSpec((1, cp * SUBLANES, LANES), lambda bi, s: (expert(s), bi, 0)),
            pl.BlockSpec((1, OUT_CHUNK, d), lambda bi, s: (bi, chunk(s), 0)),
            pl.BlockSpec((1, d), lambda bi, s: (0, 0)),
        ],
        out_specs=pl.BlockSpec((1, OUT_CHUNK, d), lambda bi, s: (bi, chunk(s), 0)),
        out_shape=jax.ShapeDtypeStruct((b, n_body, d), f32),
        scratch_shapes=[pltpu.VMEM(((l + 1) * SUBLANES, LANES), f32)],
        compiler_params=pltpu.CompilerParams(
            dimension_semantics=("arbitrary", "arbitrary"), vmem_limit_bytes=48 * MIB),
        name="combine",
    )(idx.reshape(b * n_e, 1, cp), y, h2, g)


def _round_up(n, k):
    return -(-n // k) * k


def kernel(x, meta_tokens, g_mix, w_in, lam_vecs, subln_gain, pool_w, pool_scale, w_o, g_ffn, router_w,
           w_gate, w_up, w_down, g_final):
    b, n_body, d = x.shape
    l = n_body + N_META
    qk = HEADS * D_QK
    attn_width = HEADS * D_V
    depth = g_mix.shape[0]
    cap = CAPACITY_FACTOR * l // N_EXPERTS
    cp = _round_up(cap, BF16_ROWS)

    meta = jnp.broadcast_to(meta_tokens[None].astype(x.dtype), (b, N_META, d))
    h = jnp.concatenate([x, meta], axis=1)
    slopes = jnp.broadcast_to(
        jnp.array([2.0 ** (-8.0 * (i + 1) / HEADS) for i in range(HEADS)], f32)[:, None, None],
        (HEADS, 1, LANES))
    cols = jnp.arange(HEADS)[:, None] * D_QK + jnp.arange(D_QK)[None, :]
    q_cols = jnp.concatenate([cols, qk + cols], axis=1).reshape(-1)
    k_cols = q_cols + 2 * qk
    v_cols = 4 * qk + jnp.arange(attn_width)
    u_cols = jnp.arange(4 * qk + attn_width, w_in.shape[2])

    out = None
    for layer in range(depth):
        lam_init = 0.8 - 0.6 * math.exp(-0.3 * layer)
        w = w_in[layer]
        w_qv_t = jnp.concatenate([w[:, q_cols] * (1.0 / math.sqrt(D_QK)), w[:, v_cols]], axis=1).T.astype(bf16)
        w_ku = jnp.concatenate([w[:, k_cols], w[:, u_cols]], axis=1).astype(bf16)
        qvt, k, u = _norm_inproj(h, g_mix[layer][None, :], w_qv_t, w_ku, 2 * qk)
        a = _diff_attention(qvt, k, lam_vecs[layer].astype(f32), subln_gain[layer][None, :],
                            slopes, n_body, lam_init)
        p = _multiscale_pool(u, pool_w[layer].astype(bf16), pool_scale[layer][None, :], n_body)
        wo = w_o[layer].astype(bf16)
        h2, hn_tiles, aff = _outproj_router(
            h.reshape(b * l, d), a.reshape(b * l, -1), p.reshape(b * l, -1), wo[:attn_width], wo[attn_width:],
            g_ffn[layer][None, :],
            jnp.pad(router_w[layer], ((0, 0), (0, LANES - N_EXPERTS))).astype(bf16))

        aff = aff.reshape(b, l, LANES)
        lp = _round_up(l, LANES)
        aff_pos = jnp.concatenate(
            [aff[:, n_body:, :N_EXPERTS], aff[:, :n_body, :N_EXPERTS], jnp.full((b, lp - l, N_EXPERTS), -1.0, f32)],
            axis=1)
        idx = _topk_rows(jnp.swapaxes(aff_pos, 1, 2), cap, cp, n_body)
        idx = jnp.swapaxes(idx[:, :, :N_EXPERTS], 1, 2)
        real = jnp.arange(cp)[None, None, :] < cap
        idx_gather = jnp.where(real, idx, 0)
        idx_scatter = jnp.where(real, idx, l)

        xg, gate = _gather_tokens(idx_gather, hn_tiles.reshape(b, l * SUBLANES, LANES), aff, d)
        y = _expert_ffn(xg, gate, w_gate[layer], w_up[layer], w_down[layer], cp)
        last = layer == depth - 1
        g_out = g_final[None, :] if last else jnp.ones((1, d), f32)
        out = _combine(idx_scatter, y, h2.reshape(b, l, d), g_out, n_body)
    return out
```

```python
import functools
import math

import jax
import jax.numpy as jnp
from jax import lax
from jax.experimental import pallas as pl
from jax.experimental.pallas import tpu as pltpu

N_META = 16
HEADS = 4
D_QK = 64
D_V = 128
POOL_WINDOWS = (2, 4, 8, 16)
POOL_HALO = max(POOL_WINDOWS) // 2
N_EXPERTS = 16
CAPACITY_FACTOR = 2
EPS = 1e-6

LANES = 128
SUBLANES = 8
BF16_ROWS = 16

ROW_TILE = 512
ATTN_TQ = 256
ATTN_TK = 512
POOL_CHUNK = 256
FF_TILE = 256
OUT_CHUNK = 512
MIB = 1024 * 1024

f32 = jnp.float32
bf16 = jnp.bfloat16


def _rms(x, g):
    return x * lax.rsqrt(jnp.mean(x * x, axis=-1, keepdims=True) + EPS) * g


def _norm_inproj_kernel(x_ref, g_ref, wt_ref, w_ref, qvt_ref, k_ref, u_ref, *, seq_len):
    x = x_ref[0]
    valid = seq_len - pl.program_id(1) * x.shape[0]
    x = jnp.where(lax.broadcasted_iota(jnp.int32, x.shape, 0) < valid, x, 0.0)
    n = _rms(x, g_ref[...]).astype(bf16)
    qvt = lax.dot_general(wt_ref[...], n, (((1,), (1,)), ((), ())), preferred_element_type=f32)
    qvt_ref[0, 0] = qvt.astype(bf16)
    ku = jnp.dot(n, w_ref[...], preferred_element_type=f32)
    n_k = k_ref.shape[-1]
    k_ref[0] = ku[:, :n_k].astype(bf16)
    u_ref[0] = ku[:, n_k:]


def _norm_inproj(h, g, w_qv_t, w_ku, n_k):
    b, l, d = h.shape
    n_t = pl.cdiv(l, ATTN_TK)
    n_qv = w_qv_t.shape[0]
    n_u = w_ku.shape[1] - n_k
    return pl.pallas_call(
        functools.partial(_norm_inproj_kernel, seq_len=l),
        grid=(b, n_t),
        in_specs=[
            pl.BlockSpec((1, ATTN_TK, d), lambda bi, i: (bi, i, 0)),
            pl.BlockSpec((1, d), lambda bi, i: (0, 0)),
            pl.BlockSpec(w_qv_t.shape, lambda bi, i: (0, 0)),
            pl.BlockSpec(w_ku.shape, lambda bi, i: (0, 0)),
        ],
        out_specs=[
            pl.BlockSpec((1, 1, n_qv, ATTN_TK), lambda bi, i: (bi, i, 0, 0)),
            pl.BlockSpec((1, ATTN_TK, n_k), lambda bi, i: (bi, i, 0)),
            pl.BlockSpec((1, ATTN_TK, n_u), lambda bi, i: (bi, i, 0)),
        ],
        out_shape=[
            jax.ShapeDtypeStruct((b, n_t, n_qv, ATTN_TK), bf16),
            jax.ShapeDtypeStruct((b, l, n_k), bf16),
            jax.ShapeDtypeStruct((b, l, n_u), f32),
        ],
        compiler_params=pltpu.CompilerParams(
            dimension_semantics=("arbitrary", "arbitrary"), vmem_limit_bytes=40 * MIB),
        name="norm_inproj",
    )(h, g, w_qv_t, w_ku)


def _attn_kernel(lam_ref, gain_ref, slope_ref, q_ref, qe_ref, k_ref, ke_ref, vt_ref, o_ref,
                 qs_scr, bias_scr, tail_scr, *scr, n_body, lam_init):
    tq = q_ref.shape[3]
    tk = ATTN_TK
    n_diag = tk // tq
    i = pl.program_id(2)
    n_q = pl.num_programs(2)
    is_meta = i == n_q - 1
    slope = slope_ref[0][:, :1]

    @pl.when(i == 0)
    def _():
        d = (lax.broadcasted_iota(jnp.int32, (tk, tq), 0)
             - lax.broadcasted_iota(jnp.int32, (tk, tq), 1)).astype(f32)
        for v in range(n_diag):
            bias_scr[v] = -slope * jnp.abs(d - float(v * tq))
        tail_scr[0] = slope * d[0:N_META]
        tail_scr[1] = -slope * jnp.abs(d[0:N_META])
        qs_scr[:, 2 * D_QK + BF16_ROWS:, :] = jnp.zeros((3, 2 * LANES - 2 * D_QK - BF16_ROWS, 2 * tq), bf16)
        qs_scr[2, 2 * D_QK:2 * D_QK + BF16_ROWS, :] = jnp.zeros((BF16_ROWS, 2 * tq), bf16)

    q = q_ref[0, 0]
    zero = jnp.zeros((D_QK, tq), bf16)
    top = jnp.concatenate([q[0:D_QK], zero], axis=1)
    bottom = jnp.concatenate([zero, q[D_QK:]], axis=1)
    for v in range(3):
        qs_scr[v, 0:D_QK, :] = top
        qs_scr[v, D_QK:2 * D_QK, :] = bottom
    for v in range(2):
        qs_scr[v, 2 * D_QK:2 * D_QK + BF16_ROWS, :] = jnp.concatenate([qe_ref[0, v], qe_ref[0, v]], axis=1)

    n_streams = 2
    per_stream = len(scr) // n_streams
    streams = [scr[st * per_stream:(st + 1) * per_stream] for st in range(n_streams)]
    for _, _, _, _, _, _, m_scr, l_scr, acc_scr in streams:
        m_scr[...] = jnp.full(m_scr.shape, -jnp.inf, f32)
        l_scr[...] = jnp.zeros(l_scr.shape, f32)
        acc_scr[...] = jnp.zeros(acc_scr.shape, f32)

    q_pos0 = jnp.where(is_meta, 0, N_META + i * tq)
    q_valid = jnp.where(is_meta, N_META, tq)
    n_slab = tq // LANES
    piece = BF16_ROWS

    def issue_scores(k, version, slot):
        n = k.shape[0]
        for st, stream in enumerate(streams):
            stream[slot][0:n, :] = jnp.dot(k, qs_scr[version, 0:k.shape[1], st * tq:(st + 1) * tq],
                                           preferred_element_type=f32)

    def softmax(slot, n, bias_tile=None, const=None):
        def scores(s_scr, r, slab):
            t = s_scr[r:r + piece, slab * LANES:(slab + 1) * LANES]
            return t if bias_tile is None else t + bias_tile(r, slab)

        stats = []
        for stream in streams:
            s_scr, m_scr = stream[slot], stream[6]
            m_old = m_scr[...]
            maxes = []
            for slab in range(n_slab):
                mx = scores(s_scr, 0, slab)
                for r in range(piece, n, piece):
                    mx = jnp.maximum(mx, scores(s_scr, r, slab))
                maxes.append(jnp.max(mx, axis=0, keepdims=True))
            m_loc = jnp.concatenate(maxes, axis=1)
            m_new = jnp.maximum(m_old, m_loc if const is None else m_loc + const)
            stats.append((m_new, m_new if const is None else m_new - const, jnp.exp(m_old - m_new)))

        for stream, (m_new, sub, alpha) in zip(streams, stats):
            s_scr, p_scr, a_scr, m_scr, l_scr = stream[slot], stream[2 + slot], stream[4 + slot], stream[6], stream[7]
            sums = []
            for slab in range(n_slab):
                cols = slice(slab * LANES, (slab + 1) * LANES)
                sub_b = jnp.broadcast_to(sub[:, cols], (piece, LANES))
                tot = None
                for r in range(0, n, piece):
                    p = jnp.exp(scores(s_scr, r, slab) - sub_b)
                    tot = p if tot is None else tot + p
                    p_scr[r:r + piece, cols] = p.astype(bf16)
                sums.append(jnp.sum(tot, axis=0, keepdims=True))
            l_scr[...] = alpha * l_scr[...] + jnp.concatenate(sums, axis=1)
            a_scr[...] = alpha
            m_scr[...] = m_new

    def apply_values(vt, slot):
        n = vt.shape[1]
        for stream in streams:
            p_scr, a_scr, acc_scr = stream[2 + slot], stream[4 + slot], stream[8]
            acc_scr[...] = a_scr[...] * acc_scr[...] + jnp.dot(vt, p_scr[0:n, :], preferred_element_type=f32)

    def keys_with_positions(c):
        rows = pl.ds(pl.multiple_of(c * tk, tk), tk)
        return jnp.concatenate([k_ref[0, rows, :], ke_ref[rows, :]], axis=1)

    n_chunks = n_body // tk

    def issue_chunk_scores(c, slot):
        k_pos0 = N_META + c * tk
        version = jnp.where(k_pos0 + tk <= q_pos0, 0, jnp.where(k_pos0 >= q_pos0 + q_valid, 1, 2))
        issue_scores(keys_with_positions(c), version, slot)

    def add_straddle_bias(c, slot):
        k_pos0 = N_META + c * tk

        @pl.when(jnp.logical_and(k_pos0 + tk > q_pos0, k_pos0 < q_pos0 + q_valid))
        def _():
            variant = (q_pos0 - k_pos0) // tq
            for stream in streams:
                for slab in range(n_slab):
                    cols = slice(slab * LANES, (slab + 1) * LANES)
                    for r in range(0, tk, piece):
                        stream[slot][r:r + piece, cols] = (stream[slot][r:r + piece, cols]
                                                           + bias_scr[variant, r:r + piece, cols])

    def step(t, slot, issue_next, apply_prev):
        add_straddle_bias(t, slot)
        if issue_next:
            issue_chunk_scores(t + 1, 1 - slot)
        softmax(slot, tk)
        if apply_prev:
            apply_values(vt_ref[0, t - 1], 1 - slot)

    def pair(u, carry):
        step(2 * u, 0, True, True)
        step(2 * u + 1, 1, True, True)
        return carry

    issue_chunk_scores(0, 0)
    step(0, 0, True, False)
    step(1, 1, True, True)
    lax.fori_loop(1, n_chunks // 2 - 1, pair, 0)
    step(n_chunks - 2, 0, True, True)
    step(n_chunks - 1, 1, False, True)
    apply_values(vt_ref[0, n_chunks - 1], 1)

    tail = jnp.where(is_meta, 1, 0)
    const = -slope * jnp.full((1, 1), q_pos0, jnp.int32).astype(f32)
    issue_scores(k_ref[0, n_body:n_body + N_META, :], 2, 0)
    softmax(0, N_META, lambda r, slab: tail_scr[tail, r:r + piece, slab * LANES:(slab + 1) * LANES], const)
    apply_values(vt_ref[0, n_chunks, :, 0:N_META], 0)

    lv = lam_ref[...]
    lam = (jnp.exp(jnp.sum(lv[0:1] * lv[1:2], axis=-1, keepdims=True))
           - jnp.exp(jnp.sum(lv[2:3] * lv[3:4], axis=-1, keepdims=True)) + lam_init)
    o1, o2 = [stream[8][...] / stream[7][...] for stream in streams]
    att = (o1 - lam * o2).T
    o_ref[0] = (_rms(att, gain_ref[...]) * (1.0 - lam_init)).astype(bf16)


def _alibi_tables(n_body, n_q_tiles):
    l = n_body + N_META
    pos = jnp.concatenate([jnp.arange(n_body) + N_META, jnp.arange(N_META)])
    ke = jnp.zeros((l, LANES), f32)
    ke = ke.at[:, 0].set(pos // 64).at[:, 1].set(pos % 64).at[:, 2:4].set(1.0)
    q_pos = jnp.pad(pos, (0, n_q_tiles * ATTN_TQ - l)).astype(f32)
    rows = []
    for h in range(HEADS):
        s = 2.0 ** (-8.0 * (h + 1) / HEADS)
        ext = jnp.stack([jnp.full_like(q_pos, 64 * s), jnp.full_like(q_pos, s),
                         -64 * s * jnp.floor(q_pos / 64), -s * jnp.mod(q_pos, 64)])
        ext = jnp.pad(ext, ((0, BF16_ROWS - 4), (0, 0)))
        rows.append(jnp.stack([ext, -ext]))
    return ke.astype(bf16), jnp.stack(rows).astype(bf16)


def _diff_attention(qvt, k, lam_vecs, gain, slopes, n_body, lam_init):
    b, l, _ = k.shape
    n_t = qvt.shape[1]
    n_q = pl.cdiv(l, ATTN_TQ)
    per_chunk = ATTN_TK // ATTN_TQ
    assert ATTN_TK % ATTN_TQ == 0 and n_body % (2 * ATTN_TK) == 0 and n_body >= 4 * ATTN_TK
    ke, qe = _alibi_tables(n_body, n_q)
    kernel = functools.partial(_attn_kernel, n_body=n_body, lam_init=lam_init)
    return pl.pallas_call(
        kernel,
        grid=(b, HEADS, n_q),
        in_specs=[
            pl.BlockSpec(lam_vecs.shape, lambda bi, h, i: (0, 0)),
            pl.BlockSpec((1, D_V), lambda bi, h, i: (0, 0)),
            pl.BlockSpec((1, 1, LANES), lambda bi, h, i: (h, 0, 0)),
            pl.BlockSpec((1, 1, 2 * D_QK, ATTN_TQ), lambda bi, h, i: (bi, i // per_chunk, h, i % per_chunk)),
            pl.BlockSpec((1, 2, BF16_ROWS, ATTN_TQ), lambda bi, h, i: (h, 0, 0, i)),
            pl.BlockSpec((1, l, 2 * D_QK), lambda bi, h, i: (bi, 0, h)),
            pl.BlockSpec((l, LANES), lambda bi, h, i: (0, 0)),
            pl.BlockSpec((1, n_t, D_V, ATTN_TK), lambda bi, h, i: (bi, 0, HEADS + h, 0)),
        ],
        out_specs=pl.BlockSpec((1, ATTN_TQ, D_V), lambda bi, h, i: (bi, i, h)),
        out_shape=jax.ShapeDtypeStruct((b, l, HEADS * D_V), bf16),
        scratch_shapes=[
            pltpu.VMEM((3, 2 * LANES, 2 * ATTN_TQ), bf16),
            pltpu.VMEM((per_chunk, ATTN_TK, ATTN_TQ), f32),
            pltpu.VMEM((2, N_META, ATTN_TQ), f32),
        ] + 2 * [
            pltpu.VMEM((ATTN_TK, ATTN_TQ), f32), pltpu.VMEM((ATTN_TK, ATTN_TQ), f32),
            pltpu.VMEM((ATTN_TK, ATTN_TQ), bf16), pltpu.VMEM((ATTN_TK, ATTN_TQ), bf16),
            pltpu.VMEM((1, ATTN_TQ), f32), pltpu.VMEM((1, ATTN_TQ), f32),
            pltpu.VMEM((1, ATTN_TQ), f32), pltpu.VMEM((1, ATTN_TQ), f32),
            pltpu.VMEM((D_V, ATTN_TQ), f32),
        ],
        compiler_params=pltpu.CompilerParams(
            dimension_semantics=("arbitrary", "arbitrary", "arbitrary"), vmem_limit_bytes=40 * MIB),
        name="diff_attention",
    )(lam_vecs, gain, slopes, qvt, qe, k, ke, qvt)


def _pool_kernel(u_ref, pw_ref, ps_ref, o_ref, upad, *, n_body):
    l = n_body + N_META
    width = u_ref.shape[-1]
    upad[0:POOL_HALO, :] = jnp.zeros((POOL_HALO, width), f32)
    upad[POOL_HALO:POOL_HALO + N_META, :] = u_ref[0, n_body:l, :]
    upad[POOL_HALO + N_META:POOL_HALO + l, :] = u_ref[0, 0:n_body, :]
    upad[POOL_HALO + l:2 * POOL_HALO + l, :] = jnp.zeros((POOL_HALO, width), f32)

    def rows(pos0, n, out_row0):
        slab = upad[pl.ds(pos0, n + 2 * POOL_HALO), :]
        t = pos0 + lax.broadcasted_iota(jnp.int32, (n, 1), 0)
        for gi, w in enumerate(POOL_WINDOWS):
            cols = slice(gi * LANES, (gi + 1) * LANES)
            total = slab[POOL_HALO - w // 2:POOL_HALO - w // 2 + n, cols]
            for j in range(1, w):
                total = total + slab[POOL_HALO - w // 2 + j:POOL_HALO - w // 2 + j + n, cols]
            cnt = (jnp.minimum(t + w // 2, l) - jnp.maximum(t - w // 2, 0)).astype(f32)
            pooled = total / cnt - slab[POOL_HALO:POOL_HALO + n, cols]
            y = jnp.dot(pooled.astype(bf16), pw_ref[gi], preferred_element_type=f32) * ps_ref[:, cols]
            o_ref[0, pl.ds(out_row0, n), cols] = y.astype(bf16)

    rows(0, N_META, n_body)

    def body(c, carry):
        r0 = pl.multiple_of(c * POOL_CHUNK, POOL_CHUNK)
        rows(N_META + r0, POOL_CHUNK, r0)
        return carry

    lax.fori_loop(0, n_body // POOL_CHUNK, body, 0)


def _multiscale_pool(u, pool_w, pool_scale, n_body):
    b, l, width = u.shape
    kernel = functools.partial(_pool_kernel, n_body=n_body)
    return pl.pallas_call(
        kernel,
        grid=(b,),
        in_specs=[
            pl.BlockSpec((1, l, width), lambda bi: (bi, 0, 0)),
            pl.BlockSpec(pool_w.shape, lambda bi: (0, 0, 0)),
            pl.BlockSpec((1, width), lambda bi: (0, 0)),
        ],
        out_specs=pl.BlockSpec((1, l, width), lambda bi: (bi, 0, 0)),
        out_shape=jax.ShapeDtypeStruct((b, l, width), bf16),
        scratch_shapes=[pltpu.VMEM((l + 2 * POOL_HALO, width), f32)],
        compiler_params=pltpu.CompilerParams(
            dimension_semantics=("arbitrary",), vmem_limit_bytes=48 * MIB),
        name="multiscale_pool",
    )(u, pool_w, pool_scale)


def _outproj_router_kernel(h_ref, a_ref, p_ref, woa_ref, wop_ref, g_ref, rw_ref, h2_ref, hn_ref, aff_ref):
    tm = h_ref.shape[0]
    h2 = (h_ref[...]
          + jnp.dot(a_ref[...], woa_ref[...], preferred_element_type=f32)
          + jnp.dot(p_ref[...], wop_ref[...], preferred_element_type=f32))
    h2_ref[...] = h2
    hn = _rms(h2, g_ref[...])
    logits = jnp.dot(hn.astype(bf16), rw_ref[...], preferred_element_type=f32)
    lane = lax.broadcasted_iota(jnp.int32, logits.shape, 1)
    logits = jnp.where(lane < N_EXPERTS, logits, -jnp.inf)
    z = jnp.exp(logits - jnp.max(logits, axis=-1, keepdims=True))
    aff_ref[...] = z / jnp.sum(z, axis=-1, keepdims=True)
    for s in range(hn.shape[1] // LANES):
        hn_ref[pl.ds(s, tm, stride=SUBLANES), :] = hn[:, s * LANES:(s + 1) * LANES]


def _outproj_router(h, a, p, wo_a, wo_p, g, rw):
    m, d = h.shape
    half = a.shape[1]
    n_e = rw.shape[1]
    row = lambda i: (i, 0)
    fixed = lambda i: (0, 0)
    return pl.pallas_call(
        _outproj_router_kernel,
        grid=(pl.cdiv(m, ROW_TILE),),
        in_specs=[
            pl.BlockSpec((ROW_TILE, d), row),
            pl.BlockSpec((ROW_TILE, half), row),
            pl.BlockSpec((ROW_TILE, half), row),
            pl.BlockSpec((half, d), fixed),
            pl.BlockSpec((half, d), fixed),
            pl.BlockSpec((1, d), fixed),
            pl.BlockSpec((d, n_e), fixed),
        ],
        out_specs=[
            pl.BlockSpec((ROW_TILE, d), row),
            pl.BlockSpec((ROW_TILE * SUBLANES, LANES), row),
            pl.BlockSpec((ROW_TILE, n_e), row),
        ],
        out_shape=[
            jax.ShapeDtypeStruct((m, d), f32),
            jax.ShapeDtypeStruct((m * SUBLANES, LANES), f32),
            jax.ShapeDtypeStruct((m, n_e), f32),
        ],
        compiler_params=pltpu.CompilerParams(
            dimension_semantics=("arbitrary",), vmem_limit_bytes=40 * MIB),
        name="outproj_router",
    )(h, a, p, wo_a, wo_p, g, rw)


def _prefix_count(flags, w_scan):
    out = []
    offset = jnp.zeros((flags.shape[0], LANES), f32)
    for blk in range(flags.shape[1] // LANES):
        r = jnp.dot(flags[:, blk * LANES:(blk + 1) * LANES].astype(bf16), w_scan, preferred_element_type=f32)
        out.append(r[:, :LANES] + offset)
        offset = offset + r[:, LANES:]
    return jnp.concatenate(out, axis=1)


def _topk_kernel(aff_ref, idx_ref, rank_scr, chosen_scr, *, cap, n_body):
    n_e, lp = aff_ref.shape[1], aff_ref.shape[2]
    cp = idx_ref.shape[1]
    n_blk = lp // LANES
    bits = lax.bitcast_convert_type(aff_ref[0], jnp.int32)

    def refine(step, thr):
        cand = thr | (1 << (30 - step))
        cnt = jnp.sum(jnp.where(bits >= cand, 1, 0), axis=1, keepdims=True)
        return jnp.where(cnt >= cap, cand, thr)

    thr = lax.fori_loop(0, 31, refine, jnp.zeros((n_e, 1), jnp.int32))
    aff = aff_ref[0]
    pivot = jnp.max(jnp.where(bits == thr, aff, -1.0), axis=1, keepdims=True)
    above = aff > pivot
    tie = aff == pivot
    need = (cap - jnp.sum(jnp.where(above, 1, 0), axis=1, keepdims=True)).astype(f32)

    row = lax.broadcasted_iota(jnp.int32, (LANES, 2 * LANES), 0)
    col = lax.broadcasted_iota(jnp.int32, (LANES, 2 * LANES), 1)
    w_scan = jnp.where((row <= col) | (col >= LANES), 1.0, 0.0).astype(bf16)

    tie_rank = _prefix_count(jnp.where(tie, 1.0, 0.0), w_scan)
    chosen_scr[...] = jnp.where(above | (tie & (tie_rank <= need)), 1.0, 0.0)
    pos_f = lax.broadcasted_iota(jnp.int32, (n_e, lp), 1).astype(f32)
    big = float(2 * lp)

    def excess(cnt):
        return jnp.sum(jnp.abs(cnt - float(cap))) > 0.0

    def repair(cnt):
        a = aff_ref[0]
        sel = chosen_scr[...] > 0.5
        lo = jnp.min(jnp.where(sel, a, jnp.inf), axis=1, keepdims=True)
        lo_at = jnp.max(jnp.where(sel & (a == lo), pos_f, -1.0), axis=1, keepdims=True)
        drop = (pos_f == lo_at) & (cnt > float(cap))
        free = (~sel) & (a >= 0.0)
        hi = jnp.max(jnp.where(free, a, -1.0), axis=1, keepdims=True)
        hi_at = jnp.min(jnp.where(free & (a == hi), pos_f, big), axis=1, keepdims=True)
        add = (pos_f == hi_at) & (cnt < float(cap))
        new = jnp.where(add, 1.0, jnp.where(drop, 0.0, chosen_scr[...]))
        chosen_scr[...] = new
        return jnp.sum(new, axis=1, keepdims=True)

    lax.while_loop(excess, repair, jnp.sum(chosen_scr[...], axis=1, keepdims=True))
    rank = _prefix_count(chosen_scr[...], w_scan)
    for e in range(n_e):
        rank_scr[e] = rank[e:e + 1, :]

    lane = lax.broadcasted_iota(jnp.int32, (SUBLANES, LANES), 1)
    sub = lax.broadcasted_iota(jnp.int32, (SUBLANES, LANES), 0)

    n_groups = cp // SUBLANES
    unroll = next(u for u in (11, 8, 6, 5, 4, 3, 2, 1) if n_groups % u == 0)

    def per_expert(e, carry):
        def per_groups(gi, carry2):
            slots = [((gi * unroll + u) * SUBLANES + sub).astype(f32) for u in range(unroll)]
            accs = [jnp.zeros((SUBLANES, LANES), f32) for _ in range(unroll)]
            for t in range(n_blk):
                r_blk = jnp.broadcast_to(rank_scr[e, :, t * LANES:(t + 1) * LANES], (SUBLANES, LANES))
                accs = [a + jnp.where(r_blk <= c, 1.0, 0.0) for a, c in zip(accs, slots)]
            for u in range(unroll):
                pos = jnp.sum(accs[u], axis=1, keepdims=True).astype(jnp.int32)
                rows = jnp.where(pos < N_META, pos + n_body, pos - N_META)
                at = pl.ds(pl.multiple_of((gi * unroll + u) * SUBLANES, SUBLANES), SUBLANES)
                idx_ref[0, at, :] = jnp.where(lane == e, rows, idx_ref[0, at, :])
            return carry2

        return lax.fori_loop(0, n_groups // unroll, per_groups, carry)

    idx_ref[0] = jnp.zeros((cp, LANES), jnp.int32)
    lax.fori_loop(0, n_e, per_expert, 0)


def _topk_rows(aff_pos, cap, cp, n_body):
    b, n_e, lp = aff_pos.shape
    kernel = functools.partial(_topk_kernel, cap=cap, n_body=n_body)
    return pl.pallas_call(
        kernel,
        grid=(b,),
        in_specs=[pl.BlockSpec((1, n_e, lp), lambda bi: (bi, 0, 0))],
        out_specs=pl.BlockSpec((1, cp, LANES), lambda bi: (bi, 0, 0)),
        out_shape=jax.ShapeDtypeStruct((b, cp, LANES), jnp.int32),
        scratch_shapes=[pltpu.VMEM((n_e, 1, lp), f32), pltpu.VMEM((n_e, lp), f32)],
        compiler_params=pltpu.CompilerParams(
            dimension_semantics=("arbitrary",), vmem_limit_bytes=32 * MIB),
        name="topk_rows",
    )(aff_pos)


def _gather_kernel(idx_ref, src_ref, aff_ref, o_ref, gate_ref, tile):
    cp = o_ref.shape[1]
    sub = lax.broadcasted_iota(jnp.int32, (SUBLANES, LANES), 0)

    def group(g, carry):
        gates = jnp.zeros((SUBLANES, LANES), f32)
        for j in range(SUBLANES):
            slot = g * SUBLANES + j
            r = idx_ref[0, 0, slot]
            tile[pl.ds(pl.multiple_of(slot * SUBLANES, SUBLANES), SUBLANES), :] = (
                src_ref[0, pl.ds(pl.multiple_of(r * SUBLANES, SUBLANES), SUBLANES), :])
            aff8 = aff_ref[0, pl.ds(pl.multiple_of((r >> 3) << 3, SUBLANES), SUBLANES), :]
            gates = jnp.where(sub == j, pltpu.roll(aff8, (j + SUBLANES - (r & 7)) & 7, axis=0), gates)
        gate_ref[0, pl.ds(pl.multiple_of(g * SUBLANES, SUBLANES), SUBLANES), :] = gates
        return carry

    lax.fori_loop(0, cp // SUBLANES, group, 0)
    for s in range(o_ref.shape[2] // LANES):
        o_ref[0, :, s * LANES:(s + 1) * LANES] = tile[pl.ds(s, cp, stride=SUBLANES), :].astype(bf16)


def _gather_tokens(idx, hn_tiles, aff, d):
    b, n_e, cp = idx.shape
    rows = hn_tiles.shape[1]
    l = aff.shape[1]
    return pl.pallas_call(
        _gather_kernel,
        grid=(b, n_e),
        in_specs=[
            pl.BlockSpec((1, 1, cp), lambda bi, e: (bi * n_e + e, 0, 0), memory_space=pltpu.SMEM),
            pl.BlockSpec((1, rows, LANES), lambda bi, e: (bi, 0, 0)),
            pl.BlockSpec((1, l, LANES), lambda bi, e: (bi, 0, 0)),
        ],
        out_specs=[
            pl.BlockSpec((1, cp, d), lambda bi, e: (e, bi, 0)),
            pl.BlockSpec((1, cp, LANES), lambda bi, e: (e, bi, 0)),
        ],
        out_shape=[
            jax.ShapeDtypeStruct((n_e, b * cp, d), bf16),
            jax.ShapeDtypeStruct((n_e, b * cp, LANES), f32),
        ],
        scratch_shapes=[pltpu.VMEM((cp * SUBLANES, LANES), f32)],
        compiler_params=pltpu.CompilerParams(
            dimension_semantics=("arbitrary", "arbitrary"), vmem_limit_bytes=48 * MIB),
        name="gather_tokens",
    )(idx.reshape(b * n_e, 1, cp), hn_tiles, aff)


def _moe_kernel(x_ref, gate_ref, wg_ref, wu_ref, wd_ref, y_ref, wg_b, wu_b, wd_b, acc, *, cp):
    f = pl.program_id(1)
    n_f = pl.num_programs(1)
    wg_b[...] = wg_ref[0].astype(bf16)
    wu_b[...] = wu_ref[0].astype(bf16)
    wd_b[...] = wd_ref[0].astype(bf16)

    @pl.when(f == 0)
    def _():
        acc[...] = jnp.zeros(acc.shape, f32)

    n_chunks = acc.shape[0] // cp
    for c in range(n_chunks):
        rows = slice(c * cp, (c + 1) * cp)
        x = x_ref[0, rows, :]
        g = jnp.dot(x, wg_b[...], preferred_element_type=f32)
        u = jnp.dot(x, wu_b[...], preferred_element_type=f32)
        hid = (g * jax.nn.sigmoid(g) * u).astype(bf16)
        acc[rows, :] += jnp.dot(hid, wd_b[...], preferred_element_type=f32)

    @pl.when(f == n_f - 1)
    def _():
        for c in range(n_chunks):
            rows = slice(c * cp, (c + 1) * cp)
            lane = lax.broadcasted_iota(jnp.int32, (cp, LANES), 1)
            gate = jnp.sum(jnp.where(lane == pl.program_id(0), gate_ref[0, rows, :], 0.0),
                           axis=1, keepdims=True)
            y = acc[rows, :] * gate
            for s in range(y.shape[1] // LANES):
                y_ref[0, pl.ds(c * cp * SUBLANES + s, cp, stride=SUBLANES), :] = (
                    y[:, s * LANES:(s + 1) * LANES])


def _expert_ffn(xg, gate, w_gate, w_up, w_down, cp):
    n_e, m, d = xg.shape
    d_ff = w_gate.shape[2]
    kernel = functools.partial(_moe_kernel, cp=cp)
    return pl.pallas_call(
        kernel,
        grid=(n_e, d_ff // FF_TILE),
        in_specs=[
            pl.BlockSpec((1, m, d), lambda e, f: (e, 0, 0)),
            pl.BlockSpec((1, m, LANES), lambda e, f: (e, 0, 0)),
            pl.BlockSpec((1, d, FF_TILE), lambda e, f: (e, 0, f)),
            pl.BlockSpec((1, d, FF_TILE), lambda e, f: (e, 0, f)),
            pl.BlockSpec((1, FF_TILE, d), lambda e, f: (e, f, 0)),
        ],
        out_specs=pl.BlockSpec((1, m * SUBLANES, LANES), lambda e, f: (e, 0, 0)),
        out_shape=jax.ShapeDtypeStruct((n_e, m * SUBLANES, LANES), f32),
        scratch_shapes=[
            pltpu.VMEM((d, FF_TILE), bf16),
            pltpu.VMEM((d, FF_TILE), bf16),
            pltpu.VMEM((FF_TILE, d), bf16),
            pltpu.VMEM((m, d), f32),
        ],
        compiler_params=pltpu.CompilerParams(
            dimension_semantics=("arbitrary", "arbitrary"), vmem_limit_bytes=56 * MIB),
        name="expert_ffn",
    )(xg, gate, w_gate, w_up, w_down)


def _combine_kernel(idx_ref, y_ref, h2_ref, g_ref, o_ref, acc, *, n_experts):
    step = pl.program_id(1)
    cp = idx_ref.shape[2]
    tc = o_ref.shape[1]

    @pl.when(step == 0)
    def _():
        acc[...] = jnp.zeros(acc.shape, f32)

    @pl.when(step < n_experts)
    def _():
        def group(g, carry):
            dst = []
            val = []
            for j in range(SUBLANES):
                slot = g * SUBLANES + j
                r = pl.multiple_of(idx_ref[0, 0, slot] * SUBLANES, SUBLANES)
                dst.append(r)
                val.append(acc[pl.ds(r, SUBLANES), :]
                           + y_ref[0, pl.ds(pl.multiple_of(slot * SUBLANES, SUBLANES), SUBLANES), :])
            for r, v in zip(dst, val):
                acc[pl.ds(r, SUBLANES), :] = v
            return carry

        lax.fori_loop(0, cp // SUBLANES, group, 0)

    @pl.when(step >= n_experts)
    def _():
        r0 = pl.multiple_of((step - n_experts) * tc * SUBLANES, SUBLANES)
        moe = jnp.concatenate(
            [acc[pl.ds(r0 + s, tc, stride=SUBLANES), :] for s in range(o_ref.shape[2] // LANES)], axis=-1)
        o_ref[0] = _rms(h2_ref[0] + moe, g_ref[...])


def _combine(idx, y, h2, g, n_body):
    b, n_e, cp = idx.shape
    _, l, d = h2.shape
    n_out = n_body // OUT_CHUNK
    kernel = functools.partial(_combine_kernel, n_experts=n_e)
    expert = lambda s: jnp.minimum(s, n_e - 1)
    chunk = lambda s: jnp.maximum(s - n_e, 0)
    return pl.pallas_call(
        kernel,
        grid=(b, n_e + n_out),
        in_specs=[
            pl.BlockSpec((1, 1, cp), lambda bi, s: (bi * n_e + expert(s), 0, 0), memory_space=pltpu.SMEM),
            pl.BlockSpec((1, cp * SUBLANES, LANES), lambda bi, s: (expert(s), bi, 0)),
            pl.BlockSpec((1, OUT_CHUNK, d), lambda bi, s: (bi, chunk(s), 0)),
            pl.BlockSpec((1, d), lambda bi, s: (0, 0)),
        ],
        out_specs=pl.BlockSpec((1, OUT_CHUNK, d), lambda bi, s: (bi, chunk(s), 0)),
        out_shape=jax.ShapeDtypeStruct((b, n_body, d), f32),
        scratch_shapes=[pltpu.VMEM(((l + 1) * SUBLANES, LANES), f32)],
        compiler_params=pltpu.CompilerParams(
            dimension_semantics=("arbitrary", "arbitrary"), vmem_limit_bytes=48 * MIB),
        name="combine",
    )(idx.reshape(b * n_e, 1, cp), y, h2, g)


def _round_up(n, k):
    return -(-n // k) * k


def kernel(x, meta_tokens, g_mix, w_in, lam_vecs, subln_gain, pool_w, pool_scale, w_o, g_ffn, router_w,
           w_gate, w_up, w_down, g_final):
    b, n_body, d = x.shape
    l = n_body + N_META
    qk = HEADS * D_QK
    attn_width = HEADS * D_V
    depth = g_mix.shape[0]
    cap = CAPACITY_FACTOR * l // N_EXPERTS
    cp = _round_up(cap, BF16_ROWS)

    meta = jnp.broadcast_to(meta_tokens[None].astype(x.dtype), (b, N_META, d))
    h = jnp.concatenate([x, meta], axis=1)
    slopes = jnp.broadcast_to(
        jnp.array([2.0 ** (-8.0 * (i + 1) / HEADS) for i in range(HEADS)], f32)[:, None, None],
        (HEADS, 1, LANES))
    cols = jnp.arange(HEADS)[:, None] * D_QK + jnp.arange(D_QK)[None, :]
    q_cols = jnp.concatenate([cols, qk + cols], axis=1).reshape(-1)
    k_cols = q_cols + 2 * qk
    v_cols = 4 * qk + jnp.arange(attn_width)
    u_cols = jnp.arange(4 * qk + attn_width, w_in.shape[2])

    out = None
    for layer in range(depth):
        lam_init = 0.8 - 0.6 * math.exp(-0.3 * layer)
        w = w_in[layer]
        w_qv_t = jnp.concatenate([w[:, q_cols] * (1.0 / math.sqrt(D_QK)), w[:, v_cols]], axis=1).T.astype(bf16)
        w_ku = jnp.concatenate([w[:, k_cols], w[:, u_cols]], axis=1).astype(bf16)
        qvt, k, u = _norm_inproj(h, g_mix[layer][None, :], w_qv_t, w_ku, 2 * qk)
        a = _diff_attention(qvt, k, lam_vecs[layer].astype(f32), subln_gain[layer][None, :],
                            slopes, n_body, lam_init)
        p = _multiscale_pool(u, pool_w[layer].astype(bf16), pool_scale[layer][None, :], n_body)
        wo = w_o[layer].astype(bf16)
        h2, hn_tiles, aff = _outproj_router(
            h.reshape(b * l, d), a.reshape(b * l, -1), p.reshape(b * l, -1), wo[:attn_width], wo[attn_width:],
            g_ffn[layer][None, :],
            jnp.pad(router_w[layer], ((0, 0), (0, LANES - N_EXPERTS))).astype(bf16))

        aff = aff.reshape(b, l, LANES)
        lp = _round_up(l, LANES)
        aff_pos = jnp.concatenate(
            [aff[:, n_body:, :N_EXPERTS], aff[:, :n_body, :N_EXPERTS], jnp.full((b, lp - l, N_EXPERTS), -1.0, f32)],
            axis=1)
        idx = _topk_rows(jnp.swapaxes(aff_pos, 1, 2), cap, cp, n_body)
        idx = jnp.swapaxes(idx[:, :, :N_EXPERTS], 1, 2)
        real = jnp.arange(cp)[None, None, :] < cap
        idx_gather = jnp.where(real, idx, 0)
        idx_scatter = jnp.where(real, idx, l)

        xg, gate = _gather_tokens(idx_gather, hn_tiles.reshape(b, l * SUBLANES, LANES), aff, d)
        y = _expert_ffn(xg, gate, w_gate[layer], w_up[layer], w_down[layer], cp)
        last = layer == depth - 1
        g_out = g_final[None, :] if last else jnp.ones((1, d), f32)
        out = _combine(idx_scatter, y, h2.reshape(b, l, d), g_out, n_body)
    return out
```

```python
import functools
import math

import jax
import jax.numpy as jnp
from jax import lax
from jax.experimental import pallas as pl
from jax.experimental.pallas import tpu as pltpu

N_META = 16
HEADS = 4
D_QK = 64
D_V = 128
POOL_WINDOWS = (2, 4, 8, 16)
POOL_HALO = max(POOL_WINDOWS) // 2
N_EXPERTS = 16
CAPACITY_FACTOR = 2
EPS = 1e-6

LANES = 128
SUBLANES = 8
BF16_ROWS = 16

ROW_TILE = 512
ATTN_TQ = 256
ATTN_TK = 512
POOL_CHUNK = 256
FF_TILE = 256
OUT_CHUNK = 512
MIB = 1024 * 1024

f32 = jnp.float32
bf16 = jnp.bfloat16


def _rms(x, g):
    return x * lax.rsqrt(jnp.mean(x * x, axis=-1, keepdims=True) + EPS) * g


def _norm_inproj_kernel(x_ref, g_ref, wt_ref, w_ref, qvt_ref, k_ref, u_ref, *, seq_len):
    x = x_ref[0]
    valid = seq_len - pl.program_id(1) * x.shape[0]
    x = jnp.where(lax.broadcasted_iota(jnp.int32, x.shape, 0) < valid, x, 0.0)
    n = _rms(x, g_ref[...]).astype(bf16)
    qvt = lax.dot_general(wt_ref[...], n, (((1,), (1,)), ((), ())), preferred_element_type=f32)
    qvt_ref[0, 0] = qvt.astype(bf16)
    ku = jnp.dot(n, w_ref[...], preferred_element_type=f32)
    n_k = k_ref.shape[-1]
    k_ref[0] = ku[:, :n_k].astype(bf16)
    u_ref[0] = ku[:, n_k:]


def _norm_inproj(h, g, w_qv_t, w_ku, n_k):
    b, l, d = h.shape
    n_t = pl.cdiv(l, ATTN_TK)
    n_qv = w_qv_t.shape[0]
    n_u = w_ku.shape[1] - n_k
    return pl.pallas_call(
        functools.partial(_norm_inproj_kernel, seq_len=l),
        grid=(b, n_t),
        in_specs=[
            pl.BlockSpec((1, ATTN_TK, d), lambda bi, i: (bi, i, 0)),
            pl.BlockSpec((1, d), lambda bi, i: (0, 0)),
            pl.BlockSpec(w_qv_t.shape, lambda bi, i: (0, 0)),
            pl.BlockSpec(w_ku.shape, lambda bi, i: (0, 0)),
        ],
        out_specs=[
            pl.BlockSpec((1, 1, n_qv, ATTN_TK), lambda bi, i: (bi, i, 0, 0)),
            pl.BlockSpec((1, ATTN_TK, n_k), lambda bi, i: (bi, i, 0)),
            pl.BlockSpec((1, ATTN_TK, n_u), lambda bi, i: (bi, i, 0)),
        ],
        out_shape=[
            jax.ShapeDtypeStruct((b, n_t, n_qv, ATTN_TK), bf16),
            jax.ShapeDtypeStruct((b, l, n_k), bf16),
            jax.ShapeDtypeStruct((b, l, n_u), f32),
        ],
        compiler_params=pltpu.CompilerParams(
            dimension_semantics=("arbitrary", "arbitrary"), vmem_limit_bytes=40 * MIB),
        name="norm_inproj",
    )(h, g, w_qv_t, w_ku)


def _attn_kernel(lam_ref, gain_ref, slope_ref, q_ref, qe_ref, k_ref, ke_ref, vt_ref, o_ref,
                 qs_scr, bias_scr, tail_scr, *scr, n_body, lam_init):
    tq = ATTN_TQ
    tk = ATTN_TK
    n_diag = tk // tq
    n_sub = q_ref.shape[3] // tq
    i = pl.program_id(2)
    n_q = pl.num_programs(2)
    is_meta = i == n_q - 1
    slope = slope_ref[0][:, :1]

    @pl.when(i == 0)
    def _():
        d = (lax.broadcasted_iota(jnp.int32, (tk, tq), 0)
             - lax.broadcasted_iota(jnp.int32, (tk, tq), 1)).astype(f32)
        for v in range(n_diag):
            bias_scr[v] = -slope * jnp.abs(d - float(v * tq))
        tail_scr[0] = slope * d[0:N_META]
        tail_scr[1] = -slope * jnp.abs(d[0:N_META])
        width = qs_scr.shape[2]
        qs_scr[:, 2 * D_QK + BF16_ROWS:, :] = jnp.zeros((3, 2 * LANES - 2 * D_QK - BF16_ROWS, width), bf16)
        qs_scr[2, 2 * D_QK:2 * D_QK + BF16_ROWS, :] = jnp.zeros((BF16_ROWS, width), bf16)

    zero = jnp.zeros((D_QK, tq), bf16)
    for sub in range(n_sub):
        q = q_ref[0, 0, :, sub * tq:(sub + 1) * tq]
        cols = slice(2 * sub * tq, 2 * (sub + 1) * tq)
        top = jnp.concatenate([q[0:D_QK], zero], axis=1)
        bottom = jnp.concatenate([zero, q[D_QK:]], axis=1)
        for v in range(3):
            qs_scr[v, 0:D_QK, cols] = top
            qs_scr[v, D_QK:2 * D_QK, cols] = bottom
        for v in range(2):
            ext = qe_ref[0, v, :, sub * tq:(sub + 1) * tq]
            qs_scr[v, 2 * D_QK:2 * D_QK + BF16_ROWS, cols] = jnp.concatenate([ext, ext], axis=1)

    n_streams = 2 * n_sub
    per_stream = len(scr) // n_streams
    streams = [scr[st * per_stream:(st + 1) * per_stream] for st in range(n_streams)]
    for _, _, _, _, _, _, m_scr, l_scr, acc_scr in streams:
        m_scr[...] = jnp.full(m_scr.shape, -jnp.inf, f32)
        l_scr[...] = jnp.zeros(l_scr.shape, f32)
        acc_scr[...] = jnp.zeros(acc_scr.shape, f32)

    q_pos0 = jnp.where(is_meta, 0, N_META + i * n_sub * tq)
    q_valid = jnp.where(is_meta, N_META, n_sub * tq)
    n_slab = tq // LANES
    piece = BF16_ROWS

    def issue_scores(k, version, slot):
        n = k.shape[0]
        for st, stream in enumerate(streams):
            stream[slot][0:n, :] = jnp.dot(k, qs_scr[version, 0:k.shape[1], st * tq:(st + 1) * tq],
                                           preferred_element_type=f32)

    def softmax(slot, n, bias_tile=None, consts=None):
        def scores(s_scr, r, slab):
            t = s_scr[r:r + piece, slab * LANES:(slab + 1) * LANES]
            return t if bias_tile is None else t + bias_tile(r, slab)

        stats = []
        for st, stream in enumerate(streams):
            s_scr, m_scr = stream[slot], stream[6]
            const = None if consts is None else consts[st // 2]
            m_old = m_scr[...]
            maxes = []
            for slab in range(n_slab):
                mx = scores(s_scr, 0, slab)
                for r in range(piece, n, piece):
                    mx = jnp.maximum(mx, scores(s_scr, r, slab))
                maxes.append(jnp.max(mx, axis=0, keepdims=True))
            m_loc = jnp.concatenate(maxes, axis=1)
            m_new = jnp.maximum(m_old, m_loc if const is None else m_loc + const)
            stats.append((m_new, m_new if const is None else m_new - const, jnp.exp(m_old - m_new)))

        for stream, (m_new, shift, alpha) in zip(streams, stats):
            s_scr, p_scr, a_scr, m_scr, l_scr = stream[slot], stream[2 + slot], stream[4 + slot], stream[6], stream[7]
            sums = []
            for slab in range(n_slab):
                cols = slice(slab * LANES, (slab + 1) * LANES)
                sub_b = jnp.broadcast_to(shift[:, cols], (piece, LANES))
                tot = None
                for r in range(0, n, piece):
                    p = jnp.exp(scores(s_scr, r, slab) - sub_b)
                    tot = p if tot is None else tot + p
                    p_scr[r:r + piece, cols] = p.astype(bf16)
                sums.append(jnp.sum(tot, axis=0, keepdims=True))
            l_scr[...] = alpha * l_scr[...] + jnp.concatenate(sums, axis=1)
            a_scr[...] = alpha
            m_scr[...] = m_new

    def apply_values(vt, slot):
        n = vt.shape[1]
        for stream in streams:
            p_scr, a_scr, acc_scr = stream[2 + slot], stream[4 + slot], stream[8]
            acc_scr[...] = a_scr[...] * acc_scr[...] + jnp.dot(vt, p_scr[0:n, :], preferred_element_type=f32)

    def keys_with_positions(c):
        rows = pl.ds(pl.multiple_of(c * tk, tk), tk)
        return jnp.concatenate([k_ref[0, rows, :], ke_ref[rows, :]], axis=1)

    n_chunks = n_body // tk

    def issue_chunk_scores(c, slot):
        k_pos0 = N_META + c * tk
        version = jnp.where(k_pos0 + tk <= q_pos0, 0, jnp.where(k_pos0 >= q_pos0 + q_valid, 1, 2))
        issue_scores(keys_with_positions(c), version, slot)

    def add_straddle_bias(c, slot):
        k_pos0 = N_META + c * tk

        @pl.when(jnp.logical_and(k_pos0 + tk > q_pos0, k_pos0 < q_pos0 + q_valid))
        def _():
            for st, stream in enumerate(streams):
                variant = (q_pos0 - k_pos0) // tq + st // 2
                for slab in range(n_slab):
                    cols = slice(slab * LANES, (slab + 1) * LANES)
                    for r in range(0, tk, piece):
                        stream[slot][r:r + piece, cols] = (stream[slot][r:r + piece, cols]
                                                           + bias_scr[variant, r:r + piece, cols])

    def step(t, slot, issue_next, apply_prev):
        add_straddle_bias(t, slot)
        if issue_next:
            issue_chunk_scores(t + 1, 1 - slot)
        softmax(slot, tk)
        if apply_prev:
            apply_values(vt_ref[0, t - 1], 1 - slot)

    def pair(u, carry):
        step(2 * u, 0, True, True)
        step(2 * u + 1, 1, True, True)
        return carry

    issue_chunk_scores(0, 0)
    step(0, 0, True, False)
    step(1, 1, True, True)
    lax.fori_loop(1, n_chunks // 2 - 1, pair, 0)
    step(n_chunks - 2, 0, True, True)
    step(n_chunks - 1, 1, False, True)
    apply_values(vt_ref[0, n_chunks - 1], 1)

    tail = jnp.where(is_meta, 1, 0)
    consts = [-slope * jnp.full((1, 1), q_pos0 + sub * tq, jnp.int32).astype(f32) for sub in range(n_sub)]
    issue_scores(k_ref[0, n_body:n_body + N_META, :], 2, 0)
    softmax(0, N_META, lambda r, slab: tail_scr[tail, r:r + piece, slab * LANES:(slab + 1) * LANES], consts)
    apply_values(vt_ref[0, n_chunks, :, 0:N_META], 0)

    lv = lam_ref[...]
    lam = (jnp.exp(jnp.sum(lv[0:1] * lv[1:2], axis=-1, keepdims=True))
           - jnp.exp(jnp.sum(lv[2:3] * lv[3:4], axis=-1, keepdims=True)) + lam_init)
    for sub in range(n_sub):
        o1, o2 = [stream[8][...] / stream[7][...] for stream in streams[2 * sub:2 * sub + 2]]
        att = (o1 - lam * o2).T
        o_ref[0, sub * tq:(sub + 1) * tq, :] = (_rms(att, gain_ref[...]) * (1.0 - lam_init)).astype(bf16)


def _alibi_tables(n_body, n_q_tiles):
    l = n_body + N_META
    pos = jnp.concatenate([jnp.arange(n_body) + N_META, jnp.arange(N_META)])
    ke = jnp.zeros((l, LANES), f32)
    ke = ke.at[:, 0].set(pos // 64).at[:, 1].set(pos % 64).at[:, 2:4].set(1.0)
    q_pos = jnp.pad(pos, (0, n_q_tiles * ATTN_TQ - l)).astype(f32)
    rows = []
    for h in range(HEADS):
        s = 2.0 ** (-8.0 * (h + 1) / HEADS)
        ext = jnp.stack([jnp.full_like(q_pos, 64 * s), jnp.full_like(q_pos, s),
                         -64 * s * jnp.floor(q_pos / 64), -s * jnp.mod(q_pos, 64)])
        ext = jnp.pad(ext, ((0, BF16_ROWS - 4), (0, 0)))
        rows.append(jnp.stack([ext, -ext]))
    return ke.astype(bf16), jnp.stack(rows).astype(bf16)


def _diff_attention(qvt, k, lam_vecs, gain, slopes, n_body, lam_init):
    b, l, _ = k.shape
    n_t = qvt.shape[1]
    per_chunk = ATTN_TK // ATTN_TQ
    n_q = pl.cdiv(l, ATTN_TK)
    assert ATTN_TK % ATTN_TQ == 0 and n_body % (2 * ATTN_TK) == 0 and n_body >= 4 * ATTN_TK
    ke, qe = _alibi_tables(n_body, n_q * per_chunk)
    kernel = functools.partial(_attn_kernel, n_body=n_body, lam_init=lam_init)
    return pl.pallas_call(
        kernel,
        grid=(b, HEADS, n_q),
        in_specs=[
            pl.BlockSpec(lam_vecs.shape, lambda bi, h, i: (0, 0)),
            pl.BlockSpec((1, D_V), lambda bi, h, i: (0, 0)),
            pl.BlockSpec((1, 1, LANES), lambda bi, h, i: (h, 0, 0)),
            pl.BlockSpec((1, 1, 2 * D_QK, ATTN_TK), lambda bi, h, i: (bi, i, h, 0)),
            pl.BlockSpec((1, 2, BF16_ROWS, ATTN_TK), lambda bi, h, i: (h, 0, 0, i)),
            pl.BlockSpec((1, l, 2 * D_QK), lambda bi, h, i: (bi, 0, h)),
            pl.BlockSpec((l, LANES), lambda bi, h, i: (0, 0)),
            pl.BlockSpec((1, n_t, D_V, ATTN_TK), lambda bi, h, i: (bi, 0, HEADS + h, 0)),
        ],
        out_specs=pl.BlockSpec((1, ATTN_TK, D_V), lambda bi, h, i: (bi, i, h)),
        out_shape=jax.ShapeDtypeStruct((b, l, HEADS * D_V), bf16),
        scratch_shapes=[
            pltpu.VMEM((3, 2 * LANES, 2 * ATTN_TK), bf16),
            pltpu.VMEM((per_chunk, ATTN_TK, ATTN_TQ), f32),
            pltpu.VMEM((2, N_META, ATTN_TQ), f32),
        ] + 2 * per_chunk * [
            pltpu.VMEM((ATTN_TK, ATTN_TQ), f32), pltpu.VMEM((ATTN_TK, ATTN_TQ), f32),
            pltpu.VMEM((ATTN_TK, ATTN_TQ), bf16), pltpu.VMEM((ATTN_TK, ATTN_TQ), bf16),
            pltpu.VMEM((1, ATTN_TQ), f32), pltpu.VMEM((1, ATTN_TQ), f32),
            pltpu.VMEM((1, ATTN_TQ), f32), pltpu.VMEM((1, ATTN_TQ), f32),
            pltpu.VMEM((D_V, ATTN_TQ), f32),
        ],
        compiler_params=pltpu.CompilerParams(
            dimension_semantics=("arbitrary", "arbitrary", "arbitrary"), vmem_limit_bytes=40 * MIB),
        name="diff_attention",
    )(lam_vecs, gain, slopes, qvt, qe, k, ke, qvt)


def _pool_kernel(u_ref, pw_ref, ps_ref, o_ref, upad, *, n_body):
    l = n_body + N_META
    width = u_ref.shape[-1]
    upad[0:POOL_HALO, :] = jnp.zeros((POOL_HALO, width), f32)
    upad[POOL_HALO:POOL_HALO + N_META, :] = u_ref[0, n_body:l, :]
    upad[POOL_HALO + N_META:POOL_HALO + l, :] = u_ref[0, 0:n_body, :]
    upad[POOL_HALO + l:2 * POOL_HALO + l, :] = jnp.zeros((POOL_HALO, width), f32)

    def rows(pos0, n, out_row0):
        slab = upad[pl.ds(pos0, n + 2 * POOL_HALO), :]
        t = pos0 + lax.broadcasted_iota(jnp.int32, (n, 1), 0)
        for gi, w in enumerate(POOL_WINDOWS):
            cols = slice(gi * LANES, (gi + 1) * LANES)
            total = slab[POOL_HALO - w // 2:POOL_HALO - w // 2 + n, cols]
            for j in range(1, w):
                total = total + slab[POOL_HALO - w // 2 + j:POOL_HALO - w // 2 + j + n, cols]
            cnt = (jnp.minimum(t + w // 2, l) - jnp.maximum(t - w // 2, 0)).astype(f32)
            pooled = total / cnt - slab[POOL_HALO:POOL_HALO + n, cols]
            y = jnp.dot(pooled.astype(bf16), pw_ref[gi], preferred_element_type=f32) * ps_ref[:, cols]
            o_ref[0, pl.ds(out_row0, n), cols] = y.astype(bf16)

    rows(0, N_META, n_body)

    def body(c, carry):
        r0 = pl.multiple_of(c * POOL_CHUNK, POOL_CHUNK)
        rows(N_META + r0, POOL_CHUNK, r0)
        return carry

    lax.fori_loop(0, n_body // POOL_CHUNK, body, 0)


def _multiscale_pool(u, pool_w, pool_scale, n_body):
    b, l, width = u.shape
    kernel = functools.partial(_pool_kernel, n_body=n_body)
    return pl.pallas_call(
        kernel,
        grid=(b,),
        in_specs=[
            pl.BlockSpec((1, l, width), lambda bi: (bi, 0, 0)),
            pl.BlockSpec(pool_w.shape, lambda bi: (0, 0, 0)),
            pl.BlockSpec((1, width), lambda bi: (0, 0)),
        ],
        out_specs=pl.BlockSpec((1, l, width), lambda bi: (bi, 0, 0)),
        out_shape=jax.ShapeDtypeStruct((b, l, width), bf16),
        scratch_shapes=[pltpu.VMEM((l + 2 * POOL_HALO, width), f32)],
        compiler_params=pltpu.CompilerParams(
            dimension_semantics=("arbitrary",), vmem_limit_bytes=48 * MIB),
        name="multiscale_pool",
    )(u, pool_w, pool_scale)


def _outproj_router_kernel(h_ref, a_ref, p_ref, woa_ref, wop_ref, g_ref, rw_ref, h2_ref, hn_ref, aff_ref):
    tm = h_ref.shape[0]
    h2 = (h_ref[...]
          + jnp.dot(a_ref[...], woa_ref[...], preferred_element_type=f32)
          + jnp.dot(p_ref[...], wop_ref[...], preferred_element_type=f32))
    h2_ref[...] = h2
    hn = _rms(h2, g_ref[...])
    logits = jnp.dot(hn.astype(bf16), rw_ref[...], preferred_element_type=f32)
    lane = lax.broadcasted_iota(jnp.int32, logits.shape, 1)
    logits = jnp.where(lane < N_EXPERTS, logits, -jnp.inf)
    z = jnp.exp(logits - jnp.max(logits, axis=-1, keepdims=True))
    aff_ref[...] = z / jnp.sum(z, axis=-1, keepdims=True)
    for s in range(hn.shape[1] // LANES):
        hn_ref[pl.ds(s, tm, stride=SUBLANES), :] = hn[:, s * LANES:(s + 1) * LANES]


def _outproj_router(h, a, p, wo_a, wo_p, g, rw):
    m, d = h.shape
    half = a.shape[1]
    n_e = rw.shape[1]
    row = lambda i: (i, 0)
    fixed = lambda i: (0, 0)
    return pl.pallas_call(
        _outproj_router_kernel,
        grid=(pl.cdiv(m, ROW_TILE),),
        in_specs=[
            pl.BlockSpec((ROW_TILE, d), row),
            pl.BlockSpec((ROW_TILE, half), row),
            pl.BlockSpec((ROW_TILE, half), row),
            pl.BlockSpec((half, d), fixed),
            pl.BlockSpec((half, d), fixed),
            pl.BlockSpec((1, d), fixed),
            pl.BlockSpec((d, n_e), fixed),
        ],
        out_specs=[
            pl.BlockSpec((ROW_TILE, d), row),
            pl.BlockSpec((ROW_TILE * SUBLANES, LANES), row),
            pl.BlockSpec((ROW_TILE, n_e), row),
        ],
        out_shape=[
            jax.ShapeDtypeStruct((m, d), f32),
            jax.ShapeDtypeStruct((m * SUBLANES, LANES), f32),
            jax.ShapeDtypeStruct((m, n_e), f32),
        ],
        compiler_params=pltpu.CompilerParams(
            dimension_semantics=("arbitrary",), vmem_limit_bytes=40 * MIB),
        name="outproj_router",
    )(h, a, p, wo_a, wo_p, g, rw)


def _prefix_count(flags, w_scan):
    out = []
    offset = jnp.zeros((flags.shape[0], LANES), f32)
    for blk in range(flags.shape[1] // LANES):
        r = jnp.dot(flags[:, blk * LANES:(blk + 1) * LANES].astype(bf16), w_scan, preferred_element_type=f32)
        out.append(r[:, :LANES] + offset)
        offset = offset + r[:, LANES:]
    return jnp.concatenate(out, axis=1)


def _topk_kernel(aff_ref, idx_ref, rank_scr, chosen_scr, *, cap, n_body):
    n_e, lp = aff_ref.shape[1], aff_ref.shape[2]
    cp = idx_ref.shape[1]
    n_blk = lp // LANES
    bits = lax.bitcast_convert_type(aff_ref[0], jnp.int32)

    def refine(step, thr):
        cand = thr | (1 << (30 - step))
        cnt = jnp.sum(jnp.where(bits >= cand, 1, 0), axis=1, keepdims=True)
        return jnp.where(cnt >= cap, cand, thr)

    thr = lax.fori_loop(0, 31, refine, jnp.zeros((n_e, 1), jnp.int32))
    aff = aff_ref[0]
    pivot = jnp.max(jnp.where(bits == thr, aff, -1.0), axis=1, keepdims=True)
    above = aff > pivot
    tie = aff == pivot
    need = (cap - jnp.sum(jnp.where(above, 1, 0), axis=1, keepdims=True)).astype(f32)

    row = lax.broadcasted_iota(jnp.int32, (LANES, 2 * LANES), 0)
    col = lax.broadcasted_iota(jnp.int32, (LANES, 2 * LANES), 1)
    w_scan = jnp.where((row <= col) | (col >= LANES), 1.0, 0.0).astype(bf16)

    tie_rank = _prefix_count(jnp.where(tie, 1.0, 0.0), w_scan)
    chosen_scr[...] = jnp.where(above | (tie & (tie_rank <= need)), 1.0, 0.0)
    pos_f = lax.broadcasted_iota(jnp.int32, (n_e, lp), 1).astype(f32)
    big = float(2 * lp)

    def excess(cnt):
        return jnp.sum(jnp.abs(cnt - float(cap))) > 0.0

    def repair(cnt):
        a = aff_ref[0]
        sel = chosen_scr[...] > 0.5
        lo = jnp.min(jnp.where(sel, a, jnp.inf), axis=1, keepdims=True)
        lo_at = jnp.max(jnp.where(sel & (a == lo), pos_f, -1.0), axis=1, keepdims=True)
        drop = (pos_f == lo_at) & (cnt > float(cap))
        free = (~sel) & (a >= 0.0)
        hi = jnp.max(jnp.where(free, a, -1.0), axis=1, keepdims=True)
        hi_at = jnp.min(jnp.where(free & (a == hi), pos_f, big), axis=1, keepdims=True)
        add = (pos_f == hi_at) & (cnt < float(cap))
        new = jnp.where(add, 1.0, jnp.where(drop, 0.0, chosen_scr[...]))
        chosen_scr[...] = new
        return jnp.sum(new, axis=1, keepdims=True)

    lax.while_loop(excess, repair, jnp.sum(chosen_scr[...], axis=1, keepdims=True))
    rank = _prefix_count(chosen_scr[...], w_scan)
    for e in range(n_e):
        rank_scr[e] = rank[e:e + 1, :]

    lane = lax.broadcasted_iota(jnp.int32, (SUBLANES, LANES), 1)
    sub = lax.broadcasted_iota(jnp.int32, (SUBLANES, LANES), 0)

    n_groups = cp // SUBLANES
    unroll = next(u for u in (11, 8, 6, 5, 4, 3, 2, 1) if n_groups % u == 0)

    def per_expert(e, carry):
        def per_groups(gi, carry2):
            slots = [((gi * unroll + u) * SUBLANES + sub).astype(f32) for u in range(unroll)]
            accs = [jnp.zeros((SUBLANES, LANES), f32) for _ in range(unroll)]
            for t in range(n_blk):
                r_blk = jnp.broadcast_to(rank_scr[e, :, t * LANES:(t + 1) * LANES], (SUBLANES, LANES))
                accs = [a + jnp.where(r_blk <= c, 1.0, 0.0) for a, c in zip(accs, slots)]
            for u in range(unroll):
                pos = jnp.sum(accs[u], axis=1, keepdims=True).astype(jnp.int32)
                rows = jnp.where(pos < N_META, pos + n_body, pos - N_META)
                at = pl.ds(pl.multiple_of((gi * unroll + u) * SUBLANES, SUBLANES), SUBLANES)
                idx_ref[0, at, :] = jnp.where(lane == e, rows, idx_ref[0, at, :])
            return carry2

        return lax.fori_loop(0, n_groups // unroll, per_groups, carry)

    idx_ref[0] = jnp.zeros((cp, LANES), jnp.int32)
    lax.fori_loop(0, n_e, per_expert, 0)


def _topk_rows(aff_pos, cap, cp, n_body):
    b, n_e, lp = aff_pos.shape
    kernel = functools.partial(_topk_kernel, cap=cap, n_body=n_body)
    return pl.pallas_call(
        kernel,
        grid=(b,),
        in_specs=[pl.BlockSpec((1, n_e, lp), lambda bi: (bi, 0, 0))],
        out_specs=pl.BlockSpec((1, cp, LANES), lambda bi: (bi, 0, 0)),
        out_shape=jax.ShapeDtypeStruct((b, cp, LANES), jnp.int32),
        scratch_shapes=[pltpu.VMEM((n_e, 1, lp), f32), pltpu.VMEM((n_e, lp), f32)],
        compiler_params=pltpu.CompilerParams(
            dimension_semantics=("arbitrary",), vmem_limit_bytes=32 * MIB),
        name="topk_rows",
    )(aff_pos)


def _gather_kernel(idx_ref, src_ref, aff_ref, o_ref, gate_ref, tile):
    cp = o_ref.shape[1]
    sub = lax.broadcasted_iota(jnp.int32, (SUBLANES, LANES), 0)

    def group(g, carry):
        gates = jnp.zeros((SUBLANES, LANES), f32)
        for j in range(SUBLANES):
            slot = g * SUBLANES + j
            r = idx_ref[0, 0, slot]
            tile[pl.ds(pl.multiple_of(slot * SUBLANES, SUBLANES), SUBLANES), :] = (
                src_ref[0, pl.ds(pl.multiple_of(r * SUBLANES, SUBLANES), SUBLANES), :])
            aff8 = aff_ref[0, pl.ds(pl.multiple_of((r >> 3) << 3, SUBLANES), SUBLANES), :]
            gates = jnp.where(sub == j, pltpu.roll(aff8, (j + SUBLANES - (r & 7)) & 7, axis=0), gates)
        gate_ref[0, pl.ds(pl.multiple_of(g * SUBLANES, SUBLANES), SUBLANES), :] = gates
        return carry

    lax.fori_loop(0, cp // SUBLANES, group, 0)
    for s in range(o_ref.shape[2] // LANES):
        o_ref[0, :, s * LANES:(s + 1) * LANES] = tile[pl.ds(s, cp, stride=SUBLANES), :].astype(bf16)


def _gather_tokens(idx, hn_tiles, aff, d):
    b, n_e, cp = idx.shape
    rows = hn_tiles.shape[1]
    l = aff.shape[1]
    return pl.pallas_call(
        _gather_kernel,
        grid=(b, n_e),
        in_specs=[
            pl.BlockSpec((1, 1, cp), lambda bi, e: (bi * n_e + e, 0, 0), memory_space=pltpu.SMEM),
            pl.BlockSpec((1, rows, LANES), lambda bi, e: (bi, 0, 0)),
            pl.BlockSpec((1, l, LANES), lambda bi, e: (bi, 0, 0)),
        ],
        out_specs=[
            pl.BlockSpec((1, cp, d), lambda bi, e: (e, bi, 0)),
            pl.BlockSpec((1, cp, LANES), lambda bi, e: (e, bi, 0)),
        ],
        out_shape=[
            jax.ShapeDtypeStruct((n_e, b * cp, d), bf16),
            jax.ShapeDtypeStruct((n_e, b * cp, LANES), f32),
        ],
        scratch_shapes=[pltpu.VMEM((cp * SUBLANES, LANES), f32)],
        compiler_params=pltpu.CompilerParams(
            dimension_semantics=("arbitrary", "arbitrary"), vmem_limit_bytes=48 * MIB),
        name="gather_tokens",
    )(idx.reshape(b * n_e, 1, cp), hn_tiles, aff)


def _moe_kernel(x_ref, gate_ref, wg_ref, wu_ref, wd_ref, y_ref, wg_b, wu_b, wd_b, acc, *, cp):
    f = pl.program_id(1)
    n_f = pl.num_programs(1)
    wg_b[...] = wg_ref[0].astype(bf16)
    wu_b[...] = wu_ref[0].astype(bf16)
    wd_b[...] = wd_ref[0].astype(bf16)

    @pl.when(f == 0)
    def _():
        acc[...] = jnp.zeros(acc.shape, f32)

    n_chunks = acc.shape[0] // cp
    for c in range(n_chunks):
        rows = slice(c * cp, (c + 1) * cp)
        x = x_ref[0, rows, :]
        g = jnp.dot(x, wg_b[...], preferred_element_type=f32)
        u = jnp.dot(x, wu_b[...], preferred_element_type=f32)
        hid = (g * jax.nn.sigmoid(g) * u).astype(bf16)
        acc[rows, :] += jnp.dot(hid, wd_b[...], preferred_element_type=f32)

    @pl.when(f == n_f - 1)
    def _():
        for c in range(n_chunks):
            rows = slice(c * cp, (c + 1) * cp)
            lane = lax.broadcasted_iota(jnp.int32, (cp, LANES), 1)
            gate = jnp.sum(jnp.where(lane == pl.program_id(0), gate_ref[0, rows, :], 0.0),
                           axis=1, keepdims=True)
            y = acc[rows, :] * gate
            for s in range(y.shape[1] // LANES):
                y_ref[0, pl.ds(c * cp * SUBLANES + s, cp, stride=SUBLANES), :] = (
                    y[:, s * LANES:(s + 1) * LANES])


def _expert_ffn(xg, gate, w_gate, w_up, w_down, cp):
    n_e, m, d = xg.shape
    d_ff = w_gate.shape[2]
    kernel = functools.partial(_moe_kernel, cp=cp)
    return pl.pallas_call(
        kernel,
        grid=(n_e, d_ff // FF_TILE),
        in_specs=[
            pl.BlockSpec((1, m, d), lambda e, f: (e, 0, 0)),
            pl.BlockSpec((1, m, LANES), lambda e, f: (e, 0, 0)),
            pl.BlockSpec((1, d, FF_TILE), lambda e, f: (e, 0, f)),
            pl.BlockSpec((1, d, FF_TILE), lambda e, f: (e, 0, f)),
            pl.BlockSpec((1, FF_TILE, d), lambda e, f: (e, f, 0)),
        ],
        out_specs=pl.BlockSpec((1, m * SUBLANES, LANES), lambda e, f: (e, 0, 0)),
        out_shape=jax.ShapeDtypeStruct((n_e, m * SUBLANES, LANES), f32),
        scratch_shapes=[
            pltpu.VMEM((d, FF_TILE), bf16),
            pltpu.VMEM((d, FF_TILE), bf16),
            pltpu.VMEM((FF_TILE, d), bf16),
            pltpu.VMEM((m, d), f32),
        ],
        compiler_params=pltpu.CompilerParams(
            dimension_semantics=("arbitrary", "arbitrary"), vmem_limit_bytes=56 * MIB),
        name="expert_ffn",
    )(xg, gate, w_gate, w_up, w_down)


def _combine_kernel(idx_ref, y_ref, h2_ref, g_ref, o_ref, acc, *, n_experts):
    step = pl.program_id(1)
    cp = idx_ref.shape[2]
    tc = o_ref.shape[1]

    @pl.when(step == 0)
    def _():
        acc[...] = jnp.zeros(acc.shape, f32)

    @pl.when(step < n_experts)
    def _():
        def group(g, carry):
            dst = []
            val = []
            for j in range(SUBLANES):
                slot = g * SUBLANES + j
                r = pl.multiple_of(idx_ref[0, 0, slot] * SUBLANES, SUBLANES)
                dst.append(r)
                val.append(acc[pl.ds(r, SUBLANES), :]
                           + y_ref[0, pl.ds(pl.multiple_of(slot * SUBLANES, SUBLANES), SUBLANES), :])
            for r, v in zip(dst, val):
                acc[pl.ds(r, SUBLANES), :] = v
            return carry

        lax.fori_loop(0, cp // SUBLANES, group, 0)

    @pl.when(step >= n_experts)
    def _():
        r0 = pl.multiple_of((step - n_experts) * tc * SUBLANES, SUBLANES)
        moe = jnp.concatenate(
            [acc[pl.ds(r0 + s, tc, stride=SUBLANES), :] for s in range(o_ref.shape[2] // LANES)], axis=-1)
        o_ref[0] = _rms(h2_ref[0] + moe, g_ref[...])


def _combine(idx, y, h2, g, n_body):
    b, n_e, cp = idx.shape
    _, l, d = h2.shape
    n_out = n_body // OUT_CHUNK
    kernel = functools.partial(_combine_kernel, n_experts=n_e)
    expert = lambda s: jnp.minimum(s, n_e - 1)
    chunk = lambda s: jnp.maximum(s - n_e, 0)
    return pl.pallas_call(
        kernel,
        grid=(b, n_e + n_out),
        in_specs=[
            pl.BlockSpec((1, 1, cp), lambda bi, s: (bi * n_e + expert(s), 0, 0), memory_space=pltpu.SMEM),
            pl.BlockSpec((1, cp * SUBLANES, LANES), lambda bi, s: (expert(s), bi, 0)),
            pl.BlockSpec((1, OUT_CHUNK, d), lambda bi, s: (bi, chunk(s), 0)),
            pl.BlockSpec((1, d), lambda bi, s: (0, 0)),
        ],
        out_specs=pl.BlockSpec((1, OUT_CHUNK, d), lambda bi, s: (bi, chunk(s), 0)),
        out_shape=jax.ShapeDtypeStruct((b, n_body, d), f32),
        scratch_shapes=[pltpu.VMEM(((l + 1) * SUBLANES, LANES), f32)],
        compiler_params=pltpu.CompilerParams(
            dimension_semantics=("arbitrary", "arbitrary"), vmem_limit_bytes=48 * MIB),
        name="combine",
    )(idx.reshape(b * n_e, 1, cp), y, h2, g)


def _round_up(n, k):
    return -(-n // k) * k


def kernel(x, meta_tokens, g_mix, w_in, lam_vecs, subln_gain, pool_w, pool_scale, w_o, g_ffn, router_w,
           w_gate, w_up, w_down, g_final):
    b, n_body, d = x.shape
    l = n_body + N_META
    qk = HEADS * D_QK
    attn_width = HEADS * D_V
    depth = g_mix.shape[0]
    cap = CAPACITY_FACTOR * l // N_EXPERTS
    cp = _round_up(cap, BF16_ROWS)

    meta = jnp.broadcast_to(meta_tokens[None].astype(x.dtype), (b, N_META, d))
    h = jnp.concatenate([x, meta], axis=1)
    slopes = jnp.broadcast_to(
        jnp.array([2.0 ** (-8.0 * (i + 1) / HEADS) for i in range(HEADS)], f32)[:, None, None],
        (HEADS, 1, LANES))
    cols = jnp.arange(HEADS)[:, None] * D_QK + jnp.arange(D_QK)[None, :]
    q_cols = jnp.concatenate([cols, qk + cols], axis=1).reshape(-1)
    k_cols = q_cols + 2 * qk
    v_cols = 4 * qk + jnp.arange(attn_width)
    u_cols = jnp.arange(4 * qk + attn_width, w_in.shape[2])

    out = None
    for layer in range(depth):
        lam_init = 0.8 - 0.6 * math.exp(-0.3 * layer)
        w = w_in[layer]
        w_qv_t = jnp.concatenate([w[:, q_cols] * (1.0 / math.sqrt(D_QK)), w[:, v_cols]], axis=1).T.astype(bf16)
        w_ku = jnp.concatenate([w[:, k_cols], w[:, u_cols]], axis=1).astype(bf16)
        qvt, k, u = _norm_inproj(h, g_mix[layer][None, :], w_qv_t, w_ku, 2 * qk)
        a = _diff_attention(qvt, k, lam_vecs[layer].astype(f32), subln_gain[layer][None, :],
                            slopes, n_body, lam_init)
        p = _multiscale_pool(u, pool_w[layer].astype(bf16), pool_scale[layer][None, :], n_body)
        wo = w_o[layer].astype(bf16)
        h2, hn_tiles, aff = _outproj_router(
            h.reshape(b * l, d), a.reshape(b * l, -1), p.reshape(b * l, -1), wo[:attn_width], wo[attn_width:],
            g_ffn[layer][None, :],
            jnp.pad(router_w[layer], ((0, 0), (0, LANES - N_EXPERTS))).astype(bf16))

        aff = aff.reshape(b, l, LANES)
        lp = _round_up(l, LANES)
        aff_pos = jnp.concatenate(
            [aff[:, n_body:, :N_EXPERTS], aff[:, :n_body, :N_EXPERTS], jnp.full((b, lp - l, N_EXPERTS), -1.0, f32)],
            axis=1)
        idx = _topk_rows(jnp.swapaxes(aff_pos, 1, 2), cap, cp, n_body)
        idx = jnp.swapaxes(idx[:, :, :N_EXPERTS], 1, 2)
        real = jnp.arange(cp)[None, None, :] < cap
        idx_gather = jnp.where(real, idx, 0)
        idx_scatter = jnp.where(real, idx, l)

        xg, gate = _gather_tokens(idx_gather, hn_tiles.reshape(b, l * SUBLANES, LANES), aff, d)
        y = _expert_ffn(xg, gate, w_gate[layer], w_up[layer], w_down[layer], cp)
        last = layer == depth - 1
        g_out = g_final[None, :] if last else jnp.ones((1, d), f32)
        out = _combine(idx_scatter, y, h2.reshape(b, l, d), g_out, n_body)
    return out
```

```python
import functools
import math

import jax
import jax.numpy as jnp
from jax import lax
from jax.experimental import pallas as pl
from jax.experimental.pallas import tpu as pltpu

N_META = 16
HEADS = 4
D_QK = 64
D_V = 128
POOL_WINDOWS = (2, 4, 8, 16)
POOL_HALO = max(POOL_WINDOWS) // 2
N_EXPERTS = 16
CAPACITY_FACTOR = 2
EPS = 1e-6
LOG2E = 1.4426950408889634


def _alibi_slope(head):
    return 2.0 ** (-8.0 * (head + 1) / HEADS)


LANES = 128
SUBLANES = 8
BF16_ROWS = 16

ROW_TILE = 512
ATTN_TQ = 256
ATTN_TK = 512
POOL_CHUNK = 256
FF_TILE = 256
OUT_CHUNK = 512
MIB = 1024 * 1024

f32 = jnp.float32
bf16 = jnp.bfloat16


def _rms(x, g):
    return x * lax.rsqrt(jnp.mean(x * x, axis=-1, keepdims=True) + EPS) * g


def _norm_inproj_kernel(x_ref, g_ref, wt_ref, w_ref, qvt_ref, k_ref, u_ref, *, seq_len):
    x = x_ref[0]
    valid = seq_len - pl.program_id(1) * x.shape[0]
    x = jnp.where(lax.broadcasted_iota(jnp.int32, x.shape, 0) < valid, x, 0.0)
    n = _rms(x, g_ref[...]).astype(bf16)
    qvt = lax.dot_general(wt_ref[...], n, (((1,), (1,)), ((), ())), preferred_element_type=f32)
    qvt_ref[0, 0] = qvt.astype(bf16)
    ku = jnp.dot(n, w_ref[...], preferred_element_type=f32)
    n_k = k_ref.shape[-1]
    k_ref[0] = ku[:, :n_k].astype(bf16)
    u_ref[0] = ku[:, n_k:]


def _norm_inproj(h, g, w_qv_t, w_ku, n_k):
    b, l, d = h.shape
    n_t = pl.cdiv(l, ATTN_TK)
    n_qv = w_qv_t.shape[0]
    n_u = w_ku.shape[1] - n_k
    return pl.pallas_call(
        functools.partial(_norm_inproj_kernel, seq_len=l),
        grid=(b, n_t),
        in_specs=[
            pl.BlockSpec((1, ATTN_TK, d), lambda bi, i: (bi, i, 0)),
            pl.BlockSpec((1, d), lambda bi, i: (0, 0)),
            pl.BlockSpec(w_qv_t.shape, lambda bi, i: (0, 0)),
            pl.BlockSpec(w_ku.shape, lambda bi, i: (0, 0)),
        ],
        out_specs=[
            pl.BlockSpec((1, 1, n_qv, ATTN_TK), lambda bi, i: (bi, i, 0, 0)),
            pl.BlockSpec((1, ATTN_TK, n_k), lambda bi, i: (bi, i, 0)),
            pl.BlockSpec((1, ATTN_TK, n_u), lambda bi, i: (bi, i, 0)),
        ],
        out_shape=[
            jax.ShapeDtypeStruct((b, n_t, n_qv, ATTN_TK), bf16),
            jax.ShapeDtypeStruct((b, l, n_k), bf16),
            jax.ShapeDtypeStruct((b, l, n_u), f32),
        ],
        compiler_params=pltpu.CompilerParams(
            dimension_semantics=("arbitrary", "arbitrary"), vmem_limit_bytes=40 * MIB),
        name="norm_inproj",
    )(h, g, w_qv_t, w_ku)


def _attn_kernel(lam_ref, gain_ref, slope_ref, q_ref, qe_ref, k_ref, ke_ref, vt_ref, o_ref,
                 qs_scr, bias_scr, tail_scr, vt_scr, *scr, n_body, lam_init):
    tq = ATTN_TQ
    tk = ATTN_TK
    n_diag = tk // tq
    n_sub = q_ref.shape[3] // tq
    i = pl.program_id(2)
    n_q = pl.num_programs(2)
    is_meta = i == n_q - 1
    slope = slope_ref[0][:, :1]

    @pl.when(i == 0)
    def _():
        d = (lax.broadcasted_iota(jnp.int32, (tk, tq), 0)
             - lax.broadcasted_iota(jnp.int32, (tk, tq), 1)).astype(f32)
        for v in range(n_diag):
            bias_scr[v] = -slope * jnp.abs(d - float(v * tq))
        tail_scr[0] = slope * d[0:N_META]
        tail_scr[1] = -slope * jnp.abs(d[0:N_META])
        ones_row = jnp.where(lax.broadcasted_iota(jnp.int32, (BF16_ROWS, tk), 0) == 0, 1.0, 0.0).astype(bf16)
        for c in range(vt_ref.shape[1]):
            vt_scr[c, 0:D_V, :] = vt_ref[0, c]
            vt_scr[c, D_V:, :] = ones_row
        width = qs_scr.shape[2]
        qs_scr[:, 2 * D_QK + BF16_ROWS:, :] = jnp.zeros((3, 2 * LANES - 2 * D_QK - BF16_ROWS, width), bf16)
        qs_scr[2, 2 * D_QK:2 * D_QK + BF16_ROWS, :] = jnp.zeros((BF16_ROWS, width), bf16)

    zero = jnp.zeros((D_QK, tq), bf16)
    for sub in range(n_sub):
        q = q_ref[0, 0, :, sub * tq:(sub + 1) * tq]
        cols = slice(2 * sub * tq, 2 * (sub + 1) * tq)
        top = jnp.concatenate([q[0:D_QK], zero], axis=1)
        bottom = jnp.concatenate([zero, q[D_QK:]], axis=1)
        for v in range(3):
            qs_scr[v, 0:D_QK, cols] = top
            qs_scr[v, D_QK:2 * D_QK, cols] = bottom
        for v in range(2):
            ext = qe_ref[0, v, :, sub * tq:(sub + 1) * tq]
            qs_scr[v, 2 * D_QK:2 * D_QK + BF16_ROWS, cols] = jnp.concatenate([ext, ext], axis=1)

    n_streams = 2 * n_sub
    per_stream = len(scr) // n_streams
    streams = [scr[st * per_stream:(st + 1) * per_stream] for st in range(n_streams)]
    for _, _, _, _, _, _, m_scr, acc_scr in streams:
        m_scr[...] = jnp.full(m_scr.shape, -jnp.inf, f32)
        acc_scr[...] = jnp.zeros(acc_scr.shape, f32)

    q_pos0 = jnp.where(is_meta, 0, N_META + i * n_sub * tq)
    q_valid = jnp.where(is_meta, N_META, n_sub * tq)
    n_slab = tq // LANES
    piece = BF16_ROWS

    def issue_scores(k, version, slot):
        n = k.shape[0]
        for st, stream in enumerate(streams):
            stream[slot][0:n, :] = jnp.dot(k, qs_scr[version, 0:k.shape[1], st * tq:(st + 1) * tq],
                                           preferred_element_type=f32)

    def softmax(slot, n, bias_tile=None, consts=None):
        def scores(s_scr, r, slab):
            t = s_scr[r:r + piece, slab * LANES:(slab + 1) * LANES]
            return t if bias_tile is None else t + bias_tile(r, slab)

        stats = []
        for st, stream in enumerate(streams):
            s_scr, m_scr = stream[slot], stream[6]
            const = None if consts is None else consts[st // 2]
            m_old = m_scr[...]
            maxes = []
            for slab in range(n_slab):
                mx = scores(s_scr, 0, slab)
                for r in range(piece, n, piece):
                    mx = jnp.maximum(mx, scores(s_scr, r, slab))
                maxes.append(jnp.max(mx, axis=0, keepdims=True))
            m_loc = jnp.concatenate(maxes, axis=1)
            m_new = jnp.maximum(m_old, m_loc if const is None else m_loc + const)
            stats.append((m_new, m_new if const is None else m_new - const, jnp.exp2(m_old - m_new)))

        for stream, (m_new, shift, alpha) in zip(streams, stats):
            s_scr, p_scr, a_scr, m_scr = stream[slot], stream[2 + slot], stream[4 + slot], stream[6]
            for slab in range(n_slab):
                cols = slice(slab * LANES, (slab + 1) * LANES)
                sub_b = jnp.broadcast_to(shift[:, cols], (piece, LANES))
                for r in range(0, n, piece):
                    p_scr[r:r + piece, cols] = jnp.exp2(scores(s_scr, r, slab) - sub_b).astype(bf16)
            a_scr[...] = alpha
            m_scr[...] = m_new

    def apply_values(vt, slot):
        n = vt.shape[1]
        for stream in streams:
            p_scr, a_scr, acc_scr = stream[2 + slot], stream[4 + slot], stream[7]
            acc_scr[...] = a_scr[...] * acc_scr[...] + jnp.dot(vt, p_scr[0:n, :], preferred_element_type=f32)

    def keys_with_positions(c):
        rows = pl.ds(pl.multiple_of(c * tk, tk), tk)
        return jnp.concatenate([k_ref[0, rows, :], ke_ref[0, rows, :]], axis=1)

    n_chunks = n_body // tk

    def issue_chunk_scores(c, slot):
        k_pos0 = N_META + c * tk
        version = jnp.where(k_pos0 + tk <= q_pos0, 0, jnp.where(k_pos0 >= q_pos0 + q_valid, 1, 2))
        issue_scores(keys_with_positions(c), version, slot)

    def add_straddle_bias(c, slot):
        k_pos0 = N_META + c * tk

        @pl.when(jnp.logical_and(k_pos0 + tk > q_pos0, k_pos0 < q_pos0 + q_valid))
        def _():
            for st, stream in enumerate(streams):
                variant = (q_pos0 - k_pos0) // tq + st // 2
                for slab in range(n_slab):
                    cols = slice(slab * LANES, (slab + 1) * LANES)
                    for r in range(0, tk, piece):
                        stream[slot][r:r + piece, cols] = (stream[slot][r:r + piece, cols]
                                                           + bias_scr[variant, r:r + piece, cols])

    def step(t, slot, issue_next, apply_prev):
        add_straddle_bias(t, slot)
        if issue_next:
            issue_chunk_scores(t + 1, 1 - slot)
        softmax(slot, tk)
        if apply_prev:
            apply_values(vt_scr[t - 1], 1 - slot)

    def pair(u, carry):
        step(2 * u, 0, True, True)
        step(2 * u + 1, 1, True, True)
        return carry

    issue_chunk_scores(0, 0)
    step(0, 0, True, False)
    step(1, 1, True, True)
    lax.fori_loop(1, n_chunks // 2 - 1, pair, 0)
    step(n_chunks - 2, 0, True, True)
    step(n_chunks - 1, 1, False, True)
    apply_values(vt_scr[n_chunks - 1], 1)

    tail = jnp.where(is_meta, 1, 0)
    consts = [-slope * jnp.full((1, 1), q_pos0 + sub * tq, jnp.int32).astype(f32) for sub in range(n_sub)]
    issue_scores(k_ref[0, n_body:n_body + N_META, :], 2, 0)
    softmax(0, N_META, lambda r, slab: tail_scr[tail, r:r + piece, slab * LANES:(slab + 1) * LANES], consts)
    apply_values(vt_scr[n_chunks, :, 0:N_META], 0)

    lv = lam_ref[...]
    lam = (jnp.exp(jnp.sum(lv[0:1] * lv[1:2], axis=-1, keepdims=True))
           - jnp.exp(jnp.sum(lv[2:3] * lv[3:4], axis=-1, keepdims=True)) + lam_init)
    for sub in range(n_sub):
        o1, o2 = [stream[7][0:D_V, :] / stream[7][D_V:D_V + 1, :] for stream in streams[2 * sub:2 * sub + 2]]
        att = (o1 - lam * o2).T
        o_ref[0, sub * tq:(sub + 1) * tq, :] = (_rms(att, gain_ref[...]) * (1.0 - lam_init)).astype(bf16)


def _alibi_tables(n_body, n_q_tiles):
    l = n_body + N_META
    pos = jnp.concatenate([jnp.arange(n_body) + N_META, jnp.arange(N_META)]).astype(f32)
    q_pos = jnp.pad(pos, (0, n_q_tiles * ATTN_TQ - l))
    k_hi, k_lo = jnp.floor(pos / 64), jnp.mod(pos, 64)
    q_hi, q_lo = jnp.floor(q_pos / 64), jnp.mod(q_pos, 64)
    ke, qe = [], []
    for h in range(HEADS):
        c = jnp.float32(_alibi_slope(h) * LOG2E)
        c1 = c.astype(bf16).astype(f32)
        c2 = (c - c1).astype(bf16).astype(f32)
        c3 = (c - c1 - c2).astype(bf16).astype(f32)
        pieces = [c1, c2, c3]
        cols = [k_hi] * 3 + [k_lo] * 3 + [jnp.full_like(pos, p) for p in pieces] * 2
        ke.append(jnp.pad(jnp.stack(cols, axis=1), ((0, 0), (0, LANES - len(cols)))))
        rows = ([jnp.full_like(q_pos, 64 * p) for p in pieces] + [jnp.full_like(q_pos, p) for p in pieces]
                + [-64 * q_hi] * 3 + [-q_lo] * 3)
        ext = jnp.pad(jnp.stack(rows), ((0, BF16_ROWS - len(rows)), (0, 0)))
        qe.append(jnp.stack([ext, -ext]))
    return jnp.stack(ke).astype(bf16), jnp.stack(qe).astype(bf16)


def _diff_attention(qvt, k, lam_vecs, gain, slopes, n_body, lam_init):
    b, l, _ = k.shape
    n_t = qvt.shape[1]
    per_chunk = ATTN_TK // ATTN_TQ
    n_q = pl.cdiv(l, ATTN_TK)
    assert ATTN_TK % ATTN_TQ == 0 and n_body % (2 * ATTN_TK) == 0 and n_body >= 4 * ATTN_TK
    ke, qe = _alibi_tables(n_body, n_q * per_chunk)
    kernel = functools.partial(_attn_kernel, n_body=n_body, lam_init=lam_init)
    return pl.pallas_call(
        kernel,
        grid=(b, HEADS, n_q),
        in_specs=[
            pl.BlockSpec(lam_vecs.shape, lambda bi, h, i: (0, 0)),
            pl.BlockSpec((1, D_V), lambda bi, h, i: (0, 0)),
            pl.BlockSpec((1, 1, LANES), lambda bi, h, i: (h, 0, 0)),
            pl.BlockSpec((1, 1, 2 * D_QK, ATTN_TK), lambda bi, h, i: (bi, i, h, 0)),
            pl.BlockSpec((1, 2, BF16_ROWS, ATTN_TK), lambda bi, h, i: (h, 0, 0, i)),
            pl.BlockSpec((1, l, 2 * D_QK), lambda bi, h, i: (bi, 0, h)),
            pl.BlockSpec((1, l, LANES), lambda bi, h, i: (h, 0, 0)),
            pl.BlockSpec((1, n_t, D_V, ATTN_TK), lambda bi, h, i: (bi, 0, HEADS + h, 0)),
        ],
        out_specs=pl.BlockSpec((1, ATTN_TK, D_V), lambda bi, h, i: (bi, i, h)),
        out_shape=jax.ShapeDtypeStruct((b, l, HEADS * D_V), bf16),
        scratch_shapes=[
            pltpu.VMEM((3, 2 * LANES, 2 * ATTN_TK), bf16),
            pltpu.VMEM((per_chunk, ATTN_TK, ATTN_TQ), f32),
            pltpu.VMEM((2, N_META, ATTN_TQ), f32),
            pltpu.VMEM((n_t, D_V + BF16_ROWS, ATTN_TK), bf16),
        ] + 2 * per_chunk * [
            pltpu.VMEM((ATTN_TK, ATTN_TQ), f32), pltpu.VMEM((ATTN_TK, ATTN_TQ), f32),
            pltpu.VMEM((ATTN_TK, ATTN_TQ), bf16), pltpu.VMEM((ATTN_TK, ATTN_TQ), bf16),
            pltpu.VMEM((1, ATTN_TQ), f32), pltpu.VMEM((1, ATTN_TQ), f32),
            pltpu.VMEM((1, ATTN_TQ), f32),
            pltpu.VMEM((D_V + BF16_ROWS, ATTN_TQ), f32),
        ],
        compiler_params=pltpu.CompilerParams(
            dimension_semantics=("arbitrary", "arbitrary", "arbitrary"), vmem_limit_bytes=40 * MIB),
        name="diff_attention",
    )(lam_vecs, gain, slopes, qvt, qe, k, ke, qvt)


def _pool_kernel(u_ref, pw_ref, ps_ref, o_ref, upad, *, n_body):
    l = n_body + N_META
    width = u_ref.shape[-1]
    upad[0:POOL_HALO, :] = jnp.zeros((POOL_HALO, width), f32)
    upad[POOL_HALO:POOL_HALO + N_META, :] = u_ref[0, n_body:l, :]
    upad[POOL_HALO + N_META:POOL_HALO + l, :] = u_ref[0, 0:n_body, :]
    upad[POOL_HALO + l:2 * POOL_HALO + l, :] = jnp.zeros((POOL_HALO, width), f32)

    def rows(pos0, n, out_row0):
        slab = upad[pl.ds(pos0, n + 2 * POOL_HALO), :]
        t = pos0 + lax.broadcasted_iota(jnp.int32, (n, 1), 0)
        for gi, w in enumerate(POOL_WINDOWS):
            cols = slice(gi * LANES, (gi + 1) * LANES)
            total = slab[POOL_HALO - w // 2:POOL_HALO - w // 2 + n, cols]
            for j in range(1, w):
                total = total + slab[POOL_HALO - w // 2 + j:POOL_HALO - w // 2 + j + n, cols]
            cnt = (jnp.minimum(t + w // 2, l) - jnp.maximum(t - w // 2, 0)).astype(f32)
            pooled = total / cnt - slab[POOL_HALO:POOL_HALO + n, cols]
            y = jnp.dot(pooled.astype(bf16), pw_ref[gi], preferred_element_type=f32) * ps_ref[:, cols]
            o_ref[0, pl.ds(out_row0, n), cols] = y.astype(bf16)

    rows(0, N_META, n_body)

    def body(c, carry):
        r0 = pl.multiple_of(c * POOL_CHUNK, POOL_CHUNK)
        rows(N_META + r0, POOL_CHUNK, r0)
        return carry

    lax.fori_loop(0, n_body // POOL_CHUNK, body, 0)


def _multiscale_pool(u, pool_w, pool_scale, n_body):
    b, l, width = u.shape
    kernel = functools.partial(_pool_kernel, n_body=n_body)
    return pl.pallas_call(
        kernel,
        grid=(b,),
        in_specs=[
            pl.BlockSpec((1, l, width), lambda bi: (bi, 0, 0)),
            pl.BlockSpec(pool_w.shape, lambda bi: (0, 0, 0)),
            pl.BlockSpec((1, width), lambda bi: (0, 0)),
        ],
        out_specs=pl.BlockSpec((1, l, width), lambda bi: (bi, 0, 0)),
        out_shape=jax.ShapeDtypeStruct((b, l, width), bf16),
        scratch_shapes=[pltpu.VMEM((l + 2 * POOL_HALO, width), f32)],
        compiler_params=pltpu.CompilerParams(
            dimension_semantics=("arbitrary",), vmem_limit_bytes=48 * MIB),
        name="multiscale_pool",
    )(u, pool_w, pool_scale)


def _outproj_router_kernel(h_ref, a_ref, p_ref, woa_ref, wop_ref, g_ref, rw_ref, h2_ref, hn_ref, aff_ref):
    tm = h_ref.shape[0]
    h2 = (h_ref[...]
          + jnp.dot(a_ref[...], woa_ref[...], preferred_element_type=f32)
          + jnp.dot(p_ref[...], wop_ref[...], preferred_element_type=f32))
    h2_ref[...] = h2
    hn = _rms(h2, g_ref[...])
    logits = jnp.dot(hn.astype(bf16), rw_ref[...], preferred_element_type=f32)
    lane = lax.broadcasted_iota(jnp.int32, logits.shape, 1)
    logits = jnp.where(lane < N_EXPERTS, logits, -jnp.inf)
    z = jnp.exp(logits - jnp.max(logits, axis=-1, keepdims=True))
    aff_ref[...] = z / jnp.sum(z, axis=-1, keepdims=True)
    for s in range(hn.shape[1] // LANES):
        hn_ref[pl.ds(s, tm, stride=SUBLANES), :] = hn[:, s * LANES:(s + 1) * LANES]


def _outproj_router(h, a, p, wo_a, wo_p, g, rw):
    m, d = h.shape
    half = a.shape[1]
    n_e = rw.shape[1]
    row = lambda i: (i, 0)
    fixed = lambda i: (0, 0)
    return pl.pallas_call(
        _outproj_router_kernel,
        grid=(pl.cdiv(m, ROW_TILE),),
        in_specs=[
            pl.BlockSpec((ROW_TILE, d), row),
            pl.BlockSpec((ROW_TILE, half), row),
            pl.BlockSpec((ROW_TILE, half), row),
            pl.BlockSpec((half, d), fixed),
            pl.BlockSpec((half, d), fixed),
            pl.BlockSpec((1, d), fixed),
            pl.BlockSpec((d, n_e), fixed),
        ],
        out_specs=[
            pl.BlockSpec((ROW_TILE, d), row),
            pl.BlockSpec((ROW_TILE * SUBLANES, LANES), row),
            pl.BlockSpec((ROW_TILE, n_e), row),
        ],
        out_shape=[
            jax.ShapeDtypeStruct((m, d), f32),
            jax.ShapeDtypeStruct((m * SUBLANES, LANES), f32),
            jax.ShapeDtypeStruct((m, n_e), f32),
        ],
        compiler_params=pltpu.CompilerParams(
            dimension_semantics=("arbitrary",), vmem_limit_bytes=40 * MIB),
        name="outproj_router",
    )(h, a, p, wo_a, wo_p, g, rw)


def _prefix_count(flags, w_scan):
    out = []
    offset = jnp.zeros((flags.shape[0], LANES), f32)
    for blk in range(flags.shape[1] // LANES):
        r = jnp.dot(flags[:, blk * LANES:(blk + 1) * LANES].astype(bf16), w_scan, preferred_element_type=f32)
        out.append(r[:, :LANES] + offset)
        offset = offset + r[:, LANES:]
    return jnp.concatenate(out, axis=1)


def _topk_kernel(aff_ref, idx_ref, rank_scr, chosen_scr, *, cap, n_body):
    n_e, lp = aff_ref.shape[1], aff_ref.shape[2]
    cp = idx_ref.shape[1]
    n_blk = lp // LANES
    bits = lax.bitcast_convert_type(aff_ref[0], jnp.int32)

    def refine(step, thr):
        cand = thr | (1 << (30 - step))
        cnt = jnp.sum(jnp.where(bits >= cand, 1, 0), axis=1, keepdims=True)
        return jnp.where(cnt >= cap, cand, thr)

    thr = lax.fori_loop(0, 31, refine, jnp.zeros((n_e, 1), jnp.int32))
    aff = aff_ref[0]
    pivot = jnp.max(jnp.where(bits == thr, aff, -1.0), axis=1, keepdims=True)
    above = aff > pivot
    tie = aff == pivot
    need = (cap - jnp.sum(jnp.where(above, 1, 0), axis=1, keepdims=True)).astype(f32)

    row = lax.broadcasted_iota(jnp.int32, (LANES, 2 * LANES), 0)
    col = lax.broadcasted_iota(jnp.int32, (LANES, 2 * LANES), 1)
    w_scan = jnp.where((row <= col) | (col >= LANES), 1.0, 0.0).astype(bf16)

    tie_rank = _prefix_count(jnp.where(tie, 1.0, 0.0), w_scan)
    chosen_scr[...] = jnp.where(above | (tie & (tie_rank <= need)), 1.0, 0.0)
    pos_f = lax.broadcasted_iota(jnp.int32, (n_e, lp), 1).astype(f32)
    big = float(2 * lp)

    def excess(cnt):
        return jnp.sum(jnp.abs(cnt - float(cap))) > 0.0

    def repair(cnt):
        a = aff_ref[0]
        sel = chosen_scr[...] > 0.5
        lo = jnp.min(jnp.where(sel, a, jnp.inf), axis=1, keepdims=True)
        lo_at = jnp.max(jnp.where(sel & (a == lo), pos_f, -1.0), axis=1, keepdims=True)
        drop = (pos_f == lo_at) & (cnt > float(cap))
        free = (~sel) & (a >= 0.0)
        hi = jnp.max(jnp.where(free, a, -1.0), axis=1, keepdims=True)
        hi_at = jnp.min(jnp.where(free & (a == hi), pos_f, big), axis=1, keepdims=True)
        add = (pos_f == hi_at) & (cnt < float(cap))
        new = jnp.where(add, 1.0, jnp.where(drop, 0.0, chosen_scr[...]))
        chosen_scr[...] = new
        return jnp.sum(new, axis=1, keepdims=True)

    lax.while_loop(excess, repair, jnp.sum(chosen_scr[...], axis=1, keepdims=True))
    rank = _prefix_count(chosen_scr[...], w_scan)
    for e in range(n_e):
        rank_scr[e] = rank[e:e + 1, :]

    lane = lax.broadcasted_iota(jnp.int32, (SUBLANES, LANES), 1)
    sub = lax.broadcasted_iota(jnp.int32, (SUBLANES, LANES), 0)

    n_groups = cp // SUBLANES
    unroll = next(u for u in (11, 8, 6, 5, 4, 3, 2, 1) if n_groups % u == 0)

    def per_expert(e, carry):
        def per_groups(gi, carry2):
            slots = [((gi * unroll + u) * SUBLANES + sub).astype(f32) for u in range(unroll)]
            accs = [jnp.zeros((SUBLANES, LANES), f32) for _ in range(unroll)]
            for t in range(n_blk):
                r_blk = jnp.broadcast_to(rank_scr[e, :, t * LANES:(t + 1) * LANES], (SUBLANES, LANES))
                accs = [a + jnp.where(r_blk <= c, 1.0, 0.0) for a, c in zip(accs, slots)]
            for u in range(unroll):
                pos = jnp.sum(accs[u], axis=1, keepdims=True).astype(jnp.int32)
                rows = jnp.where(pos < N_META, pos + n_body, pos - N_META)
                at = pl.ds(pl.multiple_of((gi * unroll + u) * SUBLANES, SUBLANES), SUBLANES)
                idx_ref[0, at, :] = jnp.where(lane == e, rows, idx_ref[0, at, :])
            return carry2

        return lax.fori_loop(0, n_groups // unroll, per_groups, carry)

    idx_ref[0] = jnp.zeros((cp, LANES), jnp.int32)
    lax.fori_loop(0, n_e, per_expert, 0)


def _topk_rows(aff_pos, cap, cp, n_body):
    b, n_e, lp = aff_pos.shape
    kernel = functools.partial(_topk_kernel, cap=cap, n_body=n_body)
    return pl.pallas_call(
        kernel,
        grid=(b,),
        in_specs=[pl.BlockSpec((1, n_e, lp), lambda bi: (bi, 0, 0))],
        out_specs=pl.BlockSpec((1, cp, LANES), lambda bi: (bi, 0, 0)),
        out_shape=jax.ShapeDtypeStruct((b, cp, LANES), jnp.int32),
        scratch_shapes=[pltpu.VMEM((n_e, 1, lp), f32), pltpu.VMEM((n_e, lp), f32)],
        compiler_params=pltpu.CompilerParams(
            dimension_semantics=("arbitrary",), vmem_limit_bytes=32 * MIB),
        name="topk_rows",
    )(aff_pos)


def _gather_kernel(idx_ref, src_ref, aff_ref, o_ref, gate_ref, tile):
    cp = o_ref.shape[1]
    sub = lax.broadcasted_iota(jnp.int32, (SUBLANES, LANES), 0)

    def group(g, carry):
        gates = jnp.zeros((SUBLANES, LANES), f32)
        for j in range(SUBLANES):
            slot = g * SUBLANES + j
            r = idx_ref[0, 0, slot]
            tile[pl.ds(pl.multiple_of(slot * SUBLANES, SUBLANES), SUBLANES), :] = (
                src_ref[0, pl.ds(pl.multiple_of(r * SUBLANES, SUBLANES), SUBLANES), :])
            aff8 = aff_ref[0, pl.ds(pl.multiple_of((r >> 3) << 3, SUBLANES), SUBLANES), :]
            gates = jnp.where(sub == j, pltpu.roll(aff8, (j + SUBLANES - (r & 7)) & 7, axis=0), gates)
        gate_ref[0, pl.ds(pl.multiple_of(g * SUBLANES, SUBLANES), SUBLANES), :] = gates
        return carry

    lax.fori_loop(0, cp // SUBLANES, group, 0)
    for s in range(o_ref.shape[2] // LANES):
        o_ref[0, :, s * LANES:(s + 1) * LANES] = tile[pl.ds(s, cp, stride=SUBLANES), :].astype(bf16)


def _gather_tokens(idx, hn_tiles, aff, d):
    b, n_e, cp = idx.shape
    rows = hn_tiles.shape[1]
    l = aff.shape[1]
    return pl.pallas_call(
        _gather_kernel,
        grid=(b, n_e),
        in_specs=[
            pl.BlockSpec((1, 1, cp), lambda bi, e: (bi * n_e + e, 0, 0), memory_space=pltpu.SMEM),
            pl.BlockSpec((1, rows, LANES), lambda bi, e: (bi, 0, 0)),
            pl.BlockSpec((1, l, LANES), lambda bi, e: (bi, 0, 0)),
        ],
        out_specs=[
            pl.BlockSpec((1, cp, d), lambda bi, e: (e, bi, 0)),
            pl.BlockSpec((1, cp, LANES), lambda bi, e: (e, bi, 0)),
        ],
        out_shape=[
            jax.ShapeDtypeStruct((n_e, b * cp, d), bf16),
            jax.ShapeDtypeStruct((n_e, b * cp, LANES), f32),
        ],
        scratch_shapes=[pltpu.VMEM((cp * SUBLANES, LANES), f32)],
        compiler_params=pltpu.CompilerParams(
            dimension_semantics=("arbitrary", "arbitrary"), vmem_limit_bytes=48 * MIB),
        name="gather_tokens",
    )(idx.reshape(b * n_e, 1, cp), hn_tiles, aff)


def _moe_kernel(x_ref, gate_ref, wg_ref, wu_ref, wd_ref, y_ref, wg_b, wu_b, wd_b, acc, *, cp):
    f = pl.program_id(1)
    n_f = pl.num_programs(1)
    wg_b[...] = wg_ref[0].astype(bf16)
    wu_b[...] = wu_ref[0].astype(bf16)
    wd_b[...] = wd_ref[0].astype(bf16)

    @pl.when(f == 0)
    def _():
        acc[...] = jnp.zeros(acc.shape, f32)

    n_chunks = acc.shape[0] // cp
    for c in range(n_chunks):
        rows = slice(c * cp, (c + 1) * cp)
        x = x_ref[0, rows, :]
        g = jnp.dot(x, wg_b[...], preferred_element_type=f32)
        u = jnp.dot(x, wu_b[...], preferred_element_type=f32)
        hid = (g * jax.nn.sigmoid(g) * u).astype(bf16)
        acc[rows, :] += jnp.dot(hid, wd_b[...], preferred_element_type=f32)

    @pl.when(f == n_f - 1)
    def _():
        for c in range(n_chunks):
            rows = slice(c * cp, (c + 1) * cp)
            lane = lax.broadcasted_iota(jnp.int32, (cp, LANES), 1)
            gate = jnp.sum(jnp.where(lane == pl.program_id(0), gate_ref[0, rows, :], 0.0),
                           axis=1, keepdims=True)
            y = acc[rows, :] * gate
            for s in range(y.shape[1] // LANES):
                y_ref[0, pl.ds(c * cp * SUBLANES + s, cp, stride=SUBLANES), :] = (
                    y[:, s * LANES:(s + 1) * LANES])


def _expert_ffn(xg, gate, w_gate, w_up, w_down, cp):
    n_e, m, d = xg.shape
    d_ff = w_gate.shape[2]
    kernel = functools.partial(_moe_kernel, cp=cp)
    return pl.pallas_call(
        kernel,
        grid=(n_e, d_ff // FF_TILE),
        in_specs=[
            pl.BlockSpec((1, m, d), lambda e, f: (e, 0, 0)),
            pl.BlockSpec((1, m, LANES), lambda e, f: (e, 0, 0)),
            pl.BlockSpec((1, d, FF_TILE), lambda e, f: (e, 0, f)),
            pl.BlockSpec((1, d, FF_TILE), lambda e, f: (e, 0, f)),
            pl.BlockSpec((1, FF_TILE, d), lambda e, f: (e, f, 0)),
        ],
        out_specs=pl.BlockSpec((1, m * SUBLANES, LANES), lambda e, f: (e, 0, 0)),
        out_shape=jax.ShapeDtypeStruct((n_e, m * SUBLANES, LANES), f32),
        scratch_shapes=[
            pltpu.VMEM((d, FF_TILE), bf16),
            pltpu.VMEM((d, FF_TILE), bf16),
            pltpu.VMEM((FF_TILE, d), bf16),
            pltpu.VMEM((m, d), f32),
        ],
        compiler_params=pltpu.CompilerParams(
            dimension_semantics=("arbitrary", "arbitrary"), vmem_limit_bytes=56 * MIB),
        name="expert_ffn",
    )(xg, gate, w_gate, w_up, w_down)


def _combine_kernel(idx_ref, y_ref, h2_ref, g_ref, o_ref, acc, *, n_experts):
    step = pl.program_id(1)
    cp = idx_ref.shape[2]
    tc = o_ref.shape[1]

    @pl.when(step == 0)
    def _():
        acc[...] = jnp.zeros(acc.shape, f32)

    @pl.when(step < n_experts)
    def _():
        def group(g, carry):
            dst = []
            val = []
            for j in range(SUBLANES):
                slot = g * SUBLANES + j
                r = pl.multiple_of(idx_ref[0, 0, slot] * SUBLANES, SUBLANES)
                dst.append(r)
                val.append(acc[pl.ds(r, SUBLANES), :]
                           + y_ref[0, pl.ds(pl.multiple_of(slot * SUBLANES, SUBLANES), SUBLANES), :])
            for r, v in zip(dst, val):
                acc[pl.ds(r, SUBLANES), :] = v
            return carry

        lax.fori_loop(0, cp // SUBLANES, group, 0)

    @pl.when(step >= n_experts)
    def _():
        r0 = pl.multiple_of((step - n_experts) * tc * SUBLANES, SUBLANES)
        moe = jnp.concatenate(
            [acc[pl.ds(r0 + s, tc, stride=SUBLANES), :] for s in range(o_ref.shape[2] // LANES)], axis=-1)
        o_ref[0] = _rms(h2_ref[0] + moe, g_ref[...])


def _combine(idx, y, h2, g, n_body):
    b, n_e, cp = idx.shape
    _, l, d = h2.shape
    n_out = n_body // OUT_CHUNK
    kernel = functools.partial(_combine_kernel, n_experts=n_e)
    expert = lambda s: jnp.minimum(s, n_e - 1)
    chunk = lambda s: jnp.maximum(s - n_e, 0)
    return pl.pallas_call(
        kernel,
        grid=(b, n_e + n_out),
        in_specs=[
            pl.BlockSpec((1, 1, cp), lambda bi, s: (bi * n_e + expert(s), 0, 0), memory_space=pltpu.SMEM),
            pl.BlockSpec((1, cp * SUBLANES, LANES), lambda bi, s: (expert(s), bi, 0)),
            pl.BlockSpec((1, OUT_CHUNK, d), lambda bi, s: (bi, chunk(s), 0)),
            pl.BlockSpec((1, d), lambda bi, s: (0, 0)),
        ],
        out_specs=pl.BlockSpec((1, OUT_CHUNK, d), lambda bi, s: (bi, chunk(s), 0)),
        out_shape=jax.ShapeDtypeStruct((b, n_body, d), f32),
        scratch_shapes=[pltpu.VMEM(((l + 1) * SUBLANES, LANES), f32)],
        compiler_params=pltpu.CompilerParams(
            dimension_semantics=("arbitrary", "arbitrary"), vmem_limit_bytes=48 * MIB),
        name="combine",
    )(idx.reshape(b * n_e, 1, cp), y, h2, g)


def _round_up(n, k):
    return -(-n // k) * k


def kernel(x, meta_tokens, g_mix, w_in, lam_vecs, subln_gain, pool_w, pool_scale, w_o, g_ffn, router_w,
           w_gate, w_up, w_down, g_final):
    b, n_body, d = x.shape
    l = n_body + N_META
    qk = HEADS * D_QK
    attn_width = HEADS * D_V
    depth = g_mix.shape[0]
    cap = CAPACITY_FACTOR * l // N_EXPERTS
    cp = _round_up(cap, BF16_ROWS)

    meta = jnp.broadcast_to(meta_tokens[None].astype(x.dtype), (b, N_META, d))
    h = jnp.concatenate([x, meta], axis=1)
    slopes = jnp.broadcast_to(
        jnp.array([_alibi_slope(i) * LOG2E for i in range(HEADS)], f32)[:, None, None], (HEADS, 1, LANES))
    cols = jnp.arange(HEADS)[:, None] * D_QK + jnp.arange(D_QK)[None, :]
    q_cols = jnp.concatenate([cols, qk + cols], axis=1).reshape(-1)
    k_cols = q_cols + 2 * qk
    v_cols = 4 * qk + jnp.arange(attn_width)
    u_cols = jnp.arange(4 * qk + attn_width, w_in.shape[2])

    out = None
    for layer in range(depth):
        lam_init = 0.8 - 0.6 * math.exp(-0.3 * layer)
        w = w_in[layer]
        w_qv_t = jnp.concatenate([w[:, q_cols] * (LOG2E / math.sqrt(D_QK)), w[:, v_cols]], axis=1).T.astype(bf16)
        w_ku = jnp.concatenate([w[:, k_cols], w[:, u_cols]], axis=1).astype(bf16)
        qvt, k, u = _norm_inproj(h, g_mix[layer][None, :], w_qv_t, w_ku, 2 * qk)
        a = _diff_attention(qvt, k, lam_vecs[layer].astype(f32), subln_gain[layer][None, :],
                            slopes, n_body, lam_init)
        p = _multiscale_pool(u, pool_w[layer].astype(bf16), pool_scale[layer][None, :], n_body)
        wo = w_o[layer].astype(bf16)
        h2, hn_tiles, aff = _outproj_router(
            h.reshape(b * l, d), a.reshape(b * l, -1), p.reshape(b * l, -1), wo[:attn_width], wo[attn_width:],
            g_ffn[layer][None, :],
            jnp.pad(router_w[layer], ((0, 0), (0, LANES - N_EXPERTS))).astype(bf16))

        aff = aff.reshape(b, l, LANES)
        lp = _round_up(l, LANES)
        aff_pos = jnp.concatenate(
            [aff[:, n_body:, :N_EXPERTS], aff[:, :n_body, :N_EXPERTS], jnp.full((b, lp - l, N_EXPERTS), -1.0, f32)],
            axis=1)
        idx = _topk_rows(jnp.swapaxes(aff_pos, 1, 2), cap, cp, n_body)
        idx = jnp.swapaxes(idx[:, :, :N_EXPERTS], 1, 2)
        real = jnp.arange(cp)[None, None, :] < cap
        idx_gather = jnp.where(real, idx, 0)
        idx_scatter = jnp.where(real, idx, l)

        xg, gate = _gather_tokens(idx_gather, hn_tiles.reshape(b, l * SUBLANES, LANES), aff, d)
        y = _expert_ffn(xg, gate, w_gate[layer], w_up[layer], w_down[layer], cp)
        last = layer == depth - 1
        g_out = g_final[None, :] if last else jnp.ones((1, d), f32)
        out = _combine(idx_scatter, y, h2.reshape(b, l, d), g_out, n_body)
    return out
```

```python
import functools
import math

import jax
import jax.numpy as jnp
from jax import lax
from jax.experimental import pallas as pl
from jax.experimental.pallas import tpu as pltpu

N_META = 16
HEADS = 4
D_QK = 64
D_V = 128
POOL_WINDOWS = (2, 4, 8, 16)
POOL_HALO = max(POOL_WINDOWS) // 2
N_EXPERTS = 16
CAPACITY_FACTOR = 2
EPS = 1e-6
LOG2E = 1.4426950408889634


def _alibi_slope(head):
    return 2.0 ** (-8.0 * (head + 1) / HEADS)


LANES = 128
SUBLANES = 8
BF16_ROWS = 16

ROW_TILE = 512
ATTN_TQ = 256
ATTN_TK = 512
POOL_CHUNK = 256
FF_TILE = 256
OUT_CHUNK = 512
MIB = 1024 * 1024

f32 = jnp.float32
bf16 = jnp.bfloat16


def _rms(x, g):
    return x * lax.rsqrt(jnp.mean(x * x, axis=-1, keepdims=True) + EPS) * g


def _norm_inproj_kernel(x_ref, g_ref, wt_ref, w_ref, qvt_ref, k_ref, u_ref, *, seq_len):
    x = x_ref[0]
    valid = seq_len - pl.program_id(1) * x.shape[0]
    x = jnp.where(lax.broadcasted_iota(jnp.int32, x.shape, 0) < valid, x, 0.0)
    n = _rms(x, g_ref[...]).astype(bf16)
    qvt = lax.dot_general(wt_ref[...], n, (((1,), (1,)), ((), ())), preferred_element_type=f32)
    qvt_ref[0, 0] = qvt.astype(bf16)
    ku = jnp.dot(n, w_ref[...], preferred_element_type=f32)
    n_k = k_ref.shape[-1]
    k_ref[0] = ku[:, :n_k].astype(bf16)
    u_ref[0] = ku[:, n_k:]


def _norm_inproj(h, g, w_qv_t, w_ku, n_k):
    b, l, d = h.shape
    n_t = pl.cdiv(l, ATTN_TK)
    n_qv = w_qv_t.shape[0]
    n_u = w_ku.shape[1] - n_k
    return pl.pallas_call(
        functools.partial(_norm_inproj_kernel, seq_len=l),
        grid=(b, n_t),
        in_specs=[
            pl.BlockSpec((1, ATTN_TK, d), lambda bi, i: (bi, i, 0)),
            pl.BlockSpec((1, d), lambda bi, i: (0, 0)),
            pl.BlockSpec(w_qv_t.shape, lambda bi, i: (0, 0)),
            pl.BlockSpec(w_ku.shape, lambda bi, i: (0, 0)),
        ],
        out_specs=[
            pl.BlockSpec((1, 1, n_qv, ATTN_TK), lambda bi, i: (bi, i, 0, 0)),
            pl.BlockSpec((1, ATTN_TK, n_k), lambda bi, i: (bi, i, 0)),
            pl.BlockSpec((1, ATTN_TK, n_u), lambda bi, i: (bi, i, 0)),
        ],
        out_shape=[
            jax.ShapeDtypeStruct((b, n_t, n_qv, ATTN_TK), bf16),
            jax.ShapeDtypeStruct((b, l, n_k), bf16),
            jax.ShapeDtypeStruct((b, l, n_u), f32),
        ],
        compiler_params=pltpu.CompilerParams(
            dimension_semantics=("arbitrary", "arbitrary"), vmem_limit_bytes=40 * MIB),
        name="norm_inproj",
    )(h, g, w_qv_t, w_ku)


def _attn_kernel(lam_ref, gain_ref, slope_ref, q_ref, qe_ref, k_ref, ke_ref, vt_ref, o_ref,
                 qs_scr, bias_scr, tail_scr, vt_scr, *scr, n_body, lam_init):
    tq = ATTN_TQ
    tk = ATTN_TK
    n_diag = tk // tq
    n_sub = q_ref.shape[3] // tq
    i = pl.program_id(2)
    n_q = pl.num_programs(2)
    is_meta = i == n_q - 1
    slope = slope_ref[0][:, :1]

    @pl.when(i == 0)
    def _():
        d = (lax.broadcasted_iota(jnp.int32, (tk, tq), 0)
             - lax.broadcasted_iota(jnp.int32, (tk, tq), 1)).astype(f32)
        for v in range(n_diag):
            bias_scr[v] = -slope * jnp.abs(d - float(v * tq))
        tail_scr[0] = slope * d[0:N_META]
        tail_scr[1] = -slope * jnp.abs(d[0:N_META])
        ones_row = jnp.where(lax.broadcasted_iota(jnp.int32, (BF16_ROWS, tk), 0) == 0, 1.0, 0.0).astype(bf16)
        for c in range(vt_ref.shape[1]):
            vt_scr[c, 0:D_V, :] = vt_ref[0, c]
            vt_scr[c, D_V:, :] = ones_row
        width = qs_scr.shape[2]
        qs_scr[:, 2 * D_QK + BF16_ROWS:, :] = jnp.zeros((3, 2 * LANES - 2 * D_QK - BF16_ROWS, width), bf16)
        qs_scr[2, 2 * D_QK:2 * D_QK + BF16_ROWS, :] = jnp.zeros((BF16_ROWS, width), bf16)

    zero = jnp.zeros((D_QK, tq), bf16)
    for sub in range(n_sub):
        q = q_ref[0, 0, :, sub * tq:(sub + 1) * tq]
        cols = slice(2 * sub * tq, 2 * (sub + 1) * tq)
        top = jnp.concatenate([q[0:D_QK], zero], axis=1)
        bottom = jnp.concatenate([zero, q[D_QK:]], axis=1)
        for v in range(3):
            qs_scr[v, 0:D_QK, cols] = top
            qs_scr[v, D_QK:2 * D_QK, cols] = bottom
        for v in range(2):
            ext = qe_ref[0, v, :, sub * tq:(sub + 1) * tq]
            qs_scr[v, 2 * D_QK:2 * D_QK + BF16_ROWS, cols] = jnp.concatenate([ext, ext], axis=1)

    n_streams = 2 * n_sub
    per_stream = len(scr) // n_streams
    streams = [scr[st * per_stream:(st + 1) * per_stream] for st in range(n_streams)]
    for _, _, _, _, _, _, m_scr, acc_scr in streams:
        m_scr[...] = jnp.full(m_scr.shape, -jnp.inf, f32)
        acc_scr[...] = jnp.zeros(acc_scr.shape, f32)

    q_pos0 = jnp.where(is_meta, 0, N_META + i * n_sub * tq)
    q_valid = jnp.where(is_meta, N_META, n_sub * tq)
    n_slab = tq // LANES
    piece = BF16_ROWS

    def issue_scores(k, version, slot):
        n = k.shape[0]
        for st, stream in enumerate(streams):
            stream[slot][0:n, :] = jnp.dot(k, qs_scr[version, 0:k.shape[1], st * tq:(st + 1) * tq],
                                           preferred_element_type=f32)

    def softmax(slot, n, bias_tile=None, consts=None):
        def scores(s_scr, r, slab):
            t = s_scr[r:r + piece, slab * LANES:(slab + 1) * LANES]
            return t if bias_tile is None else t + bias_tile(r, slab)

        stats = []
        for st, stream in enumerate(streams):
            s_scr, m_scr = stream[slot], stream[6]
            const = None if consts is None else consts[st // 2]
            m_old = m_scr[...]
            maxes = []
            for slab in range(n_slab):
                mx = scores(s_scr, 0, slab)
                for r in range(piece, n, piece):
                    mx = jnp.maximum(mx, scores(s_scr, r, slab))
                maxes.append(jnp.max(mx, axis=0, keepdims=True))
            m_loc = jnp.concatenate(maxes, axis=1)
            m_new = jnp.maximum(m_old, m_loc if const is None else m_loc + const)
            stats.append((m_new, m_new if const is None else m_new - const, jnp.exp2(m_old - m_new)))

        for stream, (m_new, shift, alpha) in zip(streams, stats):
            s_scr, p_scr, a_scr, m_scr = stream[slot], stream[2 + slot], stream[4 + slot], stream[6]
            for slab in range(n_slab):
                cols = slice(slab * LANES, (slab + 1) * LANES)
                sub_b = jnp.broadcast_to(shift[:, cols], (piece, LANES))
                for r in range(0, n, piece):
                    p_scr[r:r + piece, cols] = jnp.exp2(scores(s_scr, r, slab) - sub_b).astype(bf16)
            a_scr[...] = alpha
            m_scr[...] = m_new

    def apply_values(vt, slot):
        n = vt.shape[1]
        for stream in streams:
            p_scr, a_scr, acc_scr = stream[2 + slot], stream[4 + slot], stream[7]
            acc_scr[...] = a_scr[...] * acc_scr[...] + jnp.dot(vt, p_scr[0:n, :], preferred_element_type=f32)

    def keys_with_positions(c):
        rows = pl.ds(pl.multiple_of(c * tk, tk), tk)
        return jnp.concatenate([k_ref[0, rows, :], ke_ref[0, rows, :]], axis=1)

    n_chunks = n_body // tk

    def issue_chunk_scores(c, slot):
        k_pos0 = N_META + c * tk
        version = jnp.where(k_pos0 + tk <= q_pos0, 0, jnp.where(k_pos0 >= q_pos0 + q_valid, 1, 2))
        issue_scores(keys_with_positions(c), version, slot)

    def add_straddle_bias(c, slot):
        k_pos0 = N_META + c * tk

        @pl.when(jnp.logical_and(k_pos0 + tk > q_pos0, k_pos0 < q_pos0 + q_valid))
        def _():
            for st, stream in enumerate(streams):
                variant = (q_pos0 - k_pos0) // tq + st // 2
                for slab in range(n_slab):
                    cols = slice(slab * LANES, (slab + 1) * LANES)
                    for r in range(0, tk, piece):
                        stream[slot][r:r + piece, cols] = (stream[slot][r:r + piece, cols]
                                                           + bias_scr[variant, r:r + piece, cols])

    def step(t, slot, issue_next, apply_prev):
        add_straddle_bias(t, slot)
        if issue_next:
            issue_chunk_scores(t + 1, 1 - slot)
        softmax(slot, tk)
        if apply_prev:
            apply_values(vt_scr[t - 1], 1 - slot)

    def run():
        def pair(u, carry):
            step(2 * u, 0, True, True)
            step(2 * u + 1, 1, True, True)
            return carry

        issue_chunk_scores(0, 0)
        step(0, 0, True, False)
        step(1, 1, True, True)
        lax.fori_loop(1, n_chunks // 2 - 1, pair, 0)
        step(n_chunks - 2, 0, True, True)
        step(n_chunks - 1, 1, False, True)
        apply_values(vt_scr[n_chunks - 1], 1)

        tail = jnp.where(is_meta, 1, 0)
        consts = [-slope * jnp.full((1, 1), q_pos0 + sub * tq, jnp.int32).astype(f32) for sub in range(n_sub)]
        issue_scores(k_ref[0, n_body:n_body + N_META, :], 2, 0)
        softmax(0, N_META, lambda r, slab: tail_scr[tail, r:r + piece, slab * LANES:(slab + 1) * LANES], consts)
        apply_values(vt_scr[n_chunks, :, 0:N_META], 0)

        lv = lam_ref[...]
        lam = (jnp.exp(jnp.sum(lv[0:1] * lv[1:2], axis=-1, keepdims=True))
               - jnp.exp(jnp.sum(lv[2:3] * lv[3:4], axis=-1, keepdims=True)) + lam_init)
        for sub in range(n_sub):
            o1, o2 = [stream[7][0:D_V, :] / stream[7][D_V:D_V + 1, :] for stream in streams[2 * sub:2 * sub + 2]]
            att = (o1 - lam * o2).T
            o_ref[0, sub * tq:(sub + 1) * tq, :] = (_rms(att, gain_ref[...]) * (1.0 - lam_init)).astype(bf16)

    pl.when(jnp.logical_not(is_meta))(lambda: run())
    streams, n_sub = streams[:2], 1
    pl.when(is_meta)(lambda: run())


def _alibi_tables(n_body, n_q_tiles):
    l = n_body + N_META
    pos = jnp.concatenate([jnp.arange(n_body) + N_META, jnp.arange(N_META)]).astype(f32)
    q_pos = jnp.pad(pos, (0, n_q_tiles * ATTN_TQ - l))
    k_hi, k_lo = jnp.floor(pos / 64), jnp.mod(pos, 64)
    q_hi, q_lo = jnp.floor(q_pos / 64), jnp.mod(q_pos, 64)
    ke, qe = [], []
    for h in range(HEADS):
        c = jnp.float32(_alibi_slope(h) * LOG2E)
        c1 = c.astype(bf16).astype(f32)
        c2 = (c - c1).astype(bf16).astype(f32)
        c3 = (c - c1 - c2).astype(bf16).astype(f32)
        pieces = [c1, c2, c3]
        cols = [k_hi] * 3 + [k_lo] * 3 + [jnp.full_like(pos, p) for p in pieces] * 2
        ke.append(jnp.pad(jnp.stack(cols, axis=1), ((0, 0), (0, LANES - len(cols)))))
        rows = ([jnp.full_like(q_pos, 64 * p) for p in pieces] + [jnp.full_like(q_pos, p) for p in pieces]
                + [-64 * q_hi] * 3 + [-q_lo] * 3)
        ext = jnp.pad(jnp.stack(rows), ((0, BF16_ROWS - len(rows)), (0, 0)))
        qe.append(jnp.stack([ext, -ext]))
    return jnp.stack(ke).astype(bf16), jnp.stack(qe).astype(bf16)


def _diff_attention(qvt, k, lam_vecs, gain, slopes, n_body, lam_init):
    b, l, _ = k.shape
    n_t = qvt.shape[1]
    per_chunk = ATTN_TK // ATTN_TQ
    n_q = pl.cdiv(l, ATTN_TK)
    assert ATTN_TK % ATTN_TQ == 0 and n_body % (2 * ATTN_TK) == 0 and n_body >= 4 * ATTN_TK
    ke, qe = _alibi_tables(n_body, n_q * per_chunk)
    kernel = functools.partial(_attn_kernel, n_body=n_body, lam_init=lam_init)
    return pl.pallas_call(
        kernel,
        grid=(b, HEADS, n_q),
        in_specs=[
            pl.BlockSpec(lam_vecs.shape, lambda bi, h, i: (0, 0)),
            pl.BlockSpec((1, D_V), lambda bi, h, i: (0, 0)),
            pl.BlockSpec((1, 1, LANES), lambda bi, h, i: (h, 0, 0)),
            pl.BlockSpec((1, 1, 2 * D_QK, ATTN_TK), lambda bi, h, i: (bi, i, h, 0)),
            pl.BlockSpec((1, 2, BF16_ROWS, ATTN_TK), lambda bi, h, i: (h, 0, 0, i)),
            pl.BlockSpec((1, l, 2 * D_QK), lambda bi, h, i: (bi, 0, h)),
            pl.BlockSpec((1, l, LANES), lambda bi, h, i: (h, 0, 0)),
            pl.BlockSpec((1, n_t, D_V, ATTN_TK), lambda bi, h, i: (bi, 0, HEADS + h, 0)),
        ],
        out_specs=pl.BlockSpec((1, ATTN_TK, D_V), lambda bi, h, i: (bi, i, h)),
        out_shape=jax.ShapeDtypeStruct((b, l, HEADS * D_V), bf16),
        scratch_shapes=[
            pltpu.VMEM((3, 2 * LANES, 2 * ATTN_TK), bf16),
            pltpu.VMEM((per_chunk, ATTN_TK, ATTN_TQ), f32),
            pltpu.VMEM((2, N_META, ATTN_TQ), f32),
            pltpu.VMEM((n_t, D_V + BF16_ROWS, ATTN_TK), bf16),
        ] + 2 * per_chunk * [
            pltpu.VMEM((ATTN_TK, ATTN_TQ), f32), pltpu.VMEM((ATTN_TK, ATTN_TQ), f32),
            pltpu.VMEM((ATTN_TK, ATTN_TQ), bf16), pltpu.VMEM((ATTN_TK, ATTN_TQ), bf16),
            pltpu.VMEM((1, ATTN_TQ), f32), pltpu.VMEM((1, ATTN_TQ), f32),
            pltpu.VMEM((1, ATTN_TQ), f32),
            pltpu.VMEM((D_V + BF16_ROWS, ATTN_TQ), f32),
        ],
        compiler_params=pltpu.CompilerParams(
            dimension_semantics=("arbitrary", "arbitrary", "arbitrary"), vmem_limit_bytes=40 * MIB),
        name="diff_attention",
    )(lam_vecs, gain, slopes, qvt, qe, k, ke, qvt)


def _pool_kernel(u_ref, pw_ref, ps_ref, o_ref, upad, *, n_body):
    l = n_body + N_META
    width = u_ref.shape[-1]
    upad[0:POOL_HALO, :] = jnp.zeros((POOL_HALO, width), f32)
    upad[POOL_HALO:POOL_HALO + N_META, :] = u_ref[0, n_body:l, :]
    upad[POOL_HALO + N_META:POOL_HALO + l, :] = u_ref[0, 0:n_body, :]
    upad[POOL_HALO + l:2 * POOL_HALO + l, :] = jnp.zeros((POOL_HALO, width), f32)

    def rows(pos0, n, out_row0):
        slab = upad[pl.ds(pos0, n + 2 * POOL_HALO), :]
        t = pos0 + lax.broadcasted_iota(jnp.int32, (n, 1), 0)
        for gi, w in enumerate(POOL_WINDOWS):
            cols = slice(gi * LANES, (gi + 1) * LANES)
            total = slab[POOL_HALO - w // 2:POOL_HALO - w // 2 + n, cols]
            for j in range(1, w):
                total = total + slab[POOL_HALO - w // 2 + j:POOL_HALO - w // 2 + j + n, cols]
            cnt = (jnp.minimum(t + w // 2, l) - jnp.maximum(t - w // 2, 0)).astype(f32)
            pooled = total / cnt - slab[POOL_HALO:POOL_HALO + n, cols]
            y = jnp.dot(pooled.astype(bf16), pw_ref[gi], preferred_element_type=f32) * ps_ref[:, cols]
            o_ref[0, pl.ds(out_row0, n), cols] = y.astype(bf16)

    rows(0, N_META, n_body)

    def body(c, carry):
        r0 = pl.multiple_of(c * POOL_CHUNK, POOL_CHUNK)
        rows(N_META + r0, POOL_CHUNK, r0)
        return carry

    lax.fori_loop(0, n_body // POOL_CHUNK, body, 0)


def _multiscale_pool(u, pool_w, pool_scale, n_body):
    b, l, width = u.shape
    kernel = functools.partial(_pool_kernel, n_body=n_body)
    return pl.pallas_call(
        kernel,
        grid=(b,),
        in_specs=[
            pl.BlockSpec((1, l, width), lambda bi: (bi, 0, 0)),
            pl.BlockSpec(pool_w.shape, lambda bi: (0, 0, 0)),
            pl.BlockSpec((1, width), lambda bi: (0, 0)),
        ],
        out_specs=pl.BlockSpec((1, l, width), lambda bi: (bi, 0, 0)),
        out_shape=jax.ShapeDtypeStruct((b, l, width), bf16),
        scratch_shapes=[pltpu.VMEM((l + 2 * POOL_HALO, width), f32)],
        compiler_params=pltpu.CompilerParams(
            dimension_semantics=("arbitrary",), vmem_limit_bytes=48 * MIB),
        name="multiscale_pool",
    )(u, pool_w, pool_scale)


def _outproj_router_kernel(h_ref, a_ref, p_ref, woa_ref, wop_ref, g_ref, rw_ref, h2_ref, hn_ref, aff_ref):
    tm = h_ref.shape[0]
    h2 = (h_ref[...]
          + jnp.dot(a_ref[...], woa_ref[...], preferred_element_type=f32)
          + jnp.dot(p_ref[...], wop_ref[...], preferred_element_type=f32))
    h2_ref[...] = h2
    hn = _rms(h2, g_ref[...])
    logits = jnp.dot(hn.astype(bf16), rw_ref[...], preferred_element_type=f32)
    lane = lax.broadcasted_iota(jnp.int32, logits.shape, 1)
    logits = jnp.where(lane < N_EXPERTS, logits, -jnp.inf)
    z = jnp.exp(logits - jnp.max(logits, axis=-1, keepdims=True))
    aff_ref[...] = z / jnp.sum(z, axis=-1, keepdims=True)
    for s in range(hn.shape[1] // LANES):
        hn_ref[pl.ds(s, tm, stride=SUBLANES), :] = hn[:, s * LANES:(s + 1) * LANES]


def _outproj_router(h, a, p, wo_a, wo_p, g, rw):
    m, d = h.shape
    half = a.shape[1]
    n_e = rw.shape[1]
    row = lambda i: (i, 0)
    fixed = lambda i: (0, 0)
    return pl.pallas_call(
        _outproj_router_kernel,
        grid=(pl.cdiv(m, ROW_TILE),),
        in_specs=[
            pl.BlockSpec((ROW_TILE, d), row),
            pl.BlockSpec((ROW_TILE, half), row),
            pl.BlockSpec((ROW_TILE, half), row),
            pl.BlockSpec((half, d), fixed),
            pl.BlockSpec((half, d), fixed),
            pl.BlockSpec((1, d), fixed),
            pl.BlockSpec((d, n_e), fixed),
        ],
        out_specs=[
            pl.BlockSpec((ROW_TILE, d), row),
            pl.BlockSpec((ROW_TILE * SUBLANES, LANES), row),
            pl.BlockSpec((ROW_TILE, n_e), row),
        ],
        out_shape=[
            jax.ShapeDtypeStruct((m, d), f32),
            jax.ShapeDtypeStruct((m * SUBLANES, LANES), f32),
            jax.ShapeDtypeStruct((m, n_e), f32),
        ],
        compiler_params=pltpu.CompilerParams(
            dimension_semantics=("arbitrary",), vmem_limit_bytes=40 * MIB),
        name="outproj_router",
    )(h, a, p, wo_a, wo_p, g, rw)


def _prefix_count(flags, w_scan):
    out = []
    offset = jnp.zeros((flags.shape[0], LANES), f32)
    for blk in range(flags.shape[1] // LANES):
        r = jnp.dot(flags[:, blk * LANES:(blk + 1) * LANES].astype(bf16), w_scan, preferred_element_type=f32)
        out.append(r[:, :LANES] + offset)
        offset = offset + r[:, LANES:]
    return jnp.concatenate(out, axis=1)


def _topk_kernel(aff_ref, idx_ref, rank_scr, chosen_scr, *, cap, n_body):
    n_e, lp = aff_ref.shape[1], aff_ref.shape[2]
    cp = idx_ref.shape[1]
    n_blk = lp // LANES
    bits = lax.bitcast_convert_type(aff_ref[0], jnp.int32)

    def refine(step, thr):
        cand = thr | (1 << (30 - step))
        cnt = jnp.sum(jnp.where(bits >= cand, 1, 0), axis=1, keepdims=True)
        return jnp.where(cnt >= cap, cand, thr)

    thr = lax.fori_loop(0, 31, refine, jnp.zeros((n_e, 1), jnp.int32))
    aff = aff_ref[0]
    pivot = jnp.max(jnp.where(bits == thr, aff, -1.0), axis=1, keepdims=True)
    above = aff > pivot
    tie = aff == pivot
    need = (cap - jnp.sum(jnp.where(above, 1, 0), axis=1, keepdims=True)).astype(f32)

    row = lax.broadcasted_iota(jnp.int32, (LANES, 2 * LANES), 0)
    col = lax.broadcasted_iota(jnp.int32, (LANES, 2 * LANES), 1)
    w_scan = jnp.where((row <= col) | (col >= LANES), 1.0, 0.0).astype(bf16)

    tie_rank = _prefix_count(jnp.where(tie, 1.0, 0.0), w_scan)
    chosen_scr[...] = jnp.where(above | (tie & (tie_rank <= need)), 1.0, 0.0)
    pos_f = lax.broadcasted_iota(jnp.int32, (n_e, lp), 1).astype(f32)
    big = float(2 * lp)

    def excess(cnt):
        return jnp.sum(jnp.abs(cnt - float(cap))) > 0.0

    def repair(cnt):
        a = aff_ref[0]
        sel = chosen_scr[...] > 0.5
        lo = jnp.min(jnp.where(sel, a, jnp.inf), axis=1, keepdims=True)
        lo_at = jnp.max(jnp.where(sel & (a == lo), pos_f, -1.0), axis=1, keepdims=True)
        drop = (pos_f == lo_at) & (cnt > float(cap))
        free = (~sel) & (a >= 0.0)
        hi = jnp.max(jnp.where(free, a, -1.0), axis=1, keepdims=True)
        hi_at = jnp.min(jnp.where(free & (a == hi), pos_f, big), axis=1, keepdims=True)
        add = (pos_f == hi_at) & (cnt < float(cap))
        new = jnp.where(add, 1.0, jnp.where(drop, 0.0, chosen_scr[...]))
        chosen_scr[...] = new
        return jnp.sum(new, axis=1, keepdims=True)

    lax.while_loop(excess, repair, jnp.sum(chosen_scr[...], axis=1, keepdims=True))
    rank = _prefix_count(chosen_scr[...], w_scan)
    for e in range(n_e):
        rank_scr[e] = rank[e:e + 1, :]

    lane = lax.broadcasted_iota(jnp.int32, (SUBLANES, LANES), 1)
    sub = lax.broadcasted_iota(jnp.int32, (SUBLANES, LANES), 0)

    n_groups = cp // SUBLANES
    unroll = next(u for u in (11, 8, 6, 5, 4, 3, 2, 1) if n_groups % u == 0)

    def per_expert(e, carry):
        def per_groups(gi, carry2):
            slots = [((gi * unroll + u) * SUBLANES + sub).astype(f32) for u in range(unroll)]
            accs = [jnp.zeros((SUBLANES, LANES), f32) for _ in range(unroll)]
            for t in range(n_blk):
                r_blk = jnp.broadcast_to(rank_scr[e, :, t * LANES:(t + 1) * LANES], (SUBLANES, LANES))
                accs = [a + jnp.where(r_blk <= c, 1.0, 0.0) for a, c in zip(accs, slots)]
            for u in range(unroll):
                pos = jnp.sum(accs[u], axis=1, keepdims=True).astype(jnp.int32)
                rows = jnp.where(pos < N_META, pos + n_body, pos - N_META)
                at = pl.ds(pl.multiple_of((gi * unroll + u) * SUBLANES, SUBLANES), SUBLANES)
                idx_ref[0, at, :] = jnp.where(lane == e, rows, idx_ref[0, at, :])
            return carry2

        return lax.fori_loop(0, n_groups // unroll, per_groups, carry)

    idx_ref[0] = jnp.zeros((cp, LANES), jnp.int32)
    lax.fori_loop(0, n_e, per_expert, 0)


def _topk_rows(aff_pos, cap, cp, n_body):
    b, n_e, lp = aff_pos.shape
    kernel = functools.partial(_topk_kernel, cap=cap, n_body=n_body)
    return pl.pallas_call(
        kernel,
        grid=(b,),
        in_specs=[pl.BlockSpec((1, n_e, lp), lambda bi: (bi, 0, 0))],
        out_specs=pl.BlockSpec((1, cp, LANES), lambda bi: (bi, 0, 0)),
        out_shape=jax.ShapeDtypeStruct((b, cp, LANES), jnp.int32),
        scratch_shapes=[pltpu.VMEM((n_e, 1, lp), f32), pltpu.VMEM((n_e, lp), f32)],
        compiler_params=pltpu.CompilerParams(
            dimension_semantics=("arbitrary",), vmem_limit_bytes=32 * MIB),
        name="topk_rows",
    )(aff_pos)


def _gather_kernel(code_ref, src_ref, aff_ref, o_ref, gate_ref, tile):
    cp = o_ref.shape[1]
    sub = lax.broadcasted_iota(jnp.int32, (SUBLANES, LANES), 0)

    def group(g, carry):
        gates = jnp.zeros((SUBLANES, LANES), f32)
        for j in range(SUBLANES):
            slot = g * SUBLANES + j
            code = code_ref[0, 0, slot]
            tile[pl.ds(pl.multiple_of(slot * SUBLANES, SUBLANES), SUBLANES), :] = (
                src_ref[0, pl.ds(pl.multiple_of(code & -SUBLANES, SUBLANES), SUBLANES), :])
            aff8 = aff_ref[0, pl.ds(pl.multiple_of((code >> 6) << 3, SUBLANES), SUBLANES), :]
            gates = jnp.where(sub == j, pltpu.roll(aff8, code & (SUBLANES - 1), axis=0), gates)
        gate_ref[0, pl.ds(pl.multiple_of(g * SUBLANES, SUBLANES), SUBLANES), :] = gates
        return carry

    lax.fori_loop(0, cp // SUBLANES, group, 0)
    for s in range(o_ref.shape[2] // LANES):
        o_ref[0, :, s * LANES:(s + 1) * LANES] = tile[pl.ds(s, cp, stride=SUBLANES), :].astype(bf16)


def _gather_tokens(idx, hn_tiles, aff, d):
    b, n_e, cp = idx.shape
    rows = hn_tiles.shape[1]
    l = aff.shape[1]
    scalars = pl.BlockSpec((1, 1, cp), lambda bi, e: (bi * n_e + e, 0, 0), memory_space=pltpu.SMEM)
    flat = idx.reshape(b * n_e, 1, cp)
    slot = jnp.arange(cp, dtype=jnp.int32)[None, None, :]
    return pl.pallas_call(
        _gather_kernel,
        grid=(b, n_e),
        in_specs=[
            scalars,
            pl.BlockSpec((1, rows, LANES), lambda bi, e: (bi, 0, 0)),
            pl.BlockSpec((1, l, LANES), lambda bi, e: (bi, 0, 0)),
        ],
        out_specs=[
            pl.BlockSpec((1, cp, d), lambda bi, e: (e, bi, 0)),
            pl.BlockSpec((1, cp, LANES), lambda bi, e: (e, bi, 0)),
        ],
        out_shape=[
            jax.ShapeDtypeStruct((n_e, b * cp, d), bf16),
            jax.ShapeDtypeStruct((n_e, b * cp, LANES), f32),
        ],
        scratch_shapes=[pltpu.VMEM((cp * SUBLANES, LANES), f32)],
        compiler_params=pltpu.CompilerParams(
            dimension_semantics=("arbitrary", "arbitrary"), vmem_limit_bytes=48 * MIB),
        name="gather_tokens",
    )(flat * SUBLANES + (slot - flat) % SUBLANES, hn_tiles, aff)


def _moe_kernel(x_ref, gate_ref, wg_ref, wu_ref, wd_ref, y_ref, wg_b, wu_b, wd_b, acc, *, cp):
    f = pl.program_id(1)
    n_f = pl.num_programs(1)
    wg_b[...] = wg_ref[0].astype(bf16)
    wu_b[...] = wu_ref[0].astype(bf16)
    wd_b[...] = wd_ref[0].astype(bf16)

    @pl.when(f == 0)
    def _():
        acc[...] = jnp.zeros(acc.shape, f32)

    n_chunks = acc.shape[0] // cp
    for c in range(n_chunks):
        rows = slice(c * cp, (c + 1) * cp)
        x = x_ref[0, rows, :]
        g = jnp.dot(x, wg_b[...], preferred_element_type=f32)
        u = jnp.dot(x, wu_b[...], preferred_element_type=f32)
        hid = (g * jax.nn.sigmoid(g) * u).astype(bf16)
        acc[rows, :] += jnp.dot(hid, wd_b[...], preferred_element_type=f32)

    @pl.when(f == n_f - 1)
    def _():
        for c in range(n_chunks):
            rows = slice(c * cp, (c + 1) * cp)
            lane = lax.broadcasted_iota(jnp.int32, (cp, LANES), 1)
            gate = jnp.sum(jnp.where(lane == pl.program_id(0), gate_ref[0, rows, :], 0.0),
                           axis=1, keepdims=True)
            y = acc[rows, :] * gate
            for s in range(y.shape[1] // LANES):
                y_ref[0, pl.ds(c * cp * SUBLANES + s, cp, stride=SUBLANES), :] = (
                    y[:, s * LANES:(s + 1) * LANES])


def _expert_ffn(xg, gate, w_gate, w_up, w_down, cp):
    n_e, m, d = xg.shape
    d_ff = w_gate.shape[2]
    kernel = functools.partial(_moe_kernel, cp=cp)
    return pl.pallas_call(
        kernel,
        grid=(n_e, d_ff // FF_TILE),
        in_specs=[
            pl.BlockSpec((1, m, d), lambda e, f: (e, 0, 0)),
            pl.BlockSpec((1, m, LANES), lambda e, f: (e, 0, 0)),
            pl.BlockSpec((1, d, FF_TILE), lambda e, f: (e, 0, f)),
            pl.BlockSpec((1, d, FF_TILE), lambda e, f: (e, 0, f)),
            pl.BlockSpec((1, FF_TILE, d), lambda e, f: (e, f, 0)),
        ],
        out_specs=pl.BlockSpec((1, m * SUBLANES, LANES), lambda e, f: (e, 0, 0)),
        out_shape=jax.ShapeDtypeStruct((n_e, m * SUBLANES, LANES), f32),
        scratch_shapes=[
            pltpu.VMEM((d, FF_TILE), bf16),
            pltpu.VMEM((d, FF_TILE), bf16),
            pltpu.VMEM((FF_TILE, d), bf16),
            pltpu.VMEM((m, d), f32),
        ],
        compiler_params=pltpu.CompilerParams(
            dimension_semantics=("arbitrary", "arbitrary"), vmem_limit_bytes=56 * MIB),
        name="expert_ffn",
    )(xg, gate, w_gate, w_up, w_down)


def _combine_kernel(idx_ref, y_ref, h2_ref, g_ref, o_ref, acc, *, n_experts):
    step = pl.program_id(1)
    cp = idx_ref.shape[2]
    tc = o_ref.shape[1]

    @pl.when(step == 0)
    def _():
        acc[...] = jnp.zeros(acc.shape, f32)

    @pl.when(step < n_experts)
    def _():
        def group(g, carry):
            dst = []
            val = []
            for j in range(SUBLANES):
                slot = g * SUBLANES + j
                r = pl.multiple_of(idx_ref[0, 0, slot] * SUBLANES, SUBLANES)
                dst.append(r)
                val.append(acc[pl.ds(r, SUBLANES), :]
                           + y_ref[0, pl.ds(pl.multiple_of(slot * SUBLANES, SUBLANES), SUBLANES), :])
            for r, v in zip(dst, val):
                acc[pl.ds(r, SUBLANES), :] = v
            return carry

        lax.fori_loop(0, cp // SUBLANES, group, 0)

    @pl.when(step >= n_experts)
    def _():
        r0 = pl.multiple_of((step - n_experts) * tc * SUBLANES, SUBLANES)
        moe = jnp.concatenate(
            [acc[pl.ds(r0 + s, tc, stride=SUBLANES), :] for s in range(o_ref.shape[2] // LANES)], axis=-1)
        o_ref[0] = _rms(h2_ref[0] + moe, g_ref[...])


def _combine(idx, y, h2, g, n_body):
    b, n_e, cp = idx.shape
    _, l, d = h2.shape
    n_out = n_body // OUT_CHUNK
    kernel = functools.partial(_combine_kernel, n_experts=n_e)
    expert = lambda s: jnp.minimum(s, n_e - 1)
    chunk = lambda s: jnp.maximum(s - n_e, 0)
    return pl.pallas_call(
        kernel,
        grid=(b, n_e + n_out),
        in_specs=[
            pl.BlockSpec((1, 1, cp), lambda bi, s: (bi * n_e + expert(s), 0, 0), memory_space=pltpu.SMEM),
            pl.BlockSpec((1, cp * SUBLANES, LANES), lambda bi, s: (expert(s), bi, 0)),
            pl.BlockSpec((1, OUT_CHUNK, d), lambda bi, s: (bi, chunk(s), 0)),
            pl.BlockSpec((1, d), lambda bi, s: (0, 0)),
        ],
        out_specs=pl.BlockSpec((1, OUT_CHUNK, d), lambda bi, s: (bi, chunk(s), 0)),
        out_shape=jax.ShapeDtypeStruct((b, n_body, d), f32),
        scratch_shapes=[pltpu.VMEM(((l + 1) * SUBLANES, LANES), f32)],
        compiler_params=pltpu.CompilerParams(
            dimension_semantics=("arbitrary", "arbitrary"), vmem_limit_bytes=48 * MIB),
        name="combine",
    )(idx.reshape(b * n_e, 1, cp), y, h2, g)


def _round_up(n, k):
    return -(-n // k) * k


def kernel(x, meta_tokens, g_mix, w_in, lam_vecs, subln_gain, pool_w, pool_scale, w_o, g_ffn, router_w,
           w_gate, w_up, w_down, g_final):
    b, n_body, d = x.shape
    l = n_body + N_META
    qk = HEADS * D_QK
    attn_width = HEADS * D_V
    depth = g_mix.shape[0]
    cap = CAPACITY_FACTOR * l // N_EXPERTS
    cp = _round_up(cap, BF16_ROWS)

    meta = jnp.broadcast_to(meta_tokens[None].astype(x.dtype), (b, N_META, d))
    h = jnp.concatenate([x, meta], axis=1)
    slopes = jnp.broadcast_to(
        jnp.array([_alibi_slope(i) * LOG2E for i in range(HEADS)], f32)[:, None, None], (HEADS, 1, LANES))

    def per_head_pairs(cols):
        return cols.reshape(d, 2, HEADS, D_QK).transpose(0, 2, 1, 3).reshape(d, 2 * qk)

    out = None
    for layer in range(depth):
        lam_init = 0.8 - 0.6 * math.exp(-0.3 * layer)
        w = w_in[layer]
        w_q = per_head_pairs(w[:, :2 * qk]) * (LOG2E / math.sqrt(D_QK))
        w_k = per_head_pairs(w[:, 2 * qk:4 * qk])
        w_v = w[:, 4 * qk:4 * qk + attn_width]
        w_u = w[:, 4 * qk + attn_width:]
        w_qv_t = jnp.concatenate([w_q, w_v], axis=1).T.astype(bf16)
        w_ku = jnp.concatenate([w_k, w_u], axis=1).astype(bf16)
        qvt, k, u = _norm_inproj(h, g_mix[layer][None, :], w_qv_t, w_ku, 2 * qk)
        a = _diff_attention(qvt, k, lam_vecs[layer].astype(f32), subln_gain[layer][None, :],
                            slopes, n_body, lam_init)
        p = _multiscale_pool(u, pool_w[layer].astype(bf16), pool_scale[layer][None, :], n_body)
        wo = w_o[layer].astype(bf16)
        h2, hn_tiles, aff = _outproj_router(
            h.reshape(b * l, d), a.reshape(b * l, -1), p.reshape(b * l, -1), wo[:attn_width], wo[attn_width:],
            g_ffn[layer][None, :],
            jnp.pad(router_w[layer], ((0, 0), (0, LANES - N_EXPERTS))).astype(bf16))

        aff = aff.reshape(b, l, LANES)
        lp = _round_up(l, LANES)
        aff_pos = jnp.concatenate(
            [aff[:, n_body:, :N_EXPERTS], aff[:, :n_body, :N_EXPERTS], jnp.full((b, lp - l, N_EXPERTS), -1.0, f32)],
            axis=1)
        idx = _topk_rows(jnp.swapaxes(aff_pos, 1, 2), cap, cp, n_body)
        idx = jnp.swapaxes(idx[:, :, :N_EXPERTS], 1, 2)
        real = jnp.arange(cp)[None, None, :] < cap
        idx_gather = jnp.where(real, idx, 0)
        idx_scatter = jnp.where(real, idx, l)

        xg, gate = _gather_tokens(idx_gather, hn_tiles.reshape(b, l * SUBLANES, LANES), aff, d)
        y = _expert_ffn(xg, gate, w_gate[layer], w_up[layer], w_down[layer], cp)
        last = layer == depth - 1
        g_out = g_final[None, :] if last else jnp.ones((1, d), f32)
        out = _combine(idx_scatter, y, h2.reshape(b, l, d), g_out, n_body)
    return out
```

```python
import functools
import math

import jax
import jax.numpy as jnp
from jax import lax
from jax.experimental import pallas as pl
from jax.experimental.pallas import tpu as pltpu

N_META = 16
HEADS = 4
D_QK = 64
D_V = 128
POOL_WINDOWS = (2, 4, 8, 16)
POOL_HALO = max(POOL_WINDOWS) // 2
N_EXPERTS = 16
CAPACITY_FACTOR = 2
EPS = 1e-6
LOG2E = 1.4426950408889634


def _alibi_slope(head):
    return 2.0 ** (-8.0 * (head + 1) / HEADS)


LANES = 128
SUBLANES = 8
BF16_ROWS = 16

ROW_TILE = 512
ATTN_TQ = 256
ATTN_TK = 512
POOL_CHUNK = 256
FF_TILE = 256
OUT_CHUNK = 512
MIB = 1024 * 1024

f32 = jnp.float32
bf16 = jnp.bfloat16


def _rms(x, g):
    return x * lax.rsqrt(jnp.mean(x * x, axis=-1, keepdims=True) + EPS) * g


def _tile_rows(x_ref, meta_ref):
    x = x_ref[0]
    meta = jnp.concatenate(
        [meta_ref[...], jnp.zeros((x.shape[0] - meta_ref.shape[0], x.shape[1]), x.dtype)], axis=0)
    return jnp.where(pl.program_id(1) == pl.num_programs(1) - 1, meta, x)


def _norm_inproj_kernel(x_ref, meta_ref, g_ref, wt_ref, w_ref, qvt_ref, k_ref, u_ref):
    n = _rms(_tile_rows(x_ref, meta_ref), g_ref[...]).astype(bf16)
    qvt = lax.dot_general(wt_ref[...], n, (((1,), (1,)), ((), ())), preferred_element_type=f32)
    qvt_ref[0, 0] = qvt.astype(bf16)
    ku = jnp.dot(n, w_ref[...], preferred_element_type=f32)
    n_k = k_ref.shape[-1]
    k_ref[0] = ku[:, :n_k].astype(bf16)
    u_ref[0] = ku[:, n_k:]


def _norm_inproj(x, meta, g, w_qv_t, w_ku, n_k):
    b, n_body, d = x.shape
    assert n_body % ATTN_TK == 0 and meta.shape[0] <= ATTN_TK
    l = n_body + meta.shape[0]
    n_t = n_body // ATTN_TK + 1
    n_qv = w_qv_t.shape[0]
    n_u = w_ku.shape[1] - n_k
    return pl.pallas_call(
        _norm_inproj_kernel,
        grid=(b, n_t),
        in_specs=[
            pl.BlockSpec((1, ATTN_TK, d), lambda bi, i: (bi, jnp.minimum(i, n_t - 2), 0)),
            pl.BlockSpec(meta.shape, lambda bi, i: (0, 0)),
            pl.BlockSpec((1, d), lambda bi, i: (0, 0)),
            pl.BlockSpec(w_qv_t.shape, lambda bi, i: (0, 0)),
            pl.BlockSpec(w_ku.shape, lambda bi, i: (0, 0)),
        ],
        out_specs=[
            pl.BlockSpec((1, 1, n_qv, ATTN_TK), lambda bi, i: (bi, i, 0, 0)),
            pl.BlockSpec((1, ATTN_TK, n_k), lambda bi, i: (bi, i, 0)),
            pl.BlockSpec((1, ATTN_TK, n_u), lambda bi, i: (bi, i, 0)),
        ],
        out_shape=[
            jax.ShapeDtypeStruct((b, n_t, n_qv, ATTN_TK), bf16),
            jax.ShapeDtypeStruct((b, l, n_k), bf16),
            jax.ShapeDtypeStruct((b, l, n_u), f32),
        ],
        compiler_params=pltpu.CompilerParams(
            dimension_semantics=("arbitrary", "arbitrary"), vmem_limit_bytes=40 * MIB),
        name="norm_inproj",
    )(x, meta, g, w_qv_t, w_ku)


def _attn_kernel(lam_ref, gain_ref, slope_ref, q_ref, qe_ref, k_ref, ke_ref, vt_ref, o_ref,
                 qs_scr, bias_scr, tail_scr, vt_scr, *scr, n_body, lam_init):
    tq = ATTN_TQ
    tk = ATTN_TK
    n_diag = tk // tq
    n_sub = q_ref.shape[3] // tq
    i = pl.program_id(2)
    n_q = pl.num_programs(2)
    is_meta = i == n_q - 1
    slope = slope_ref[0][:, :1]

    @pl.when(i == 0)
    def _():
        d = (lax.broadcasted_iota(jnp.int32, (tk, tq), 0)
             - lax.broadcasted_iota(jnp.int32, (tk, tq), 1)).astype(f32)
        for v in range(n_diag):
            bias_scr[v] = -slope * jnp.abs(d - float(v * tq))
        tail_scr[0] = slope * d[0:N_META]
        tail_scr[1] = -slope * jnp.abs(d[0:N_META])
        ones_row = jnp.where(lax.broadcasted_iota(jnp.int32, (BF16_ROWS, tk), 0) == 0, 1.0, 0.0).astype(bf16)
        for c in range(vt_ref.shape[1]):
            vt_scr[c, 0:D_V, :] = vt_ref[0, c]
            vt_scr[c, D_V:, :] = ones_row
        width = qs_scr.shape[2]
        qs_scr[:, 2 * D_QK + BF16_ROWS:, :] = jnp.zeros((3, 2 * LANES - 2 * D_QK - BF16_ROWS, width), bf16)
        qs_scr[2, 2 * D_QK:2 * D_QK + BF16_ROWS, :] = jnp.zeros((BF16_ROWS, width), bf16)

    zero = jnp.zeros((D_QK, tq), bf16)
    for sub in range(n_sub):
        q = q_ref[0, 0, :, sub * tq:(sub + 1) * tq]
        cols = slice(2 * sub * tq, 2 * (sub + 1) * tq)
        top = jnp.concatenate([q[0:D_QK], zero], axis=1)
        bottom = jnp.concatenate([zero, q[D_QK:]], axis=1)
        for v in range(3):
            qs_scr[v, 0:D_QK, cols] = top
            qs_scr[v, D_QK:2 * D_QK, cols] = bottom
        for v in range(2):
            ext = qe_ref[0, v, :, sub * tq:(sub + 1) * tq]
            qs_scr[v, 2 * D_QK:2 * D_QK + BF16_ROWS, cols] = jnp.concatenate([ext, ext], axis=1)

    n_streams = 2 * n_sub
    per_stream = len(scr) // n_streams
    streams = [scr[st * per_stream:(st + 1) * per_stream] for st in range(n_streams)]
    for stream in streams:
        stream[6][...] = jnp.full(stream[6].shape, -jnp.inf, f32)
        stream[7][...] = jnp.zeros(stream[7].shape, f32)

    q_pos0 = jnp.where(is_meta, 0, N_META + i * n_sub * tq)
    q_valid = jnp.where(is_meta, N_META, n_sub * tq)
    n_slab = tq // LANES
    piece = BF16_ROWS

    s_at, p_at, a_at = (0, 1, 8), (2, 3, 9), (4, 5, 10)

    def issue_scores(k, version, slot):
        n = k.shape[0]
        for st, stream in enumerate(streams):
            stream[s_at[slot]][0:n, :] = jnp.dot(k, qs_scr[version, 0:k.shape[1], st * tq:(st + 1) * tq],
                                                 preferred_element_type=f32)

    def softmax(slot, n, bias_tile=None, consts=None):
        def scores(s_scr, r, slab):
            t = s_scr[r:r + piece, slab * LANES:(slab + 1) * LANES]
            return t if bias_tile is None else t + bias_tile(r, slab)

        stats = []
        for st, stream in enumerate(streams):
            s_scr, m_scr = stream[s_at[slot]], stream[6]
            const = None if consts is None else consts[st // 2]
            m_old = m_scr[...]
            maxes = []
            for slab in range(n_slab):
                mx = scores(s_scr, 0, slab)
                for r in range(piece, n, piece):
                    mx = jnp.maximum(mx, scores(s_scr, r, slab))
                maxes.append(jnp.max(mx, axis=0, keepdims=True))
            m_loc = jnp.concatenate(maxes, axis=1)
            m_new = jnp.maximum(m_old, m_loc if const is None else m_loc + const)
            stats.append((m_new, m_new if const is None else m_new - const, jnp.exp2(m_old - m_new)))

        for stream, (m_new, shift, alpha) in zip(streams, stats):
            s_scr, p_scr, a_scr, m_scr = stream[s_at[slot]], stream[p_at[slot]], stream[a_at[slot]], stream[6]
            for slab in range(n_slab):
                cols = slice(slab * LANES, (slab + 1) * LANES)
                sub_b = jnp.broadcast_to(shift[:, cols], (piece, LANES))
                for r in range(0, n, piece):
                    p_scr[r:r + piece, cols] = jnp.exp2(scores(s_scr, r, slab) - sub_b).astype(bf16)
            a_scr[...] = alpha
            m_scr[...] = m_new

    def apply_values(vt, slot):
        n = vt.shape[1]
        for stream in streams:
            p_scr, a_scr, acc_scr = stream[p_at[slot]], stream[a_at[slot]], stream[7]
            acc_scr[...] = a_scr[...] * acc_scr[...] + jnp.dot(vt, p_scr[0:n, :], preferred_element_type=f32)

    def keys_with_positions(c):
        rows = pl.ds(pl.multiple_of(c * tk, tk), tk)
        return jnp.concatenate([k_ref[0, rows, :], ke_ref[0, rows, :]], axis=1)

    n_chunks = n_body // tk

    def issue_chunk_scores(c, slot):
        k_pos0 = N_META + c * tk
        version = jnp.where(k_pos0 + tk <= q_pos0, 0, jnp.where(k_pos0 >= q_pos0 + q_valid, 1, 2))
        issue_scores(keys_with_positions(c), version, slot)

    def add_straddle_bias(c, slot):
        k_pos0 = N_META + c * tk

        @pl.when(jnp.logical_and(k_pos0 + tk > q_pos0, k_pos0 < q_pos0 + q_valid))
        def _():
            for st, stream in enumerate(streams):
                variant = (q_pos0 - k_pos0) // tq + st // 2
                for slab in range(n_slab):
                    cols = slice(slab * LANES, (slab + 1) * LANES)
                    for r in range(0, tk, piece):
                        stream[slot][r:r + piece, cols] = (stream[slot][r:r + piece, cols]
                                                           + bias_scr[variant, r:r + piece, cols])

    def step(t, slot, issue_next, apply_prev, meta_keys=None):
        add_straddle_bias(t, slot)
        if issue_next:
            issue_chunk_scores(t + 1, 1 - slot)
        if meta_keys is not None:
            softmax(2, N_META, *meta_keys)
        softmax(slot, tk)
        if apply_prev:
            apply_values(vt_scr[t - 1], 1 - slot)
        if meta_keys is not None:
            apply_values(vt_scr[n_chunks, :, 0:N_META], 2)

    def run():
        def pair(u, carry):
            step(2 * u, 0, True, True)
            step(2 * u + 1, 1, True, True)
            return carry

        tail = jnp.where(is_meta, 1, 0)
        consts = [-slope * jnp.full((1, 1), q_pos0 + sub * tq, jnp.int32).astype(f32) for sub in range(n_sub)]
        meta_keys = (lambda r, slab: tail_scr[tail, r:r + piece, slab * LANES:(slab + 1) * LANES], consts)
        issue_scores(k_ref[0, n_body:n_body + N_META, :], 2, 2)
        issue_chunk_scores(0, 0)
        step(0, 0, True, False, meta_keys)
        step(1, 1, True, True)
        lax.fori_loop(1, n_chunks // 2 - 1, pair, 0)
        step(n_chunks - 2, 0, True, True)
        step(n_chunks - 1, 1, False, True)
        apply_values(vt_scr[n_chunks - 1], 1)

        lv = lam_ref[...]
        lam = (jnp.exp(jnp.sum(lv[0:1] * lv[1:2], axis=-1, keepdims=True))
               - jnp.exp(jnp.sum(lv[2:3] * lv[3:4], axis=-1, keepdims=True)) + lam_init)
        for sub in range(n_sub):
            o1, o2 = [stream[7][0:D_V, :] / stream[7][D_V:D_V + 1, :] for stream in streams[2 * sub:2 * sub + 2]]
            att = (o1 - lam * o2).T
            o_ref[0, sub * tq:(sub + 1) * tq, :] = (_rms(att, gain_ref[...]) * (1.0 - lam_init)).astype(bf16)

    pl.when(jnp.logical_not(is_meta))(lambda: run())
    streams, n_sub = streams[:2], 1
    pl.when(is_meta)(lambda: run())


def _alibi_tables(n_body, n_q_tiles):
    l = n_body + N_META
    pos = jnp.concatenate([jnp.arange(n_body) + N_META, jnp.arange(N_META)]).astype(f32)
    q_pos = jnp.pad(pos, (0, n_q_tiles * ATTN_TQ - l))
    k_hi, k_lo = jnp.floor(pos / 64), jnp.mod(pos, 64)
    q_hi, q_lo = jnp.floor(q_pos / 64), jnp.mod(q_pos, 64)
    ke, qe = [], []
    for h in range(HEADS):
        c = jnp.float32(_alibi_slope(h) * LOG2E)
        c1 = c.astype(bf16).astype(f32)
        c2 = (c - c1).astype(bf16).astype(f32)
        c3 = (c - c1 - c2).astype(bf16).astype(f32)
        pieces = [c1, c2, c3]
        cols = [k_hi] * 3 + [k_lo] * 3 + [jnp.full_like(pos, p) for p in pieces] * 2
        ke.append(jnp.pad(jnp.stack(cols, axis=1), ((0, 0), (0, LANES - len(cols)))))
        rows = ([jnp.full_like(q_pos, 64 * p) for p in pieces] + [jnp.full_like(q_pos, p) for p in pieces]
                + [-64 * q_hi] * 3 + [-q_lo] * 3)
        ext = jnp.pad(jnp.stack(rows), ((0, BF16_ROWS - len(rows)), (0, 0)))
        qe.append(jnp.stack([ext, -ext]))
    return jnp.stack(ke).astype(bf16), jnp.stack(qe).astype(bf16)


def _diff_attention(qvt, k, lam_vecs, gain, slopes, n_body, lam_init):
    b, l, _ = k.shape
    n_t = qvt.shape[1]
    per_chunk = ATTN_TK // ATTN_TQ
    n_q = pl.cdiv(l, ATTN_TK)
    assert ATTN_TK % ATTN_TQ == 0 and n_body % (2 * ATTN_TK) == 0 and n_body >= 4 * ATTN_TK
    ke, qe = _alibi_tables(n_body, n_q * per_chunk)
    kernel = functools.partial(_attn_kernel, n_body=n_body, lam_init=lam_init)
    return pl.pallas_call(
        kernel,
        grid=(b, HEADS, n_q),
        in_specs=[
            pl.BlockSpec(lam_vecs.shape, lambda bi, h, i: (0, 0)),
            pl.BlockSpec((1, D_V), lambda bi, h, i: (0, 0)),
            pl.BlockSpec((1, 1, LANES), lambda bi, h, i: (h, 0, 0)),
            pl.BlockSpec((1, 1, 2 * D_QK, ATTN_TK), lambda bi, h, i: (bi, i, h, 0)),
            pl.BlockSpec((1, 2, BF16_ROWS, ATTN_TK), lambda bi, h, i: (h, 0, 0, i)),
            pl.BlockSpec((1, l, 2 * D_QK), lambda bi, h, i: (bi, 0, h)),
            pl.BlockSpec((1, l, LANES), lambda bi, h, i: (h, 0, 0)),
            pl.BlockSpec((1, n_t, D_V, ATTN_TK), lambda bi, h, i: (bi, 0, HEADS + h, 0)),
        ],
        out_specs=pl.BlockSpec((1, ATTN_TK, D_V), lambda bi, h, i: (bi, i, h)),
        out_shape=jax.ShapeDtypeStruct((b, l, HEADS * D_V), bf16),
        scratch_shapes=[
            pltpu.VMEM((3, 2 * LANES, 2 * ATTN_TK), bf16),
            pltpu.VMEM((per_chunk, ATTN_TK, ATTN_TQ), f32),
            pltpu.VMEM((2, N_META, ATTN_TQ), f32),
            pltpu.VMEM((n_t, D_V + BF16_ROWS, ATTN_TK), bf16),
        ] + 2 * per_chunk * [
            pltpu.VMEM((ATTN_TK, ATTN_TQ), f32), pltpu.VMEM((ATTN_TK, ATTN_TQ), f32),
            pltpu.VMEM((ATTN_TK, ATTN_TQ), bf16), pltpu.VMEM((ATTN_TK, ATTN_TQ), bf16),
            pltpu.VMEM((1, ATTN_TQ), f32), pltpu.VMEM((1, ATTN_TQ), f32),
            pltpu.VMEM((1, ATTN_TQ), f32),
            pltpu.VMEM((D_V + BF16_ROWS, ATTN_TQ), f32),
            pltpu.VMEM((N_META, ATTN_TQ), f32), pltpu.VMEM((N_META, ATTN_TQ), bf16), pltpu.VMEM((1, ATTN_TQ), f32),
        ],
        compiler_params=pltpu.CompilerParams(
            dimension_semantics=("arbitrary", "arbitrary", "arbitrary"), vmem_limit_bytes=40 * MIB),
        name="diff_attention",
    )(lam_vecs, gain, slopes, qvt, qe, k, ke, qvt)


def _pool_kernel(u_ref, pw_ref, ps_ref, o_ref, upad, *, n_body):
    l = n_body + N_META
    width = u_ref.shape[-1]
    upad[0:POOL_HALO, :] = jnp.zeros((POOL_HALO, width), f32)
    upad[POOL_HALO:POOL_HALO + N_META, :] = u_ref[0, n_body:l, :]
    upad[POOL_HALO + N_META:POOL_HALO + l, :] = u_ref[0, 0:n_body, :]
    upad[POOL_HALO + l:2 * POOL_HALO + l, :] = jnp.zeros((POOL_HALO, width), f32)

    def rows(pos0, n, out_row0):
        slab = upad[pl.ds(pos0, n + 2 * POOL_HALO), :]
        t = pos0 + lax.broadcasted_iota(jnp.int32, (n, 1), 0)
        for gi, w in enumerate(POOL_WINDOWS):
            cols = slice(gi * LANES, (gi + 1) * LANES)
            total = slab[POOL_HALO - w // 2:POOL_HALO - w // 2 + n, cols]
            for j in range(1, w):
                total = total + slab[POOL_HALO - w // 2 + j:POOL_HALO - w // 2 + j + n, cols]
            cnt = (jnp.minimum(t + w // 2, l) - jnp.maximum(t - w // 2, 0)).astype(f32)
            pooled = total / cnt - slab[POOL_HALO:POOL_HALO + n, cols]
            y = jnp.dot(pooled.astype(bf16), pw_ref[gi], preferred_element_type=f32) * ps_ref[:, cols]
            o_ref[0, pl.ds(out_row0, n), cols] = y.astype(bf16)

    rows(0, N_META, n_body)

    def body(c, carry):
        r0 = pl.multiple_of(c * POOL_CHUNK, POOL_CHUNK)
        rows(N_META + r0, POOL_CHUNK, r0)
        return carry

    lax.fori_loop(0, n_body // POOL_CHUNK, body, 0)


def _multiscale_pool(u, pool_w, pool_scale, n_body):
    b, l, width = u.shape
    kernel = functools.partial(_pool_kernel, n_body=n_body)
    return pl.pallas_call(
        kernel,
        grid=(b,),
        in_specs=[
            pl.BlockSpec((1, l, width), lambda bi: (bi, 0, 0)),
            pl.BlockSpec(pool_w.shape, lambda bi: (0, 0, 0)),
            pl.BlockSpec((1, width), lambda bi: (0, 0)),
        ],
        out_specs=pl.BlockSpec((1, l, width), lambda bi: (bi, 0, 0)),
        out_shape=jax.ShapeDtypeStruct((b, l, width), bf16),
        scratch_shapes=[pltpu.VMEM((l + 2 * POOL_HALO, width), f32)],
        compiler_params=pltpu.CompilerParams(
            dimension_semantics=("arbitrary",), vmem_limit_bytes=48 * MIB),
        name="multiscale_pool",
    )(u, pool_w, pool_scale)


def _outproj_router_kernel(x_ref, meta_ref, a_ref, p_ref, woa_ref, wop_ref, g_ref, rw_ref,
                           h2_ref, hn_ref, aff_ref):
    tm = x_ref.shape[1]
    h2 = (_tile_rows(x_ref, meta_ref)
          + jnp.dot(a_ref[0], woa_ref[...], preferred_element_type=f32)
          + jnp.dot(p_ref[0], wop_ref[...], preferred_element_type=f32))
    h2_ref[0] = h2
    hn = _rms(h2, g_ref[...])
    logits = jnp.dot(hn.astype(bf16), rw_ref[...], preferred_element_type=f32)
    lane = lax.broadcasted_iota(jnp.int32, logits.shape, 1)
    logits = jnp.where(lane < N_EXPERTS, logits, -jnp.inf)
    z = jnp.exp(logits - jnp.max(logits, axis=-1, keepdims=True))
    aff_ref[0] = z / jnp.sum(z, axis=-1, keepdims=True)
    for s in range(hn.shape[1] // LANES):
        hn_ref[0, pl.ds(s, tm, stride=SUBLANES), :] = hn[:, s * LANES:(s + 1) * LANES]


def _outproj_router(x, meta, a, p, wo_a, wo_p, g, rw):
    b, n_body, d = x.shape
    assert n_body % ROW_TILE == 0 and meta.shape[0] <= ROW_TILE
    l = n_body + meta.shape[0]
    n_t = n_body // ROW_TILE + 1
    half = a.shape[2]
    n_e = rw.shape[1]
    row = lambda bi, i: (bi, i, 0)
    fixed = lambda bi, i: (0, 0)
    return pl.pallas_call(
        _outproj_router_kernel,
        grid=(b, n_t),
        in_specs=[
            pl.BlockSpec((1, ROW_TILE, d), lambda bi, i: (bi, jnp.minimum(i, n_t - 2), 0)),
            pl.BlockSpec(meta.shape, fixed),
            pl.BlockSpec((1, ROW_TILE, half), row),
            pl.BlockSpec((1, ROW_TILE, half), row),
            pl.BlockSpec((half, d), fixed),
            pl.BlockSpec((half, d), fixed),
            pl.BlockSpec((1, d), fixed),
            pl.BlockSpec((d, n_e), fixed),
        ],
        out_specs=[
            pl.BlockSpec((1, ROW_TILE, d), row),
            pl.BlockSpec((1, ROW_TILE * SUBLANES, LANES), row),
            pl.BlockSpec((1, ROW_TILE, n_e), row),
        ],
        out_shape=[
            jax.ShapeDtypeStruct((b, l, d), f32),
            jax.ShapeDtypeStruct((b, l * SUBLANES, LANES), f32),
            jax.ShapeDtypeStruct((b, l, n_e), f32),
        ],
        compiler_params=pltpu.CompilerParams(
            dimension_semantics=("arbitrary", "arbitrary"), vmem_limit_bytes=40 * MIB),
        name="outproj_router",
    )(x, meta, a, p, wo_a, wo_p, g, rw)


def _prefix_count(flags, w_scan):
    out = []
    offset = jnp.zeros((flags.shape[0], LANES), f32)
    for blk in range(flags.shape[1] // LANES):
        r = jnp.dot(flags[:, blk * LANES:(blk + 1) * LANES].astype(bf16), w_scan, preferred_element_type=f32)
        out.append(r[:, :LANES] + offset)
        offset = offset + r[:, LANES:]
    return jnp.concatenate(out, axis=1)


def _topk_kernel(aff_ref, idx_ref, rank_scr, chosen_scr, *, cap, n_body):
    n_e, lp = aff_ref.shape[1], aff_ref.shape[2]
    cp = idx_ref.shape[1]
    n_blk = lp // LANES
    bits = lax.bitcast_convert_type(aff_ref[0], jnp.int32)

    def refine(step, thr):
        cand = thr | (1 << (30 - step))
        cnt = jnp.sum(jnp.where(bits >= cand, 1, 0), axis=1, keepdims=True)
        return jnp.where(cnt >= cap, cand, thr)

    thr = lax.fori_loop(0, 31, refine, jnp.zeros((n_e, 1), jnp.int32))
    aff = aff_ref[0]
    pivot = jnp.max(jnp.where(bits == thr, aff, -1.0), axis=1, keepdims=True)
    above = aff > pivot
    tie = aff == pivot
    need = (cap - jnp.sum(jnp.where(above, 1, 0), axis=1, keepdims=True)).astype(f32)

    row = lax.broadcasted_iota(jnp.int32, (LANES, 2 * LANES), 0)
    col = lax.broadcasted_iota(jnp.int32, (LANES, 2 * LANES), 1)
    w_scan = jnp.where((row <= col) | (col >= LANES), 1.0, 0.0).astype(bf16)

    tie_rank = _prefix_count(jnp.where(tie, 1.0, 0.0), w_scan)
    chosen_scr[...] = jnp.where(above | (tie & (tie_rank <= need)), 1.0, 0.0)
    pos_f = lax.broadcasted_iota(jnp.int32, (n_e, lp), 1).astype(f32)
    big = float(2 * lp)

    def excess(cnt):
        return jnp.sum(jnp.abs(cnt - float(cap))) > 0.0

    def repair(cnt):
        a = aff_ref[0]
        sel = chosen_scr[...] > 0.5
        lo = jnp.min(jnp.where(sel, a, jnp.inf), axis=1, keepdims=True)
        lo_at = jnp.max(jnp.where(sel & (a == lo), pos_f, -1.0), axis=1, keepdims=True)
        drop = (pos_f == lo_at) & (cnt > float(cap))
        free = (~sel) & (a >= 0.0)
        hi = jnp.max(jnp.where(free, a, -1.0), axis=1, keepdims=True)
        hi_at = jnp.min(jnp.where(free & (a == hi), pos_f, big), axis=1, keepdims=True)
        add = (pos_f == hi_at) & (cnt < float(cap))
        new = jnp.where(add, 1.0, jnp.where(drop, 0.0, chosen_scr[...]))
        chosen_scr[...] = new
        return jnp.sum(new, axis=1, keepdims=True)

    lax.while_loop(excess, repair, jnp.sum(chosen_scr[...], axis=1, keepdims=True))
    rank = _prefix_count(chosen_scr[...], w_scan)
    for e in range(n_e):
        rank_scr[e] = rank[e:e + 1, :]

    lane = lax.broadcasted_iota(jnp.int32, (SUBLANES, LANES), 1)
    sub = lax.broadcasted_iota(jnp.int32, (SUBLANES, LANES), 0)

    n_groups = cp // SUBLANES
    unroll = next(u for u in (11, 8, 6, 5, 4, 3, 2, 1) if n_groups % u == 0)

    def per_expert(e, carry):
        def per_groups(gi, carry2):
            slots = [((gi * unroll + u) * SUBLANES + sub).astype(f32) for u in range(unroll)]
            accs = [jnp.zeros((SUBLANES, LANES), f32) for _ in range(unroll)]
            for t in range(n_blk):
                r_blk = jnp.broadcast_to(rank_scr[e, :, t * LANES:(t + 1) * LANES], (SUBLANES, LANES))
                accs = [a + jnp.where(r_blk <= c, 1.0, 0.0) for a, c in zip(accs, slots)]
            for u in range(unroll):
                pos = jnp.sum(accs[u], axis=1, keepdims=True).astype(jnp.int32)
                rows = jnp.where(pos < N_META, pos + n_body, pos - N_META)
                at = pl.ds(pl.multiple_of((gi * unroll + u) * SUBLANES, SUBLANES), SUBLANES)
                idx_ref[0, at, :] = jnp.where(lane == e, rows, idx_ref[0, at, :])
            return carry2

        return lax.fori_loop(0, n_groups // unroll, per_groups, carry)

    idx_ref[0] = jnp.zeros((cp, LANES), jnp.int32)
    lax.fori_loop(0, n_e, per_expert, 0)


def _topk_rows(aff_pos, cap, cp, n_body):
    b, n_e, lp = aff_pos.shape
    kernel = functools.partial(_topk_kernel, cap=cap, n_body=n_body)
    return pl.pallas_call(
        kernel,
        grid=(b,),
        in_specs=[pl.BlockSpec((1, n_e, lp), lambda bi: (bi, 0, 0))],
        out_specs=pl.BlockSpec((1, cp, LANES), lambda bi: (bi, 0, 0)),
        out_shape=jax.ShapeDtypeStruct((b, cp, LANES), jnp.int32),
        scratch_shapes=[pltpu.VMEM((n_e, 1, lp), f32), pltpu.VMEM((n_e, lp), f32)],
        compiler_params=pltpu.CompilerParams(
            dimension_semantics=("arbitrary",), vmem_limit_bytes=32 * MIB),
        name="topk_rows",
    )(aff_pos)


def _gather_kernel(code_ref, src_ref, aff_ref, o_ref, gate_ref, tile):
    cp = o_ref.shape[1]
    sub = lax.broadcasted_iota(jnp.int32, (SUBLANES, LANES), 0)

    def group(g, carry):
        gates = jnp.zeros((SUBLANES, LANES), f32)
        for j in range(SUBLANES):
            slot = g * SUBLANES + j
            code = code_ref[0, 0, slot]
            tile[pl.ds(pl.multiple_of(slot * SUBLANES, SUBLANES), SUBLANES), :] = (
                src_ref[0, pl.ds(pl.multiple_of(code & -SUBLANES, SUBLANES), SUBLANES), :])
            aff8 = aff_ref[0, pl.ds(pl.multiple_of((code >> 6) << 3, SUBLANES), SUBLANES), :]
            gates = jnp.where(sub == j, pltpu.roll(aff8, code & (SUBLANES - 1), axis=0), gates)
        gate_ref[0, pl.ds(pl.multiple_of(g * SUBLANES, SUBLANES), SUBLANES), :] = gates
        return carry

    lax.fori_loop(0, cp // SUBLANES, group, 0)
    for s in range(o_ref.shape[2] // LANES):
        o_ref[0, :, s * LANES:(s + 1) * LANES] = tile[pl.ds(s, cp, stride=SUBLANES), :].astype(bf16)


def _gather_tokens(idx, hn_tiles, aff, d):
    b, n_e, cp = idx.shape
    rows = hn_tiles.shape[1]
    l = aff.shape[1]
    scalars = pl.BlockSpec((1, 1, cp), lambda bi, e: (bi * n_e + e, 0, 0), memory_space=pltpu.SMEM)
    flat = idx.reshape(b * n_e, 1, cp)
    slot = jnp.arange(cp, dtype=jnp.int32)[None, None, :]
    return pl.pallas_call(
        _gather_kernel,
        grid=(b, n_e),
        in_specs=[
            scalars,
            pl.BlockSpec((1, rows, LANES), lambda bi, e: (bi, 0, 0)),
            pl.BlockSpec((1, l, LANES), lambda bi, e: (bi, 0, 0)),
        ],
        out_specs=[
            pl.BlockSpec((1, cp, d), lambda bi, e: (e, bi, 0)),
            pl.BlockSpec((1, cp, LANES), lambda bi, e: (e, bi, 0)),
        ],
        out_shape=[
            jax.ShapeDtypeStruct((n_e, b * cp, d), bf16),
            jax.ShapeDtypeStruct((n_e, b * cp, LANES), f32),
        ],
        scratch_shapes=[pltpu.VMEM((cp * SUBLANES, LANES), f32)],
        compiler_params=pltpu.CompilerParams(
            dimension_semantics=("arbitrary", "arbitrary"), vmem_limit_bytes=48 * MIB),
        name="gather_tokens",
    )(flat * SUBLANES + (slot - flat) % SUBLANES, hn_tiles, aff)


def _moe_kernel(x_ref, gate_ref, wg_ref, wu_ref, wd_ref, y_ref, wg_b, wu_b, wd_b, acc, *, cp):
    f = pl.program_id(1)
    n_f = pl.num_programs(1)
    wg_b[...] = wg_ref[0].astype(bf16)
    wu_b[...] = wu_ref[0].astype(bf16)
    wd_b[...] = wd_ref[0].astype(bf16)

    @pl.when(f == 0)
    def _():
        acc[...] = jnp.zeros(acc.shape, f32)

    n_chunks = acc.shape[0] // cp
    for c in range(n_chunks):
        rows = slice(c * cp, (c + 1) * cp)
        x = x_ref[0, rows, :]
        g = jnp.dot(x, wg_b[...], preferred_element_type=f32)
        u = jnp.dot(x, wu_b[...], preferred_element_type=f32)
        hid = (g * jax.nn.sigmoid(g) * u).astype(bf16)
        acc[rows, :] += jnp.dot(hid, wd_b[...], preferred_element_type=f32)

    @pl.when(f == n_f - 1)
    def _():
        for c in range(n_chunks):
            rows = slice(c * cp, (c + 1) * cp)
            lane = lax.broadcasted_iota(jnp.int32, (cp, LANES), 1)
            gate = jnp.sum(jnp.where(lane == pl.program_id(0), gate_ref[0, rows, :], 0.0),
                           axis=1, keepdims=True)
            y = acc[rows, :] * gate
            for s in range(y.shape[1] // LANES):
                y_ref[0, pl.ds(c * cp * SUBLANES + s, cp, stride=SUBLANES), :] = (
                    y[:, s * LANES:(s + 1) * LANES])


def _expert_ffn(xg, gate, w_gate, w_up, w_down, cp):
    n_e, m, d = xg.shape
    d_ff = w_gate.shape[2]
    kernel = functools.partial(_moe_kernel, cp=cp)
    return pl.pallas_call(
        kernel,
        grid=(n_e, d_ff // FF_TILE),
        in_specs=[
            pl.BlockSpec((1, m, d), lambda e, f: (e, 0, 0)),
            pl.BlockSpec((1, m, LANES), lambda e, f: (e, 0, 0)),
            pl.BlockSpec((1, d, FF_TILE), lambda e, f: (e, 0, f)),
            pl.BlockSpec((1, d, FF_TILE), lambda e, f: (e, 0, f)),
            pl.BlockSpec((1, FF_TILE, d), lambda e, f: (e, f, 0)),
        ],
        out_specs=pl.BlockSpec((1, m * SUBLANES, LANES), lambda e, f: (e, 0, 0)),
        out_shape=jax.ShapeDtypeStruct((n_e, m * SUBLANES, LANES), f32),
        scratch_shapes=[
            pltpu.VMEM((d, FF_TILE), bf16),
            pltpu.VMEM((d, FF_TILE), bf16),
            pltpu.VMEM((FF_TILE, d), bf16),
            pltpu.VMEM((m, d), f32),
        ],
        compiler_params=pltpu.CompilerParams(
            dimension_semantics=("arbitrary", "arbitrary"), vmem_limit_bytes=56 * MIB),
        name="expert_ffn",
    )(xg, gate, w_gate, w_up, w_down)


def _combine_kernel(idx_ref, y_ref, h2_ref, g_ref, o_ref, acc, *, n_experts):
    step = pl.program_id(1)
    cp = idx_ref.shape[2]
    tc = o_ref.shape[1]

    @pl.when(step == 0)
    def _():
        acc[...] = jnp.zeros(acc.shape, f32)

    @pl.when(step < n_experts)
    def _():
        def group(g, carry):
            dst = []
            val = []
            for j in range(SUBLANES):
                slot = g * SUBLANES + j
                r = pl.multiple_of(idx_ref[0, 0, slot] * SUBLANES, SUBLANES)
                dst.append(r)
                val.append(acc[pl.ds(r, SUBLANES), :]
                           + y_ref[0, pl.ds(pl.multiple_of(slot * SUBLANES, SUBLANES), SUBLANES), :])
            for r, v in zip(dst, val):
                acc[pl.ds(r, SUBLANES), :] = v
            return carry

        lax.fori_loop(0, cp // SUBLANES, group, 0)

    @pl.when(step >= n_experts)
    def _():
        r0 = pl.multiple_of((step - n_experts) * tc * SUBLANES, SUBLANES)
        moe = jnp.concatenate(
            [acc[pl.ds(r0 + s, tc, stride=SUBLANES), :] for s in range(o_ref.shape[2] // LANES)], axis=-1)
        o_ref[0] = _rms(h2_ref[0] + moe, g_ref[...])


def _combine(idx, y, h2, g, n_body):
    b, n_e, cp = idx.shape
    _, l, d = h2.shape
    n_out = n_body // OUT_CHUNK
    kernel = functools.partial(_combine_kernel, n_experts=n_e)
    expert = lambda s: jnp.minimum(s, n_e - 1)
    chunk = lambda s: jnp.maximum(s - n_e, 0)
    return pl.pallas_call(
        kernel,
        grid=(b, n_e + n_out),
        in_specs=[
            pl.BlockSpec((1, 1, cp), lambda bi, s: (bi * n_e + expert(s), 0, 0), memory_space=pltpu.SMEM),
            pl.BlockSpec((1, cp * SUBLANES, LANES), lambda bi, s: (expert(s), bi, 0)),
            pl.BlockSpec((1, OUT_CHUNK, d), lambda bi, s: (bi, chunk(s), 0)),
            pl.BlockSpec((1, d), lambda bi, s: (0, 0)),
        ],
        out_specs=pl.BlockSpec((1, OUT_CHUNK, d), lambda bi, s: (bi, chunk(s), 0)),
        out_shape=jax.ShapeDtypeStruct((b, n_body, d), f32),
        scratch_shapes=[pltpu.VMEM(((l + 1) * SUBLANES, LANES), f32)],
        compiler_params=pltpu.CompilerParams(
            dimension_semantics=("arbitrary", "arbitrary"), vmem_limit_bytes=48 * MIB),
        name="combine",
    )(idx.reshape(b * n_e, 1, cp), y, h2, g)


def _round_up(n, k):
    return -(-n // k) * k


def kernel(x, meta_tokens, g_mix, w_in, lam_vecs, subln_gain, pool_w, pool_scale, w_o, g_ffn, router_w,
           w_gate, w_up, w_down, g_final):
    b, n_body, d = x.shape
    l = n_body + N_META
    qk = HEADS * D_QK
    attn_width = HEADS * D_V
    depth = g_mix.shape[0]
    cap = CAPACITY_FACTOR * l // N_EXPERTS
    cp = _round_up(cap, BF16_ROWS)

    assert depth == 1, "one layer: the combine kernel emits only the rows that are returned"
    meta = meta_tokens.astype(x.dtype)
    slopes = jnp.broadcast_to(
        jnp.array([_alibi_slope(i) * LOG2E for i in range(HEADS)], f32)[:, None, None], (HEADS, 1, LANES))

    def per_head_pairs(cols):
        return cols.reshape(d, 2, HEADS, D_QK).transpose(0, 2, 1, 3).reshape(d, 2 * qk)

    out = None
    for layer in range(depth):
        lam_init = 0.8 - 0.6 * math.exp(-0.3 * layer)
        w = w_in[layer]
        w_q = per_head_pairs(w[:, :2 * qk]) * (LOG2E / math.sqrt(D_QK))
        w_k = per_head_pairs(w[:, 2 * qk:4 * qk])
        w_v = w[:, 4 * qk:4 * qk + attn_width]
        w_u = w[:, 4 * qk + attn_width:]
        w_qv_t = jnp.concatenate([w_q, w_v], axis=1).T.astype(bf16)
        w_ku = jnp.concatenate([w_k, w_u], axis=1).astype(bf16)
        qvt, k, u = _norm_inproj(x, meta, g_mix[layer][None, :], w_qv_t, w_ku, 2 * qk)
        a = _diff_attention(qvt, k, lam_vecs[layer].astype(f32), subln_gain[layer][None, :],
                            slopes, n_body, lam_init)
        p = _multiscale_pool(u, pool_w[layer].astype(bf16), pool_scale[layer][None, :], n_body)
        wo = w_o[layer].astype(bf16)
        h2, hn_tiles, aff = _outproj_router(
            x, meta, a, p, wo[:attn_width], wo[attn_width:], g_ffn[layer][None, :],
            jnp.pad(router_w[layer], ((0, 0), (0, LANES - N_EXPERTS))).astype(bf16))

        lp = _round_up(l, LANES)
        aff_pos = jnp.concatenate(
            [aff[:, n_body:, :N_EXPERTS], aff[:, :n_body, :N_EXPERTS], jnp.full((b, lp - l, N_EXPERTS), -1.0, f32)],
            axis=1)
        idx = _topk_rows(jnp.swapaxes(aff_pos, 1, 2), cap, cp, n_body)
        idx = jnp.swapaxes(idx[:, :, :N_EXPERTS], 1, 2)
        real = jnp.arange(cp)[None, None, :] < cap
        idx_gather = jnp.where(real, idx, 0)
        idx_scatter = jnp.where(real, idx, l)

        xg, gate = _gather_tokens(idx_gather, hn_tiles, aff, d)
        y = _expert_ffn(xg, gate, w_gate[layer], w_up[layer], w_down[layer], cp)
        out = _combine(idx_scatter, y, h2, g_final[None, :], n_body)
    return out
```

```python
import functools
import math

import jax
import jax.numpy as jnp
from jax import lax
from jax.experimental import pallas as pl
from jax.experimental.pallas import tpu as pltpu

N_META = 16
HEADS = 4
D_QK = 64
D_V = 128
POOL_WINDOWS = (2, 4, 8, 16)
POOL_HALO = max(POOL_WINDOWS) // 2
N_EXPERTS = 16
CAPACITY_FACTOR = 2
EPS = 1e-6
LOG2E = 1.4426950408889634


def _alibi_slope(head):
    return 2.0 ** (-8.0 * (head + 1) / HEADS)


LANES = 128
SUBLANES = 8
BF16_ROWS = 16

ROW_TILE = 512
ATTN_TQ = 256
ATTN_TK = 512
POOL_CHUNK = 256
FF_TILE = 256
OUT_CHUNK = 1024
MIB = 1024 * 1024

f32 = jnp.float32
bf16 = jnp.bfloat16


def _rms(x, g):
    return x * lax.rsqrt(jnp.mean(x * x, axis=-1, keepdims=True) + EPS) * g


def _tile_rows(x_ref, meta_ref):
    x = x_ref[0]
    meta = jnp.concatenate(
        [meta_ref[...], jnp.zeros((x.shape[0] - meta_ref.shape[0], x.shape[1]), x.dtype)], axis=0)
    return jnp.where(pl.program_id(1) == pl.num_programs(1) - 1, meta, x)


def _norm_inproj_kernel(x_ref, meta_ref, g_ref, wt_ref, w_ref, qvt_ref, k_ref, u_ref):
    n = _rms(_tile_rows(x_ref, meta_ref), g_ref[...]).astype(bf16)
    qvt = lax.dot_general(wt_ref[...], n, (((1,), (1,)), ((), ())), preferred_element_type=f32)
    qvt_ref[0, 0] = qvt.astype(bf16)
    ku = jnp.dot(n, w_ref[...], preferred_element_type=f32)
    n_k = k_ref.shape[-1]
    k_ref[0] = ku[:, :n_k].astype(bf16)
    u_ref[0] = ku[:, n_k:]


def _norm_inproj(x, meta, g, w_qv_t, w_ku, n_k):
    b, n_body, d = x.shape
    assert n_body % ATTN_TK == 0 and meta.shape[0] <= ATTN_TK
    l = n_body + meta.shape[0]
    n_t = n_body // ATTN_TK + 1
    n_qv = w_qv_t.shape[0]
    n_u = w_ku.shape[1] - n_k
    return pl.pallas_call(
        _norm_inproj_kernel,
        grid=(b, n_t),
        in_specs=[
            pl.BlockSpec((1, ATTN_TK, d), lambda bi, i: (bi, jnp.minimum(i, n_t - 2), 0)),
            pl.BlockSpec(meta.shape, lambda bi, i: (0, 0)),
            pl.BlockSpec((1, d), lambda bi, i: (0, 0)),
            pl.BlockSpec(w_qv_t.shape, lambda bi, i: (0, 0)),
            pl.BlockSpec(w_ku.shape, lambda bi, i: (0, 0)),
        ],
        out_specs=[
            pl.BlockSpec((1, 1, n_qv, ATTN_TK), lambda bi, i: (bi, i, 0, 0)),
            pl.BlockSpec((1, ATTN_TK, n_k), lambda bi, i: (bi, i, 0)),
            pl.BlockSpec((1, ATTN_TK, n_u), lambda bi, i: (bi, i, 0)),
        ],
        out_shape=[
            jax.ShapeDtypeStruct((b, n_t, n_qv, ATTN_TK), bf16),
            jax.ShapeDtypeStruct((b, l, n_k), bf16),
            jax.ShapeDtypeStruct((b, l, n_u), f32),
        ],
        compiler_params=pltpu.CompilerParams(
            dimension_semantics=("arbitrary", "arbitrary"), vmem_limit_bytes=40 * MIB),
        name="norm_inproj",
    )(x, meta, g, w_qv_t, w_ku)


def _attn_kernel(lam_ref, gain_ref, slope_ref, q_ref, qe_ref, k_ref, ke_ref, vt_ref, o_ref,
                 qs_scr, bias_scr, tail_scr, vt_scr, *scr, n_body, lam_init):
    tq = ATTN_TQ
    tk = ATTN_TK
    n_diag = tk // tq
    n_sub = q_ref.shape[3] // tq
    i = pl.program_id(2)
    n_q = pl.num_programs(2)
    is_meta = i == n_q - 1
    slope = slope_ref[0][:, :1]

    @pl.when(i == 0)
    def _():
        d = (lax.broadcasted_iota(jnp.int32, (tk, tq), 0)
             - lax.broadcasted_iota(jnp.int32, (tk, tq), 1)).astype(f32)
        for v in range(n_diag):
            bias_scr[v] = -slope * jnp.abs(d - float(v * tq))
        tail_scr[0] = slope * d[0:N_META]
        tail_scr[1] = -slope * jnp.abs(d[0:N_META])
        ones_row = jnp.where(lax.broadcasted_iota(jnp.int32, (BF16_ROWS, tk), 0) == 0, 1.0, 0.0).astype(bf16)
        for c in range(vt_ref.shape[1]):
            vt_scr[c, 0:D_V, :] = vt_ref[0, c]
            vt_scr[c, D_V:, :] = ones_row
        width = qs_scr.shape[2]
        qs_scr[:, 2 * D_QK + BF16_ROWS:, :] = jnp.zeros((3, 2 * LANES - 2 * D_QK - BF16_ROWS, width), bf16)
        qs_scr[2, 2 * D_QK:2 * D_QK + BF16_ROWS, :] = jnp.zeros((BF16_ROWS, width), bf16)

    zero = jnp.zeros((D_QK, tq), bf16)
    for sub in range(n_sub):
        q = q_ref[0, 0, :, sub * tq:(sub + 1) * tq]
        cols = slice(2 * sub * tq, 2 * (sub + 1) * tq)
        top = jnp.concatenate([q[0:D_QK], zero], axis=1)
        bottom = jnp.concatenate([zero, q[D_QK:]], axis=1)
        for v in range(3):
            qs_scr[v, 0:D_QK, cols] = top
            qs_scr[v, D_QK:2 * D_QK, cols] = bottom
        for v in range(2):
            ext = qe_ref[0, v, :, sub * tq:(sub + 1) * tq]
            qs_scr[v, 2 * D_QK:2 * D_QK + BF16_ROWS, cols] = jnp.concatenate([ext, ext], axis=1)

    n_streams = 2 * n_sub
    per_stream = len(scr) // n_streams
    streams = [scr[st * per_stream:(st + 1) * per_stream] for st in range(n_streams)]
    for stream in streams:
        stream[6][...] = jnp.full(stream[6].shape, -jnp.inf, f32)
        stream[7][...] = jnp.zeros(stream[7].shape, f32)

    q_pos0 = jnp.where(is_meta, 0, N_META + i * n_sub * tq)
    q_valid = jnp.where(is_meta, N_META, n_sub * tq)
    n_slab = tq // LANES
    piece = BF16_ROWS

    s_at, p_at, a_at = (0, 1, 8), (2, 3, 9), (4, 5, 10)

    def issue_scores(k, version, slot):
        n = k.shape[0]
        for st, stream in enumerate(streams):
            stream[s_at[slot]][0:n, :] = jnp.dot(k, qs_scr[version, 0:k.shape[1], st * tq:(st + 1) * tq],
                                                 preferred_element_type=f32)

    def softmax(slot, n, bias_tile=None, consts=None):
        def scores(s_scr, r, slab):
            t = s_scr[r:r + piece, slab * LANES:(slab + 1) * LANES]
            return t if bias_tile is None else t + bias_tile(r, slab)

        stats = []
        for st, stream in enumerate(streams):
            s_scr, m_scr = stream[s_at[slot]], stream[6]
            const = None if consts is None else consts[st // 2]
            m_old = m_scr[...]
            maxes = []
            for slab in range(n_slab):
                mx = scores(s_scr, 0, slab)
                for r in range(piece, n, piece):
                    mx = jnp.maximum(mx, scores(s_scr, r, slab))
                maxes.append(jnp.max(mx, axis=0, keepdims=True))
            m_loc = jnp.concatenate(maxes, axis=1)
            m_new = jnp.maximum(m_old, m_loc if const is None else m_loc + const)
            stats.append((m_new, m_new if const is None else m_new - const, jnp.exp2(m_old - m_new)))

        for stream, (m_new, shift, alpha) in zip(streams, stats):
            s_scr, p_scr, a_scr, m_scr = stream[s_at[slot]], stream[p_at[slot]], stream[a_at[slot]], stream[6]
            for slab in range(n_slab):
                cols = slice(slab * LANES, (slab + 1) * LANES)
                sub_b = jnp.broadcast_to(shift[:, cols], (piece, LANES))
                for r in range(0, n, piece):
                    p_scr[r:r + piece, cols] = jnp.exp2(scores(s_scr, r, slab) - sub_b).astype(bf16)
            a_scr[...] = alpha
            m_scr[...] = m_new

    def apply_values(vt, slot):
        n = vt.shape[1]
        for stream in streams:
            p_scr, a_scr, acc_scr = stream[p_at[slot]], stream[a_at[slot]], stream[7]
            acc_scr[...] = a_scr[...] * acc_scr[...] + jnp.dot(vt, p_scr[0:n, :], preferred_element_type=f32)

    def keys_with_positions(c):
        rows = pl.ds(pl.multiple_of(c * tk, tk), tk)
        return jnp.concatenate([k_ref[0, rows, :], ke_ref[0, rows, :]], axis=1)

    n_chunks = n_body // tk

    def issue_chunk_scores(c, slot):
        k_pos0 = N_META + c * tk
        version = jnp.where(k_pos0 + tk <= q_pos0, 0, jnp.where(k_pos0 >= q_pos0 + q_valid, 1, 2))
        issue_scores(keys_with_positions(c), version, slot)

    def add_straddle_bias(c, slot):
        k_pos0 = N_META + c * tk

        @pl.when(jnp.logical_and(k_pos0 + tk > q_pos0, k_pos0 < q_pos0 + q_valid))
        def _():
            for st, stream in enumerate(streams):
                variant = (q_pos0 - k_pos0) // tq + st // 2
                for slab in range(n_slab):
                    cols = slice(slab * LANES, (slab + 1) * LANES)
                    for r in range(0, tk, piece):
                        stream[slot][r:r + piece, cols] = (stream[slot][r:r + piece, cols]
                                                           + bias_scr[variant, r:r + piece, cols])

    def step(t, slot, issue_next, apply_prev, meta_keys=None):
        add_straddle_bias(t, slot)
        if issue_next:
            issue_chunk_scores(t + 1, 1 - slot)
        if meta_keys is not None:
            softmax(2, N_META, *meta_keys)
        softmax(slot, tk)
        if apply_prev:
            apply_values(vt_scr[t - 1], 1 - slot)
        if meta_keys is not None:
            apply_values(vt_scr[n_chunks, :, 0:N_META], 2)

    def run():
        def pair(u, carry):
            step(2 * u, 0, True, True)
            step(2 * u + 1, 1, True, True)
            return carry

        tail = jnp.where(is_meta, 1, 0)
        consts = [-slope * jnp.full((1, 1), q_pos0 + sub * tq, jnp.int32).astype(f32) for sub in range(n_sub)]
        meta_keys = (lambda r, slab: tail_scr[tail, r:r + piece, slab * LANES:(slab + 1) * LANES], consts)
        issue_scores(k_ref[0, n_body:n_body + N_META, :], 2, 2)
        issue_chunk_scores(0, 0)
        step(0, 0, True, False, meta_keys)
        step(1, 1, True, True)
        lax.fori_loop(1, n_chunks // 2 - 1, pair, 0)
        step(n_chunks - 2, 0, True, True)
        step(n_chunks - 1, 1, False, True)
        apply_values(vt_scr[n_chunks - 1], 1)

        lv = lam_ref[...]
        lam = (jnp.exp(jnp.sum(lv[0:1] * lv[1:2], axis=-1, keepdims=True))
               - jnp.exp(jnp.sum(lv[2:3] * lv[3:4], axis=-1, keepdims=True)) + lam_init)
        for sub in range(n_sub):
            o1, o2 = [stream[7][0:D_V, :] / stream[7][D_V:D_V + 1, :] for stream in streams[2 * sub:2 * sub + 2]]
            att = (o1 - lam * o2).T
            o_ref[0, sub * tq:(sub + 1) * tq, :] = (_rms(att, gain_ref[...]) * (1.0 - lam_init)).astype(bf16)

    pl.when(jnp.logical_not(is_meta))(lambda: run())
    streams, n_sub = streams[:2], 1
    pl.when(is_meta)(lambda: run())


def _alibi_tables(n_body, n_q_tiles):
    l = n_body + N_META
    pos = jnp.concatenate([jnp.arange(n_body) + N_META, jnp.arange(N_META)]).astype(f32)
    q_pos = jnp.pad(pos, (0, n_q_tiles * ATTN_TQ - l))
    k_hi, k_lo = jnp.floor(pos / 64), jnp.mod(pos, 64)
    q_hi, q_lo = jnp.floor(q_pos / 64), jnp.mod(q_pos, 64)
    ke, qe = [], []
    for h in range(HEADS):
        c = jnp.float32(_alibi_slope(h) * LOG2E)
        c1 = c.astype(bf16).astype(f32)
        c2 = (c - c1).astype(bf16).astype(f32)
        c3 = (c - c1 - c2).astype(bf16).astype(f32)
        pieces = [c1, c2, c3]
        cols = [k_hi] * 3 + [k_lo] * 3 + [jnp.full_like(pos, p) for p in pieces] * 2
        ke.append(jnp.pad(jnp.stack(cols, axis=1), ((0, 0), (0, LANES - len(cols)))))
        rows = ([jnp.full_like(q_pos, 64 * p) for p in pieces] + [jnp.full_like(q_pos, p) for p in pieces]
                + [-64 * q_hi] * 3 + [-q_lo] * 3)
        ext = jnp.pad(jnp.stack(rows), ((0, BF16_ROWS - len(rows)), (0, 0)))
        qe.append(jnp.stack([ext, -ext]))
    return jnp.stack(ke).astype(bf16), jnp.stack(qe).astype(bf16)


def _diff_attention(qvt, k, lam_vecs, gain, slopes, n_body, lam_init):
    b, l, _ = k.shape
    n_t = qvt.shape[1]
    per_chunk = ATTN_TK // ATTN_TQ
    n_q = pl.cdiv(l, ATTN_TK)
    assert ATTN_TK % ATTN_TQ == 0 and n_body % (2 * ATTN_TK) == 0 and n_body >= 4 * ATTN_TK
    ke, qe = _alibi_tables(n_body, n_q * per_chunk)
    kernel = functools.partial(_attn_kernel, n_body=n_body, lam_init=lam_init)
    return pl.pallas_call(
        kernel,
        grid=(b, HEADS, n_q),
        in_specs=[
            pl.BlockSpec(lam_vecs.shape, lambda bi, h, i: (0, 0)),
            pl.BlockSpec((1, D_V), lambda bi, h, i: (0, 0)),
            pl.BlockSpec((1, 1, LANES), lambda bi, h, i: (h, 0, 0)),
            pl.BlockSpec((1, 1, 2 * D_QK, ATTN_TK), lambda bi, h, i: (bi, i, h, 0)),
            pl.BlockSpec((1, 2, BF16_ROWS, ATTN_TK), lambda bi, h, i: (h, 0, 0, i)),
            pl.BlockSpec((1, l, 2 * D_QK), lambda bi, h, i: (bi, 0, h)),
            pl.BlockSpec((1, l, LANES), lambda bi, h, i: (h, 0, 0)),
            pl.BlockSpec((1, n_t, D_V, ATTN_TK), lambda bi, h, i: (bi, 0, HEADS + h, 0)),
        ],
        out_specs=pl.BlockSpec((1, ATTN_TK, D_V), lambda bi, h, i: (bi, i, h)),
        out_shape=jax.ShapeDtypeStruct((b, l, HEADS * D_V), bf16),
        scratch_shapes=[
            pltpu.VMEM((3, 2 * LANES, 2 * ATTN_TK), bf16),
            pltpu.VMEM((per_chunk, ATTN_TK, ATTN_TQ), f32),
            pltpu.VMEM((2, N_META, ATTN_TQ), f32),
            pltpu.VMEM((n_t, D_V + BF16_ROWS, ATTN_TK), bf16),
        ] + 2 * per_chunk * [
            pltpu.VMEM((ATTN_TK, ATTN_TQ), f32), pltpu.VMEM((ATTN_TK, ATTN_TQ), f32),
            pltpu.VMEM((ATTN_TK, ATTN_TQ), bf16), pltpu.VMEM((ATTN_TK, ATTN_TQ), bf16),
            pltpu.VMEM((1, ATTN_TQ), f32), pltpu.VMEM((1, ATTN_TQ), f32),
            pltpu.VMEM((1, ATTN_TQ), f32),
            pltpu.VMEM((D_V + BF16_ROWS, ATTN_TQ), f32),
            pltpu.VMEM((N_META, ATTN_TQ), f32), pltpu.VMEM((N_META, ATTN_TQ), bf16), pltpu.VMEM((1, ATTN_TQ), f32),
        ],
        compiler_params=pltpu.CompilerParams(
            dimension_semantics=("arbitrary", "arbitrary", "arbitrary"), vmem_limit_bytes=40 * MIB),
        name="diff_attention",
    )(lam_vecs, gain, slopes, qvt, qe, k, ke, qvt)


def _pool_kernel(u_ref, pw_ref, ps_ref, o_ref, upad, *, n_body):
    l = n_body + N_META
    width = u_ref.shape[-1]
    upad[0:POOL_HALO, :] = jnp.zeros((POOL_HALO, width), f32)
    upad[POOL_HALO:POOL_HALO + N_META, :] = u_ref[0, n_body:l, :]
    upad[POOL_HALO + N_META:POOL_HALO + l, :] = u_ref[0, 0:n_body, :]
    upad[POOL_HALO + l:2 * POOL_HALO + l, :] = jnp.zeros((POOL_HALO, width), f32)

    def rows(pos0, n, out_row0):
        slab = upad[pl.ds(pos0, n + 2 * POOL_HALO), :]
        t = pos0 + lax.broadcasted_iota(jnp.int32, (n, 1), 0)
        for gi, w in enumerate(POOL_WINDOWS):
            cols = slice(gi * LANES, (gi + 1) * LANES)
            total = slab[POOL_HALO - w // 2:POOL_HALO - w // 2 + n, cols]
            for j in range(1, w):
                total = total + slab[POOL_HALO - w // 2 + j:POOL_HALO - w // 2 + j + n, cols]
            cnt = (jnp.minimum(t + w // 2, l) - jnp.maximum(t - w // 2, 0)).astype(f32)
            pooled = total / cnt - slab[POOL_HALO:POOL_HALO + n, cols]
            y = jnp.dot(pooled.astype(bf16), pw_ref[gi], preferred_element_type=f32) * ps_ref[:, cols]
            o_ref[0, pl.ds(out_row0, n), cols] = y.astype(bf16)

    rows(0, N_META, n_body)

    def body(c, carry):
        r0 = pl.multiple_of(c * POOL_CHUNK, POOL_CHUNK)
        rows(N_META + r0, POOL_CHUNK, r0)
        return carry

    lax.fori_loop(0, n_body // POOL_CHUNK, body, 0)


def _multiscale_pool(u, pool_w, pool_scale, n_body):
    b, l, width = u.shape
    kernel = functools.partial(_pool_kernel, n_body=n_body)
    return pl.pallas_call(
        kernel,
        grid=(b,),
        in_specs=[
            pl.BlockSpec((1, l, width), lambda bi: (bi, 0, 0)),
            pl.BlockSpec(pool_w.shape, lambda bi: (0, 0, 0)),
            pl.BlockSpec((1, width), lambda bi: (0, 0)),
        ],
        out_specs=pl.BlockSpec((1, l, width), lambda bi: (bi, 0, 0)),
        out_shape=jax.ShapeDtypeStruct((b, l, width), bf16),
        scratch_shapes=[pltpu.VMEM((l + 2 * POOL_HALO, width), f32)],
        compiler_params=pltpu.CompilerParams(
            dimension_semantics=("arbitrary",), vmem_limit_bytes=48 * MIB),
        name="multiscale_pool",
    )(u, pool_w, pool_scale)


def _outproj_router_kernel(x_ref, meta_ref, a_ref, p_ref, woa_ref, wop_ref, g_ref, rw_ref,
                           h2_ref, hn_ref, aff_ref):
    tm = x_ref.shape[1]
    h2 = (_tile_rows(x_ref, meta_ref)
          + jnp.dot(a_ref[0], woa_ref[...], preferred_element_type=f32)
          + jnp.dot(p_ref[0], wop_ref[...], preferred_element_type=f32))
    h2_ref[0] = h2
    hn = _rms(h2, g_ref[...])
    logits = jnp.dot(hn.astype(bf16), rw_ref[...], preferred_element_type=f32)
    lane = lax.broadcasted_iota(jnp.int32, logits.shape, 1)
    logits = jnp.where(lane < N_EXPERTS, logits, -jnp.inf)
    z = jnp.exp(logits - jnp.max(logits, axis=-1, keepdims=True))
    aff_ref[0] = z / jnp.sum(z, axis=-1, keepdims=True)
    for s in range(hn.shape[1] // LANES):
        hn_ref[0, pl.ds(s, tm, stride=SUBLANES), :] = hn[:, s * LANES:(s + 1) * LANES]


def _outproj_router(x, meta, a, p, wo_a, wo_p, g, rw):
    b, n_body, d = x.shape
    assert n_body % ROW_TILE == 0 and meta.shape[0] <= ROW_TILE
    l = n_body + meta.shape[0]
    n_t = n_body // ROW_TILE + 1
    half = a.shape[2]
    n_e = rw.shape[1]
    row = lambda bi, i: (bi, i, 0)
    fixed = lambda bi, i: (0, 0)
    return pl.pallas_call(
        _outproj_router_kernel,
        grid=(b, n_t),
        in_specs=[
            pl.BlockSpec((1, ROW_TILE, d), lambda bi, i: (bi, jnp.minimum(i, n_t - 2), 0)),
            pl.BlockSpec(meta.shape, fixed),
            pl.BlockSpec((1, ROW_TILE, half), row),
            pl.BlockSpec((1, ROW_TILE, half), row),
            pl.BlockSpec((half, d), fixed),
            pl.BlockSpec((half, d), fixed),
            pl.BlockSpec((1, d), fixed),
            pl.BlockSpec((d, n_e), fixed),
        ],
        out_specs=[
            pl.BlockSpec((1, ROW_TILE, d), row),
            pl.BlockSpec((1, ROW_TILE * SUBLANES, LANES), row),
            pl.BlockSpec((1, ROW_TILE, n_e), row),
        ],
        out_shape=[
            jax.ShapeDtypeStruct((b, l, d), f32),
            jax.ShapeDtypeStruct((b, l * SUBLANES, LANES), f32),
            jax.ShapeDtypeStruct((b, l, n_e), f32),
        ],
        compiler_params=pltpu.CompilerParams(
            dimension_semantics=("arbitrary", "arbitrary"), vmem_limit_bytes=40 * MIB),
        name="outproj_router",
    )(x, meta, a, p, wo_a, wo_p, g, rw)


def _prefix_count(flags, w_scan):
    out = []
    offset = jnp.zeros((flags.shape[0], LANES), f32)
    for blk in range(flags.shape[1] // LANES):
        r = jnp.dot(flags[:, blk * LANES:(blk + 1) * LANES].astype(bf16), w_scan, preferred_element_type=f32)
        out.append(r[:, :LANES] + offset)
        offset = offset + r[:, LANES:]
    return jnp.concatenate(out, axis=1)


def _topk_kernel(aff_ref, idx_ref, rank_scr, chosen_scr, *, cap, n_body):
    n_e, lp = aff_ref.shape[1], aff_ref.shape[2]
    cp = idx_ref.shape[1]
    n_blk = lp // LANES
    bits = lax.bitcast_convert_type(aff_ref[0], jnp.int32)

    def refine(step, thr):
        cand = thr | (1 << (30 - step))
        cnt = jnp.sum(jnp.where(bits >= cand, 1, 0), axis=1, keepdims=True)
        return jnp.where(cnt >= cap, cand, thr)

    thr = lax.fori_loop(0, 31, refine, jnp.zeros((n_e, 1), jnp.int32))
    aff = aff_ref[0]
    pivot = jnp.max(jnp.where(bits == thr, aff, -1.0), axis=1, keepdims=True)
    above = aff > pivot
    tie = aff == pivot
    need = (cap - jnp.sum(jnp.where(above, 1, 0), axis=1, keepdims=True)).astype(f32)

    row = lax.broadcasted_iota(jnp.int32, (LANES, 2 * LANES), 0)
    col = lax.broadcasted_iota(jnp.int32, (LANES, 2 * LANES), 1)
    w_scan = jnp.where((row <= col) | (col >= LANES), 1.0, 0.0).astype(bf16)

    tie_rank = _prefix_count(jnp.where(tie, 1.0, 0.0), w_scan)
    chosen_scr[...] = jnp.where(above | (tie & (tie_rank <= need)), 1.0, 0.0)
    pos_f = lax.broadcasted_iota(jnp.int32, (n_e, lp), 1).astype(f32)
    big = float(2 * lp)

    def excess(cnt):
        return jnp.sum(jnp.abs(cnt - float(cap))) > 0.0

    def repair(cnt):
        a = aff_ref[0]
        sel = chosen_scr[...] > 0.5
        lo = jnp.min(jnp.where(sel, a, jnp.inf), axis=1, keepdims=True)
        lo_at = jnp.max(jnp.where(sel & (a == lo), pos_f, -1.0), axis=1, keepdims=True)
        drop = (pos_f == lo_at) & (cnt > float(cap))
        free = (~sel) & (a >= 0.0)
        hi = jnp.max(jnp.where(free, a, -1.0), axis=1, keepdims=True)
        hi_at = jnp.min(jnp.where(free & (a == hi), pos_f, big), axis=1, keepdims=True)
        add = (pos_f == hi_at) & (cnt < float(cap))
        new = jnp.where(add, 1.0, jnp.where(drop, 0.0, chosen_scr[...]))
        chosen_scr[...] = new
        return jnp.sum(new, axis=1, keepdims=True)

    lax.while_loop(excess, repair, jnp.sum(chosen_scr[...], axis=1, keepdims=True))
    rank = _prefix_count(chosen_scr[...], w_scan)
    for e in range(n_e):
        rank_scr[e] = rank[e:e + 1, :]

    lane = lax.broadcasted_iota(jnp.int32, (SUBLANES, LANES), 1)
    sub = lax.broadcasted_iota(jnp.int32, (SUBLANES, LANES), 0)

    n_groups = cp // SUBLANES
    unroll = next(u for u in (11, 8, 6, 5, 4, 3, 2, 1) if n_groups % u == 0)

    def per_expert(e, carry):
        def per_groups(gi, carry2):
            slots = [((gi * unroll + u) * SUBLANES + sub).astype(f32) for u in range(unroll)]
            accs = [jnp.zeros((SUBLANES, LANES), f32) for _ in range(unroll)]
            for t in range(n_blk):
                r_blk = jnp.broadcast_to(rank_scr[e, :, t * LANES:(t + 1) * LANES], (SUBLANES, LANES))
                accs = [a + jnp.where(r_blk <= c, 1.0, 0.0) for a, c in zip(accs, slots)]
            for u in range(unroll):
                pos = jnp.sum(accs[u], axis=1, keepdims=True).astype(jnp.int32)
                rows = jnp.where(pos < N_META, pos + n_body, pos - N_META)
                at = pl.ds(pl.multiple_of((gi * unroll + u) * SUBLANES, SUBLANES), SUBLANES)
                idx_ref[0, at, :] = jnp.where(lane == e, rows, idx_ref[0, at, :])
            return carry2

        return lax.fori_loop(0, n_groups // unroll, per_groups, carry)

    idx_ref[0] = jnp.zeros((cp, LANES), jnp.int32)
    lax.fori_loop(0, n_e, per_expert, 0)


def _topk_rows(aff_pos, cap, cp, n_body):
    b, n_e, lp = aff_pos.shape
    kernel = functools.partial(_topk_kernel, cap=cap, n_body=n_body)
    return pl.pallas_call(
        kernel,
        grid=(b,),
        in_specs=[pl.BlockSpec((1, n_e, lp), lambda bi: (bi, 0, 0))],
        out_specs=pl.BlockSpec((1, cp, LANES), lambda bi: (bi, 0, 0)),
        out_shape=jax.ShapeDtypeStruct((b, cp, LANES), jnp.int32),
        scratch_shapes=[pltpu.VMEM((n_e, 1, lp), f32), pltpu.VMEM((n_e, lp), f32)],
        compiler_params=pltpu.CompilerParams(
            dimension_semantics=("arbitrary",), vmem_limit_bytes=32 * MIB),
        name="topk_rows",
    )(aff_pos)


def _gather_kernel(code_ref, src_ref, aff_ref, o_ref, gate_ref, tile):
    cp = o_ref.shape[1]
    sub = lax.broadcasted_iota(jnp.int32, (SUBLANES, LANES), 0)

    def group(g, carry):
        gates = jnp.zeros((SUBLANES, LANES), f32)
        for j in range(SUBLANES):
            slot = g * SUBLANES + j
            code = code_ref[0, 0, slot]
            tile[pl.ds(pl.multiple_of(slot * SUBLANES, SUBLANES), SUBLANES), :] = (
                src_ref[0, pl.ds(pl.multiple_of(code & -SUBLANES, SUBLANES), SUBLANES), :])
            aff8 = aff_ref[0, pl.ds(pl.multiple_of((code >> 6) << 3, SUBLANES), SUBLANES), :]
            gates = jnp.where(sub == j, pltpu.roll(aff8, code & (SUBLANES - 1), axis=0), gates)
        gate_ref[0, pl.ds(pl.multiple_of(g * SUBLANES, SUBLANES), SUBLANES), :] = gates
        return carry

    lax.fori_loop(0, cp // SUBLANES, group, 0)
    for s in range(o_ref.shape[2] // LANES):
        o_ref[0, :, s * LANES:(s + 1) * LANES] = tile[pl.ds(s, cp, stride=SUBLANES), :].astype(bf16)


def _gather_tokens(idx, hn_tiles, aff, d):
    b, n_e, cp = idx.shape
    rows = hn_tiles.shape[1]
    l = aff.shape[1]
    scalars = pl.BlockSpec((1, 1, cp), lambda bi, e: (bi * n_e + e, 0, 0), memory_space=pltpu.SMEM)
    flat = idx.reshape(b * n_e, 1, cp)
    slot = jnp.arange(cp, dtype=jnp.int32)[None, None, :]
    return pl.pallas_call(
        _gather_kernel,
        grid=(b, n_e),
        in_specs=[
            scalars,
            pl.BlockSpec((1, rows, LANES), lambda bi, e: (bi, 0, 0)),
            pl.BlockSpec((1, l, LANES), lambda bi, e: (bi, 0, 0)),
        ],
        out_specs=[
            pl.BlockSpec((1, cp, d), lambda bi, e: (e, bi, 0)),
            pl.BlockSpec((1, cp, LANES), lambda bi, e: (e, bi, 0)),
        ],
        out_shape=[
            jax.ShapeDtypeStruct((n_e, b * cp, d), bf16),
            jax.ShapeDtypeStruct((n_e, b * cp, LANES), f32),
        ],
        scratch_shapes=[pltpu.VMEM((cp * SUBLANES, LANES), f32)],
        compiler_params=pltpu.CompilerParams(
            dimension_semantics=("arbitrary", "arbitrary"), vmem_limit_bytes=48 * MIB),
        name="gather_tokens",
    )(flat * SUBLANES + (slot - flat) % SUBLANES, hn_tiles, aff)


def _moe_kernel(x_ref, gate_ref, wg_ref, wu_ref, wd_ref, y_ref, wg_b, wu_b, wd_b, acc, *, cp):
    f = pl.program_id(1)
    n_f = pl.num_programs(1)
    wg_b[...] = wg_ref[0].astype(bf16)
    wu_b[...] = wu_ref[0].astype(bf16)
    wd_b[...] = wd_ref[0].astype(bf16)

    @pl.when(f == 0)
    def _():
        acc[...] = jnp.zeros(acc.shape, f32)

    n_chunks = acc.shape[0] // cp
    for c in range(n_chunks):
        rows = slice(c * cp, (c + 1) * cp)
        x = x_ref[0, rows, :]
        g = jnp.dot(x, wg_b[...], preferred_element_type=f32)
        u = jnp.dot(x, wu_b[...], preferred_element_type=f32)
        hid = (g * jax.nn.sigmoid(g) * u).astype(bf16)
        acc[rows, :] += jnp.dot(hid, wd_b[...], preferred_element_type=f32)

    @pl.when(f == n_f - 1)
    def _():
        for c in range(n_chunks):
            rows = slice(c * cp, (c + 1) * cp)
            lane = lax.broadcasted_iota(jnp.int32, (cp, LANES), 1)
            gate = jnp.sum(jnp.where(lane == pl.program_id(0), gate_ref[0, rows, :], 0.0),
                           axis=1, keepdims=True)
            y = acc[rows, :] * gate
            for s in range(y.shape[1] // LANES):
                y_ref[0, pl.ds(c * cp * SUBLANES + s, cp, stride=SUBLANES), :] = (
                    y[:, s * LANES:(s + 1) * LANES])


def _expert_ffn(xg, gate, w_gate, w_up, w_down, cp):
    n_e, m, d = xg.shape
    d_ff = w_gate.shape[2]
    kernel = functools.partial(_moe_kernel, cp=cp)
    return pl.pallas_call(
        kernel,
        grid=(n_e, d_ff // FF_TILE),
        in_specs=[
            pl.BlockSpec((1, m, d), lambda e, f: (e, 0, 0)),
            pl.BlockSpec((1, m, LANES), lambda e, f: (e, 0, 0)),
            pl.BlockSpec((1, d, FF_TILE), lambda e, f: (e, 0, f)),
            pl.BlockSpec((1, d, FF_TILE), lambda e, f: (e, 0, f)),
            pl.BlockSpec((1, FF_TILE, d), lambda e, f: (e, f, 0)),
        ],
        out_specs=pl.BlockSpec((1, m * SUBLANES, LANES), lambda e, f: (e, 0, 0)),
        out_shape=jax.ShapeDtypeStruct((n_e, m * SUBLANES, LANES), f32),
        scratch_shapes=[
            pltpu.VMEM((d, FF_TILE), bf16),
            pltpu.VMEM((d, FF_TILE), bf16),
            pltpu.VMEM((FF_TILE, d), bf16),
            pltpu.VMEM((m, d), f32),
        ],
        compiler_params=pltpu.CompilerParams(
            dimension_semantics=("arbitrary", "arbitrary"), vmem_limit_bytes=56 * MIB),
        name="expert_ffn",
    )(xg, gate, w_gate, w_up, w_down)


def _combine_kernel(idx_ref, y_ref, h2_ref, g_ref, o_ref, acc, *, n_experts):
    step = pl.program_id(1)
    cp = idx_ref.shape[2]
    tc = o_ref.shape[1]

    @pl.when(step == 0)
    def _():
        acc[...] = jnp.zeros(acc.shape, f32)

    @pl.when(step < n_experts)
    def _():
        def group(g, carry):
            dst = []
            val = []
            for j in range(SUBLANES):
                slot = g * SUBLANES + j
                r = pl.multiple_of(idx_ref[0, 0, slot] * SUBLANES, SUBLANES)
                dst.append(r)
                val.append(acc[pl.ds(r, SUBLANES), :]
                           + y_ref[0, pl.ds(pl.multiple_of(slot * SUBLANES, SUBLANES), SUBLANES), :])
            for r, v in zip(dst, val):
                acc[pl.ds(r, SUBLANES), :] = v
            return carry

        lax.fori_loop(0, cp // SUBLANES, group, 0)

    @pl.when(step >= n_experts)
    def _():
        r0 = pl.multiple_of((step - n_experts) * tc * SUBLANES, SUBLANES)
        moe = jnp.concatenate(
            [acc[pl.ds(r0 + s, tc, stride=SUBLANES), :] for s in range(o_ref.shape[2] // LANES)], axis=-1)
        o_ref[0] = _rms(h2_ref[0] + moe, g_ref[...])


def _combine(idx, y, h2, g, n_body):
    b, n_e, cp = idx.shape
    _, l, d = h2.shape
    n_out = n_body // OUT_CHUNK
    kernel = functools.partial(_combine_kernel, n_experts=n_e)
    expert = lambda s: jnp.minimum(s, n_e - 1)
    chunk = lambda s: jnp.maximum(s - n_e, 0)
    return pl.pallas_call(
        kernel,
        grid=(b, n_e + n_out),
        in_specs=[
            pl.BlockSpec((1, 1, cp), lambda bi, s: (bi * n_e + expert(s), 0, 0), memory_space=pltpu.SMEM),
            pl.BlockSpec((1, cp * SUBLANES, LANES), lambda bi, s: (expert(s), bi, 0)),
            pl.BlockSpec((1, OUT_CHUNK, d), lambda bi, s: (bi, chunk(s), 0)),
            pl.BlockSpec((1, d), lambda bi, s: (0, 0)),
        ],
        out_specs=pl.BlockSpec((1, OUT_CHUNK, d), lambda bi, s: (bi, chunk(s), 0)),
        out_shape=jax.ShapeDtypeStruct((b, n_body, d), f32),
        scratch_shapes=[pltpu.VMEM(((l + 1) * SUBLANES, LANES), f32)],
        compiler_params=pltpu.CompilerParams(
            dimension_semantics=("arbitrary", "arbitrary"), vmem_limit_bytes=48 * MIB),
        name="combine",
    )(idx.reshape(b * n_e, 1, cp), y, h2, g)


def _round_up(n, k):
    return -(-n // k) * k


def kernel(x, meta_tokens, g_mix, w_in, lam_vecs, subln_gain, pool_w, pool_scale, w_o, g_ffn, router_w,
           w_gate, w_up, w_down, g_final):
    b, n_body, d = x.shape
    l = n_body + N_META
    qk = HEADS * D_QK
    attn_width = HEADS * D_V
    depth = g_mix.shape[0]
    cap = CAPACITY_FACTOR * l // N_EXPERTS
    cp = _round_up(cap, BF16_ROWS)

    assert depth == 1, "one layer: the combine kernel emits only the rows that are returned"
    meta = meta_tokens.astype(x.dtype)
    slopes = jnp.broadcast_to(
        jnp.array([_alibi_slope(i) * LOG2E for i in range(HEADS)], f32)[:, None, None], (HEADS, 1, LANES))

    def per_head_pairs(cols):
        return cols.reshape(d, 2, HEADS, D_QK).transpose(0, 2, 1, 3).reshape(d, 2 * qk)

    out = None
    for layer in range(depth):
        lam_init = 0.8 - 0.6 * math.exp(-0.3 * layer)
        w = w_in[layer]
        w_q = per_head_pairs(w[:, :2 * qk]) * (LOG2E / math.sqrt(D_QK))
        w_k = per_head_pairs(w[:, 2 * qk:4 * qk])
        w_v = w[:, 4 * qk:4 * qk + attn_width]
        w_u = w[:, 4 * qk + attn_width:]
        w_qv_t = jnp.concatenate([w_q, w_v], axis=1).T.astype(bf16)
        w_ku = jnp.concatenate([w_k, w_u], axis=1).astype(bf16)
        qvt, k, u = _norm_inproj(x, meta, g_mix[layer][None, :], w_qv_t, w_ku, 2 * qk)
        a = _diff_attention(qvt, k, lam_vecs[layer].astype(f32), subln_gain[layer][None, :],
                            slopes, n_body, lam_init)
        p = _multiscale_pool(u, pool_w[layer].astype(bf16), pool_scale[layer][None, :], n_body)
        wo = w_o[layer].astype(bf16)
        h2, hn_tiles, aff = _outproj_router(
            x, meta, a, p, wo[:attn_width], wo[attn_width:], g_ffn[layer][None, :],
            jnp.pad(router_w[layer], ((0, 0), (0, LANES - N_EXPERTS))).astype(bf16))

        lp = _round_up(l, LANES)
        aff_pos = jnp.concatenate(
            [aff[:, n_body:, :N_EXPERTS], aff[:, :n_body, :N_EXPERTS], jnp.full((b, lp - l, N_EXPERTS), -1.0, f32)],
            axis=1)
        idx = _topk_rows(jnp.swapaxes(aff_pos, 1, 2), cap, cp, n_body)
        idx = jnp.swapaxes(idx[:, :, :N_EXPERTS], 1, 2)
        real = jnp.arange(cp)[None, None, :] < cap
        idx_gather = jnp.where(real, idx, 0)
        idx_scatter = jnp.where(real, idx, l)

        xg, gate = _gather_tokens(idx_gather, hn_tiles, aff, d)
        y = _expert_ffn(xg, gate, w_gate[layer], w_up[layer], w_down[layer], cp)
        out = _combine(idx_scatter, y, h2, g_final[None, :], n_body)
    return out
```

```python
import functools
import math

import jax
import jax.numpy as jnp
from jax import lax
from jax.experimental import pallas as pl
from jax.experimental.pallas import tpu as pltpu

N_META = 16
HEADS = 4
D_QK = 64
D_V = 128
POOL_WINDOWS = (2, 4, 8, 16)
POOL_HALO = max(POOL_WINDOWS) // 2
N_EXPERTS = 16
CAPACITY_FACTOR = 2
EPS = 1e-6
LOG2E = 1.4426950408889634


def _alibi_slope(head):
    return 2.0 ** (-8.0 * (head + 1) / HEADS)


LANES = 128
SUBLANES = 8
BF16_ROWS = 16

ROW_TILE = 512
ATTN_TQ = 256
ATTN_TK = 512
POOL_CHUNK = 256
FF_TILE = 256
OUT_CHUNK = 1024
MIB = 1024 * 1024

f32 = jnp.float32
bf16 = jnp.bfloat16


def _rms(x, g):
    return x * lax.rsqrt(jnp.mean(x * x, axis=-1, keepdims=True) + EPS) * g


def _tile_rows(x_ref, meta_ref):
    x = x_ref[0]
    meta = jnp.concatenate(
        [meta_ref[...], jnp.zeros((x.shape[0] - meta_ref.shape[0], x.shape[1]), x.dtype)], axis=0)
    return jnp.where(pl.program_id(1) == pl.num_programs(1) - 1, meta, x)


def _norm_inproj_kernel(x_ref, meta_ref, g_ref, wt_ref, w_ref, qvt_ref, k_ref, u_ref):
    n = _rms(_tile_rows(x_ref, meta_ref), g_ref[...]).astype(bf16)
    qvt = lax.dot_general(wt_ref[...], n, (((1,), (1,)), ((), ())), preferred_element_type=f32)
    qvt_ref[0, 0] = qvt.astype(bf16)
    ku = jnp.dot(n, w_ref[...], preferred_element_type=f32)
    n_k = k_ref.shape[-1]
    k_ref[0] = ku[:, :n_k].astype(bf16)
    u_ref[0] = ku[:, n_k:]


def _norm_inproj(x, meta, g, w_qv_t, w_ku, n_k):
    b, n_body, d = x.shape
    assert n_body % ATTN_TK == 0 and meta.shape[0] <= ATTN_TK
    l = n_body + meta.shape[0]
    n_t = n_body // ATTN_TK + 1
    n_qv = w_qv_t.shape[0]
    n_u = w_ku.shape[1] - n_k
    return pl.pallas_call(
        _norm_inproj_kernel,
        grid=(b, n_t),
        in_specs=[
            pl.BlockSpec((1, ATTN_TK, d), lambda bi, i: (bi, jnp.minimum(i, n_t - 2), 0)),
            pl.BlockSpec(meta.shape, lambda bi, i: (0, 0)),
            pl.BlockSpec((1, d), lambda bi, i: (0, 0)),
            pl.BlockSpec(w_qv_t.shape, lambda bi, i: (0, 0)),
            pl.BlockSpec(w_ku.shape, lambda bi, i: (0, 0)),
        ],
        out_specs=[
            pl.BlockSpec((1, 1, n_qv, ATTN_TK), lambda bi, i: (bi, i, 0, 0)),
            pl.BlockSpec((1, ATTN_TK, n_k), lambda bi, i: (bi, i, 0)),
            pl.BlockSpec((1, ATTN_TK, n_u), lambda bi, i: (bi, i, 0)),
        ],
        out_shape=[
            jax.ShapeDtypeStruct((b, n_t, n_qv, ATTN_TK), bf16),
            jax.ShapeDtypeStruct((b, l, n_k), bf16),
            jax.ShapeDtypeStruct((b, l, n_u), f32),
        ],
        compiler_params=pltpu.CompilerParams(
            dimension_semantics=("arbitrary", "arbitrary"), vmem_limit_bytes=40 * MIB),
        name="norm_inproj",
    )(x, meta, g, w_qv_t, w_ku)


def _attn_kernel(lam_ref, gain_ref, slope_ref, q_ref, qe_ref, k_ref, ke_ref, vt_ref, o_ref,
                 qs_scr, bias_scr, tail_scr, vt_scr, *scr, n_body, lam_init):
    tq = ATTN_TQ
    tk = ATTN_TK
    n_diag = tk // tq
    n_sub = q_ref.shape[3] // tq
    i = pl.program_id(2)
    n_q = pl.num_programs(2)
    is_meta = i == n_q - 1
    slope = slope_ref[0][:, :1]

    @pl.when(i == 0)
    def _():
        d = (lax.broadcasted_iota(jnp.int32, (tk, tq), 0)
             - lax.broadcasted_iota(jnp.int32, (tk, tq), 1)).astype(f32)
        for v in range(n_diag):
            bias_scr[v] = -slope * jnp.abs(d - float(v * tq))
        tail_scr[0] = slope * d[0:N_META]
        tail_scr[1] = -slope * jnp.abs(d[0:N_META])
        ones_row = jnp.where(lax.broadcasted_iota(jnp.int32, (BF16_ROWS, tk), 0) == 0, 1.0, 0.0).astype(bf16)
        for c in range(vt_ref.shape[1]):
            vt_scr[c, 0:D_V, :] = vt_ref[0, c]
            vt_scr[c, D_V:, :] = ones_row
        width = qs_scr.shape[2]
        qs_scr[:, 2 * D_QK + BF16_ROWS:, :] = jnp.zeros((3, 2 * LANES - 2 * D_QK - BF16_ROWS, width), bf16)
        qs_scr[2, 2 * D_QK:2 * D_QK + BF16_ROWS, :] = jnp.zeros((BF16_ROWS, width), bf16)

    zero = jnp.zeros((D_QK, tq), bf16)
    for sub in range(n_sub):
        q = q_ref[0, 0, :, sub * tq:(sub + 1) * tq]
        cols = slice(2 * sub * tq, 2 * (sub + 1) * tq)
        top = jnp.concatenate([q[0:D_QK], zero], axis=1)
        bottom = jnp.concatenate([zero, q[D_QK:]], axis=1)
        for v in range(3):
            qs_scr[v, 0:D_QK, cols] = top
            qs_scr[v, D_QK:2 * D_QK, cols] = bottom
        for v in range(2):
            ext = qe_ref[0, v, :, sub * tq:(sub + 1) * tq]
            qs_scr[v, 2 * D_QK:2 * D_QK + BF16_ROWS, cols] = jnp.concatenate([ext, ext], axis=1)

    n_streams = 2 * n_sub
    per_stream = len(scr) // n_streams
    streams = [scr[st * per_stream:(st + 1) * per_stream] for st in range(n_streams)]
    for stream in streams:
        stream[6][...] = jnp.full(stream[6].shape, -jnp.inf, f32)
        stream[7][...] = jnp.zeros(stream[7].shape, f32)

    q_pos0 = jnp.where(is_meta, 0, N_META + i * n_sub * tq)
    q_valid = jnp.where(is_meta, N_META, n_sub * tq)
    n_slab = tq // LANES
    piece = BF16_ROWS

    s_at, p_at, a_at = (0, 1, 8), (2, 3, 9), (4, 5, 10)

    def issue_scores(k, version, slot):
        n = k.shape[0]
        for st, stream in enumerate(streams):
            stream[s_at[slot]][0:n, :] = jnp.dot(k, qs_scr[version, 0:k.shape[1], st * tq:(st + 1) * tq],
                                                 preferred_element_type=f32)

    def softmax(slot, n, bias_tile=None, consts=None):
        def scores(s_scr, r, slab):
            t = s_scr[r:r + piece, slab * LANES:(slab + 1) * LANES]
            return t if bias_tile is None else t + bias_tile(r, slab)

        stats = []
        for st, stream in enumerate(streams):
            s_scr, m_scr = stream[s_at[slot]], stream[6]
            const = None if consts is None else consts[st // 2]
            m_old = m_scr[...]
            maxes = []
            for slab in range(n_slab):
                mx = scores(s_scr, 0, slab)
                for r in range(piece, n, piece):
                    mx = jnp.maximum(mx, scores(s_scr, r, slab))
                maxes.append(jnp.max(mx, axis=0, keepdims=True))
            m_loc = jnp.concatenate(maxes, axis=1)
            m_new = jnp.maximum(m_old, m_loc if const is None else m_loc + const)
            stats.append((m_new, m_new if const is None else m_new - const, jnp.exp2(m_old - m_new)))

        for stream, (m_new, shift, alpha) in zip(streams, stats):
            s_scr, p_scr, a_scr, m_scr = stream[s_at[slot]], stream[p_at[slot]], stream[a_at[slot]], stream[6]
            for slab in range(n_slab):
                cols = slice(slab * LANES, (slab + 1) * LANES)
                sub_b = jnp.broadcast_to(shift[:, cols], (piece, LANES))
                for r in range(0, n, piece):
                    p_scr[r:r + piece, cols] = jnp.exp2(scores(s_scr, r, slab) - sub_b).astype(bf16)
            a_scr[...] = alpha
            m_scr[...] = m_new

    def apply_values(vt, slot):
        n = vt.shape[1]
        for stream in streams:
            p_scr, a_scr, acc_scr = stream[p_at[slot]], stream[a_at[slot]], stream[7]
            acc_scr[...] = a_scr[...] * acc_scr[...] + jnp.dot(vt, p_scr[0:n, :], preferred_element_type=f32)

    def keys_with_positions(c):
        rows = pl.ds(pl.multiple_of(c * tk, tk), tk)
        return jnp.concatenate([k_ref[0, rows, :], ke_ref[0, rows, :]], axis=1)

    n_chunks = n_body // tk

    def issue_chunk_scores(c, slot):
        k_pos0 = N_META + c * tk
        version = jnp.where(k_pos0 + tk <= q_pos0, 0, jnp.where(k_pos0 >= q_pos0 + q_valid, 1, 2))
        issue_scores(keys_with_positions(c), version, slot)

    def add_straddle_bias(c, slot):
        k_pos0 = N_META + c * tk

        @pl.when(jnp.logical_and(k_pos0 + tk > q_pos0, k_pos0 < q_pos0 + q_valid))
        def _():
            for st, stream in enumerate(streams):
                variant = (q_pos0 - k_pos0) // tq + st // 2
                for slab in range(n_slab):
                    cols = slice(slab * LANES, (slab + 1) * LANES)
                    for r in range(0, tk, piece):
                        stream[slot][r:r + piece, cols] = (stream[slot][r:r + piece, cols]
                                                           + bias_scr[variant, r:r + piece, cols])

    def step(t, slot, issue_next, apply_prev, meta_keys=None):
        add_straddle_bias(t, slot)
        if issue_next:
            issue_chunk_scores(t + 1, 1 - slot)
        if meta_keys is not None:
            softmax(2, N_META, *meta_keys)
        softmax(slot, tk)
        if apply_prev:
            apply_values(vt_scr[t - 1], 1 - slot)
        if meta_keys is not None:
            apply_values(vt_scr[n_chunks, :, 0:N_META], 2)

    def run():
        def pair(u, carry):
            step(2 * u, 0, True, True)
            step(2 * u + 1, 1, True, True)
            return carry

        tail = jnp.where(is_meta, 1, 0)
        consts = [-slope * jnp.full((1, 1), q_pos0 + sub * tq, jnp.int32).astype(f32) for sub in range(n_sub)]
        meta_keys = (lambda r, slab: tail_scr[tail, r:r + piece, slab * LANES:(slab + 1) * LANES], consts)
        issue_scores(k_ref[0, n_body:n_body + N_META, :], 2, 2)
        issue_chunk_scores(0, 0)
        step(0, 0, True, False, meta_keys)
        step(1, 1, True, True)
        lax.fori_loop(1, n_chunks // 2 - 1, pair, 0)
        step(n_chunks - 2, 0, True, True)
        step(n_chunks - 1, 1, False, True)
        apply_values(vt_scr[n_chunks - 1], 1)

        lv = lam_ref[...]
        lam = (jnp.exp(jnp.sum(lv[0:1] * lv[1:2], axis=-1, keepdims=True))
               - jnp.exp(jnp.sum(lv[2:3] * lv[3:4], axis=-1, keepdims=True)) + lam_init)
        for sub in range(n_sub):
            o1, o2 = [stream[7][0:D_V, :] / stream[7][D_V:D_V + 1, :] for stream in streams[2 * sub:2 * sub + 2]]
            att = (o1 - lam * o2).T
            o_ref[0, sub * tq:(sub + 1) * tq, :] = (_rms(att, gain_ref[...]) * (1.0 - lam_init)).astype(bf16)

    pl.when(jnp.logical_not(is_meta))(lambda: run())
    streams, n_sub = streams[:2], 1
    pl.when(is_meta)(lambda: run())


def _alibi_tables(n_body, n_q_tiles):
    l = n_body + N_META
    pos = jnp.concatenate([jnp.arange(n_body) + N_META, jnp.arange(N_META)]).astype(f32)
    q_pos = jnp.pad(pos, (0, n_q_tiles * ATTN_TQ - l))
    k_hi, k_lo = jnp.floor(pos / 64), jnp.mod(pos, 64)
    q_hi, q_lo = jnp.floor(q_pos / 64), jnp.mod(q_pos, 64)
    ke, qe = [], []
    for h in range(HEADS):
        c = jnp.float32(_alibi_slope(h) * LOG2E)
        c1 = c.astype(bf16).astype(f32)
        c2 = (c - c1).astype(bf16).astype(f32)
        c3 = (c - c1 - c2).astype(bf16).astype(f32)
        pieces = [c1, c2, c3]
        cols = [k_hi] * 3 + [k_lo] * 3 + [jnp.full_like(pos, p) for p in pieces] * 2
        ke.append(jnp.pad(jnp.stack(cols, axis=1), ((0, 0), (0, LANES - len(cols)))))
        rows = ([jnp.full_like(q_pos, 64 * p) for p in pieces] + [jnp.full_like(q_pos, p) for p in pieces]
                + [-64 * q_hi] * 3 + [-q_lo] * 3)
        ext = jnp.pad(jnp.stack(rows), ((0, BF16_ROWS - len(rows)), (0, 0)))
        qe.append(jnp.stack([ext, -ext]))
    return jnp.stack(ke).astype(bf16), jnp.stack(qe).astype(bf16)


def _diff_attention(qvt, k, lam_vecs, gain, slopes, n_body, lam_init):
    b, l, _ = k.shape
    n_t = qvt.shape[1]
    per_chunk = ATTN_TK // ATTN_TQ
    n_q = pl.cdiv(l, ATTN_TK)
    assert ATTN_TK % ATTN_TQ == 0 and n_body % (2 * ATTN_TK) == 0 and n_body >= 4 * ATTN_TK
    ke, qe = _alibi_tables(n_body, n_q * per_chunk)
    kernel = functools.partial(_attn_kernel, n_body=n_body, lam_init=lam_init)
    return pl.pallas_call(
        kernel,
        grid=(b, HEADS, n_q),
        in_specs=[
            pl.BlockSpec(lam_vecs.shape, lambda bi, h, i: (0, 0)),
            pl.BlockSpec((1, D_V), lambda bi, h, i: (0, 0)),
            pl.BlockSpec((1, 1, LANES), lambda bi, h, i: (h, 0, 0)),
            pl.BlockSpec((1, 1, 2 * D_QK, ATTN_TK), lambda bi, h, i: (bi, i, h, 0)),
            pl.BlockSpec((1, 2, BF16_ROWS, ATTN_TK), lambda bi, h, i: (h, 0, 0, i)),
            pl.BlockSpec((1, l, 2 * D_QK), lambda bi, h, i: (bi, 0, h)),
            pl.BlockSpec((1, l, LANES), lambda bi, h, i: (h, 0, 0)),
            pl.BlockSpec((1, n_t, D_V, ATTN_TK), lambda bi, h, i: (bi, 0, HEADS + h, 0)),
        ],
        out_specs=pl.BlockSpec((1, ATTN_TK, D_V), lambda bi, h, i: (bi, i, h)),
        out_shape=jax.ShapeDtypeStruct((b, l, HEADS * D_V), bf16),
        scratch_shapes=[
            pltpu.VMEM((3, 2 * LANES, 2 * ATTN_TK), bf16),
            pltpu.VMEM((per_chunk, ATTN_TK, ATTN_TQ), f32),
            pltpu.VMEM((2, N_META, ATTN_TQ), f32),
            pltpu.VMEM((n_t, D_V + BF16_ROWS, ATTN_TK), bf16),
        ] + 2 * per_chunk * [
            pltpu.VMEM((ATTN_TK, ATTN_TQ), f32), pltpu.VMEM((ATTN_TK, ATTN_TQ), f32),
            pltpu.VMEM((ATTN_TK, ATTN_TQ), bf16), pltpu.VMEM((ATTN_TK, ATTN_TQ), bf16),
            pltpu.VMEM((1, ATTN_TQ), f32), pltpu.VMEM((1, ATTN_TQ), f32),
            pltpu.VMEM((1, ATTN_TQ), f32),
            pltpu.VMEM((D_V + BF16_ROWS, ATTN_TQ), f32),
            pltpu.VMEM((N_META, ATTN_TQ), f32), pltpu.VMEM((N_META, ATTN_TQ), bf16), pltpu.VMEM((1, ATTN_TQ), f32),
        ],
        compiler_params=pltpu.CompilerParams(
            dimension_semantics=("arbitrary", "arbitrary", "arbitrary"), vmem_limit_bytes=40 * MIB),
        name="diff_attention",
    )(lam_vecs, gain, slopes, qvt, qe, k, ke, qvt)


def _pool_kernel(u_ref, pw_ref, ps_ref, o_ref, upad, *, n_body):
    l = n_body + N_META
    width = u_ref.shape[-1]
    upad[0:POOL_HALO, :] = jnp.zeros((POOL_HALO, width), f32)
    upad[POOL_HALO:POOL_HALO + N_META, :] = u_ref[0, n_body:l, :]
    upad[POOL_HALO + N_META:POOL_HALO + l, :] = u_ref[0, 0:n_body, :]
    upad[POOL_HALO + l:2 * POOL_HALO + l, :] = jnp.zeros((POOL_HALO, width), f32)

    def rows(pos0, n, out_row0):
        slab = upad[pl.ds(pos0, n + 2 * POOL_HALO), :]
        t = pos0 + lax.broadcasted_iota(jnp.int32, (n, 1), 0)
        for gi, w in enumerate(POOL_WINDOWS):
            cols = slice(gi * LANES, (gi + 1) * LANES)
            sums, size = slab[:, cols], 1
            while size < w:
                m = sums.shape[0] - size
                sums = sums[0:m] + sums[size:size + m]
                size *= 2
            total = sums[POOL_HALO - w // 2:POOL_HALO - w // 2 + n]
            cnt = (jnp.minimum(t + w // 2, l) - jnp.maximum(t - w // 2, 0)).astype(f32)
            pooled = total / cnt - slab[POOL_HALO:POOL_HALO + n, cols]
            y = jnp.dot(pooled.astype(bf16), pw_ref[gi], preferred_element_type=f32) * ps_ref[:, cols]
            o_ref[0, pl.ds(out_row0, n), cols] = y.astype(bf16)

    rows(0, N_META, n_body)

    def body(c, carry):
        r0 = pl.multiple_of(c * POOL_CHUNK, POOL_CHUNK)
        rows(N_META + r0, POOL_CHUNK, r0)
        return carry

    lax.fori_loop(0, n_body // POOL_CHUNK, body, 0)


def _multiscale_pool(u, pool_w, pool_scale, n_body):
    b, l, width = u.shape
    kernel = functools.partial(_pool_kernel, n_body=n_body)
    return pl.pallas_call(
        kernel,
        grid=(b,),
        in_specs=[
            pl.BlockSpec((1, l, width), lambda bi: (bi, 0, 0)),
            pl.BlockSpec(pool_w.shape, lambda bi: (0, 0, 0)),
            pl.BlockSpec((1, width), lambda bi: (0, 0)),
        ],
        out_specs=pl.BlockSpec((1, l, width), lambda bi: (bi, 0, 0)),
        out_shape=jax.ShapeDtypeStruct((b, l, width), bf16),
        scratch_shapes=[pltpu.VMEM((l + 2 * POOL_HALO, width), f32)],
        compiler_params=pltpu.CompilerParams(
            dimension_semantics=("arbitrary",), vmem_limit_bytes=48 * MIB),
        name="multiscale_pool",
    )(u, pool_w, pool_scale)


def _outproj_router_kernel(x_ref, meta_ref, a_ref, p_ref, woa_ref, wop_ref, g_ref, rw_ref,
                           h2_ref, hn_ref, aff_ref):
    tm = x_ref.shape[1]
    h2 = (_tile_rows(x_ref, meta_ref)
          + jnp.dot(a_ref[0], woa_ref[...], preferred_element_type=f32)
          + jnp.dot(p_ref[0], wop_ref[...], preferred_element_type=f32))
    h2_ref[0] = h2
    hn = _rms(h2, g_ref[...])
    logits = jnp.dot(hn.astype(bf16), rw_ref[...], preferred_element_type=f32)
    lane = lax.broadcasted_iota(jnp.int32, logits.shape, 1)
    logits = jnp.where(lane < N_EXPERTS, logits, -jnp.inf)
    z = jnp.exp(logits - jnp.max(logits, axis=-1, keepdims=True))
    aff_ref[0] = z / jnp.sum(z, axis=-1, keepdims=True)
    for s in range(hn.shape[1] // LANES):
        hn_ref[0, pl.ds(s, tm, stride=SUBLANES), :] = hn[:, s * LANES:(s + 1) * LANES]


def _outproj_router(x, meta, a, p, wo_a, wo_p, g, rw):
    b, n_body, d = x.shape
    assert n_body % ROW_TILE == 0 and meta.shape[0] <= ROW_TILE
    l = n_body + meta.shape[0]
    n_t = n_body // ROW_TILE + 1
    half = a.shape[2]
    n_e = rw.shape[1]
    row = lambda bi, i: (bi, i, 0)
    fixed = lambda bi, i: (0, 0)
    return pl.pallas_call(
        _outproj_router_kernel,
        grid=(b, n_t),
        in_specs=[
            pl.BlockSpec((1, ROW_TILE, d), lambda bi, i: (bi, jnp.minimum(i, n_t - 2), 0)),
            pl.BlockSpec(meta.shape, fixed),
            pl.BlockSpec((1, ROW_TILE, half), row),
            pl.BlockSpec((1, ROW_TILE, half), row),
            pl.BlockSpec((half, d), fixed),
            pl.BlockSpec((half, d), fixed),
            pl.BlockSpec((1, d), fixed),
            pl.BlockSpec((d, n_e), fixed),
        ],
        out_specs=[
            pl.BlockSpec((1, ROW_TILE, d), row),
            pl.BlockSpec((1, ROW_TILE * SUBLANES, LANES), row),
            pl.BlockSpec((1, ROW_TILE, n_e), row),
        ],
        out_shape=[
            jax.ShapeDtypeStruct((b, l, d), f32),
            jax.ShapeDtypeStruct((b, l * SUBLANES, LANES), f32),
            jax.ShapeDtypeStruct((b, l, n_e), f32),
        ],
        compiler_params=pltpu.CompilerParams(
            dimension_semantics=("arbitrary", "arbitrary"), vmem_limit_bytes=40 * MIB),
        name="outproj_router",
    )(x, meta, a, p, wo_a, wo_p, g, rw)


def _block_counts(flags, w_scan):
    inside, before = [], []
    offset = jnp.zeros((flags.shape[0], LANES), f32)
    for blk in range(flags.shape[1] // LANES):
        r = jnp.dot(flags[:, blk * LANES:(blk + 1) * LANES].astype(bf16), w_scan, preferred_element_type=f32)
        inside.append(r[:, :LANES])
        before.append(offset)
        offset = offset + r[:, LANES:]
    return inside, before, offset


def _prefix_count(flags, w_scan):
    inside, before, _ = _block_counts(flags, w_scan)
    return jnp.concatenate([i + b for i, b in zip(inside, before)], axis=1)


def _topk_kernel(aff_ref, idx_ref, chosen_scr, first_scr, last_scr, inside_scr, *, cap, n_body):
    n_e, lp = aff_ref.shape[1], aff_ref.shape[2]
    cp = idx_ref.shape[1]
    n_blk = lp // LANES
    bits = lax.bitcast_convert_type(aff_ref[0], jnp.int32)

    def refine(step, thr):
        cand = thr | (1 << (30 - step))
        cnt = jnp.sum(jnp.where(bits >= cand, 1, 0), axis=1, keepdims=True)
        return jnp.where(cnt >= cap, cand, thr)

    thr = lax.fori_loop(0, 31, refine, jnp.zeros((n_e, 1), jnp.int32))
    aff = aff_ref[0]
    pivot = jnp.max(jnp.where(bits == thr, aff, -1.0), axis=1, keepdims=True)
    above = aff > pivot
    tie = aff == pivot
    need = (cap - jnp.sum(jnp.where(above, 1, 0), axis=1, keepdims=True)).astype(f32)

    row = lax.broadcasted_iota(jnp.int32, (LANES, 2 * LANES), 0)
    col = lax.broadcasted_iota(jnp.int32, (LANES, 2 * LANES), 1)
    w_scan = jnp.where((row <= col) | (col >= LANES), 1.0, 0.0).astype(bf16)

    tie_rank = _prefix_count(jnp.where(tie, 1.0, 0.0), w_scan)
    chosen_scr[...] = jnp.where(above | (tie & (tie_rank <= need)), 1.0, 0.0)
    pos_f = lax.broadcasted_iota(jnp.int32, (n_e, lp), 1).astype(f32)
    big = float(2 * lp)

    def excess(cnt):
        return jnp.sum(jnp.abs(cnt - float(cap))) > 0.0

    def repair(cnt):
        a = aff_ref[0]
        sel = chosen_scr[...] > 0.5
        lo = jnp.min(jnp.where(sel, a, jnp.inf), axis=1, keepdims=True)
        lo_at = jnp.max(jnp.where(sel & (a == lo), pos_f, -1.0), axis=1, keepdims=True)
        drop = (pos_f == lo_at) & (cnt > float(cap))
        free = (~sel) & (a >= 0.0)
        hi = jnp.max(jnp.where(free, a, -1.0), axis=1, keepdims=True)
        hi_at = jnp.min(jnp.where(free & (a == hi), pos_f, big), axis=1, keepdims=True)
        add = (pos_f == hi_at) & (cnt < float(cap))
        new = jnp.where(add, 1.0, jnp.where(drop, 0.0, chosen_scr[...]))
        chosen_scr[...] = new
        return jnp.sum(new, axis=1, keepdims=True)

    lax.while_loop(excess, repair, jnp.sum(chosen_scr[...], axis=1, keepdims=True))
    assert n_blk <= LANES
    inside, before, total = _block_counts(chosen_scr[...], w_scan)
    blk_lane = lax.broadcasted_iota(jnp.int32, (n_e, LANES), 1)
    first = jnp.zeros((n_e, LANES), f32)
    for t in range(n_blk):
        first = jnp.where(blk_lane == t, before[t], first)
    last = jnp.zeros((n_e, LANES), f32)
    for t in range(n_blk):
        last = jnp.where(blk_lane == t, before[t + 1] if t + 1 < n_blk else total, last)
    inside_scr[...] = jnp.zeros(inside_scr.shape, f32)
    for e in range(n_e):
        first_scr[e] = first[e:e + 1, :]
        last_scr[e] = last[e:e + 1, :]
        for t in range(n_blk):
            inside_scr[e, t:t + 1, :] = inside[t][e:e + 1, :]

    lane = lax.broadcasted_iota(jnp.int32, (SUBLANES, LANES), 1)
    sub = lax.broadcasted_iota(jnp.int32, (SUBLANES, LANES), 0)
    blk_f = lane.astype(f32)

    n_groups = cp // SUBLANES
    unroll = next(u for u in (33, 22, 16, 11, 8, 6, 5, 4, 3, 2, 1) if n_groups % u == 0)

    def per_expert(e, carry):
        inside_e = inside_scr[e].astype(bf16)
        first_e = jnp.broadcast_to(first_scr[e], (SUBLANES, LANES))
        last_e = jnp.broadcast_to(last_scr[e], (SUBLANES, LANES))

        def per_groups(gi, carry2):
            found = []
            for u in range(unroll):
                c = ((gi * unroll + u) * SUBLANES + sub).astype(f32)
                own = jnp.where((first_e <= c) & (c < last_e), 1.0, 0.0)
                in_block = jnp.dot(own.astype(bf16), inside_e, preferred_element_type=f32)
                start = jnp.sum(own * first_e, axis=1, keepdims=True)
                block = jnp.sum(own * blk_f, axis=1, keepdims=True)
                upto = jnp.sum(jnp.where(in_block <= c - start, 1.0, 0.0), axis=1, keepdims=True)
                pos = (block * LANES + upto).astype(jnp.int32)
                found.append(jnp.where(pos < N_META, pos + n_body, pos - N_META))
            for u, rows in enumerate(found):
                at = pl.ds(pl.multiple_of((gi * unroll + u) * SUBLANES, SUBLANES), SUBLANES)
                idx_ref[0, at, :] = jnp.where(lane == e, rows, idx_ref[0, at, :])
            return carry2

        return lax.fori_loop(0, n_groups // unroll, per_groups, carry)

    idx_ref[0] = jnp.zeros((cp, LANES), jnp.int32)
    lax.fori_loop(0, n_e, per_expert, 0)


def _topk_rows(aff_pos, cap, cp, n_body):
    b, n_e, lp = aff_pos.shape
    kernel = functools.partial(_topk_kernel, cap=cap, n_body=n_body)
    return pl.pallas_call(
        kernel,
        grid=(b,),
        in_specs=[pl.BlockSpec((1, n_e, lp), lambda bi: (bi, 0, 0))],
        out_specs=pl.BlockSpec((1, cp, LANES), lambda bi: (bi, 0, 0)),
        out_shape=jax.ShapeDtypeStruct((b, cp, LANES), jnp.int32),
        scratch_shapes=[pltpu.VMEM((n_e, lp), f32), pltpu.VMEM((n_e, 1, LANES), f32),
                        pltpu.VMEM((n_e, 1, LANES), f32), pltpu.VMEM((n_e, LANES, LANES), f32)],
        compiler_params=pltpu.CompilerParams(
            dimension_semantics=("arbitrary",), vmem_limit_bytes=32 * MIB),
        name="topk_rows",
    )(aff_pos)


def _gather_kernel(code_ref, src_ref, aff_ref, o_ref, gate_ref, tile):
    cp = o_ref.shape[1]
    sub = lax.broadcasted_iota(jnp.int32, (SUBLANES, LANES), 0)

    def group(g, carry):
        gates = jnp.zeros((SUBLANES, LANES), f32)
        for j in range(SUBLANES):
            slot = g * SUBLANES + j
            code = code_ref[0, 0, slot]
            tile[pl.ds(pl.multiple_of(slot * SUBLANES, SUBLANES), SUBLANES), :] = (
                src_ref[0, pl.ds(pl.multiple_of(code & -SUBLANES, SUBLANES), SUBLANES), :])
            aff8 = aff_ref[0, pl.ds(pl.multiple_of((code >> 6) << 3, SUBLANES), SUBLANES), :]
            gates = jnp.where(sub == j, pltpu.roll(aff8, code & (SUBLANES - 1), axis=0), gates)
        gate_ref[0, pl.ds(pl.multiple_of(g * SUBLANES, SUBLANES), SUBLANES), :] = gates
        return carry

    lax.fori_loop(0, cp // SUBLANES, group, 0)
    for s in range(o_ref.shape[2] // LANES):
        o_ref[0, :, s * LANES:(s + 1) * LANES] = tile[pl.ds(s, cp, stride=SUBLANES), :].astype(bf16)


def _gather_tokens(idx, hn_tiles, aff, d):
    b, n_e, cp = idx.shape
    rows = hn_tiles.shape[1]
    l = aff.shape[1]
    scalars = pl.BlockSpec((1, 1, cp), lambda bi, e: (bi * n_e + e, 0, 0), memory_space=pltpu.SMEM)
    flat = idx.reshape(b * n_e, 1, cp)
    slot = jnp.arange(cp, dtype=jnp.int32)[None, None, :]
    return pl.pallas_call(
        _gather_kernel,
        grid=(b, n_e),
        in_specs=[
            scalars,
            pl.BlockSpec((1, rows, LANES), lambda bi, e: (bi, 0, 0)),
            pl.BlockSpec((1, l, LANES), lambda bi, e: (bi, 0, 0)),
        ],
        out_specs=[
            pl.BlockSpec((1, cp, d), lambda bi, e: (e, bi, 0)),
            pl.BlockSpec((1, cp, LANES), lambda bi, e: (e, bi, 0)),
        ],
        out_shape=[
            jax.ShapeDtypeStruct((n_e, b * cp, d), bf16),
            jax.ShapeDtypeStruct((n_e, b * cp, LANES), f32),
        ],
        scratch_shapes=[pltpu.VMEM((cp * SUBLANES, LANES), f32)],
        compiler_params=pltpu.CompilerParams(
            dimension_semantics=("arbitrary", "arbitrary"), vmem_limit_bytes=48 * MIB),
        name="gather_tokens",
    )(flat * SUBLANES + (slot - flat) % SUBLANES, hn_tiles, aff)


def _moe_kernel(x_ref, gate_ref, wg_ref, wu_ref, wd_ref, y_ref, wg_b, wu_b, wd_b, acc, *, cp):
    f = pl.program_id(1)
    n_f = pl.num_programs(1)
    wg_b[...] = wg_ref[0].astype(bf16)
    wu_b[...] = wu_ref[0].astype(bf16)
    wd_b[...] = wd_ref[0].astype(bf16)

    @pl.when(f == 0)
    def _():
        acc[...] = jnp.zeros(acc.shape, f32)

    n_chunks = acc.shape[0] // cp
    for c in range(n_chunks):
        rows = slice(c * cp, (c + 1) * cp)
        x = x_ref[0, rows, :]
        g = jnp.dot(x, wg_b[...], preferred_element_type=f32)
        u = jnp.dot(x, wu_b[...], preferred_element_type=f32)
        hid = (g * jax.nn.sigmoid(g) * u).astype(bf16)
        acc[rows, :] += jnp.dot(hid, wd_b[...], preferred_element_type=f32)

    @pl.when(f == n_f - 1)
    def _():
        for c in range(n_chunks):
            rows = slice(c * cp, (c + 1) * cp)
            lane = lax.broadcasted_iota(jnp.int32, (cp, LANES), 1)
            gate = jnp.sum(jnp.where(lane == pl.program_id(0), gate_ref[0, rows, :], 0.0),
                           axis=1, keepdims=True)
            y = acc[rows, :] * gate
            for s in range(y.shape[1] // LANES):
                y_ref[0, pl.ds(c * cp * SUBLANES + s, cp, stride=SUBLANES), :] = (
                    y[:, s * LANES:(s + 1) * LANES])


def _expert_ffn(xg, gate, w_gate, w_up, w_down, cp):
    n_e, m, d = xg.shape
    d_ff = w_gate.shape[2]
    kernel = functools.partial(_moe_kernel, cp=cp)
    return pl.pallas_call(
        kernel,
        grid=(n_e, d_ff // FF_TILE),
        in_specs=[
            pl.BlockSpec((1, m, d), lambda e, f: (e, 0, 0)),
            pl.BlockSpec((1, m, LANES), lambda e, f: (e, 0, 0)),
            pl.BlockSpec((1, d, FF_TILE), lambda e, f: (e, 0, f)),
            pl.BlockSpec((1, d, FF_TILE), lambda e, f: (e, 0, f)),
            pl.BlockSpec((1, FF_TILE, d), lambda e, f: (e, f, 0)),
        ],
        out_specs=pl.BlockSpec((1, m * SUBLANES, LANES), lambda e, f: (e, 0, 0)),
        out_shape=jax.ShapeDtypeStruct((n_e, m * SUBLANES, LANES), f32),
        scratch_shapes=[
            pltpu.VMEM((d, FF_TILE), bf16),
            pltpu.VMEM((d, FF_TILE), bf16),
            pltpu.VMEM((FF_TILE, d), bf16),
            pltpu.VMEM((m, d), f32),
        ],
        compiler_params=pltpu.CompilerParams(
            dimension_semantics=("arbitrary", "arbitrary"), vmem_limit_bytes=56 * MIB),
        name="expert_ffn",
    )(xg, gate, w_gate, w_up, w_down)


def _combine_kernel(idx_ref, y_ref, h2_ref, g_ref, o_ref, acc, *, n_experts):
    step = pl.program_id(1)
    cp = idx_ref.shape[2]
    tc = o_ref.shape[1]

    @pl.when(step == 0)
    def _():
        acc[...] = jnp.zeros(acc.shape, f32)

    @pl.when(step < n_experts)
    def _():
        def group(g, carry):
            dst = []
            val = []
            for j in range(SUBLANES):
                slot = g * SUBLANES + j
                r = pl.multiple_of(idx_ref[0, 0, slot] * SUBLANES, SUBLANES)
                dst.append(r)
                val.append(acc[pl.ds(r, SUBLANES), :]
                           + y_ref[0, pl.ds(pl.multiple_of(slot * SUBLANES, SUBLANES), SUBLANES), :])
            for r, v in zip(dst, val):
                acc[pl.ds(r, SUBLANES), :] = v
            return carry

        lax.fori_loop(0, cp // SUBLANES, group, 0)

    @pl.when(step >= n_experts)
    def _():
        r0 = pl.multiple_of((step - n_experts) * tc * SUBLANES, SUBLANES)
        moe = jnp.concatenate(
            [acc[pl.ds(r0 + s, tc, stride=SUBLANES), :] for s in range(o_ref.shape[2] // LANES)], axis=-1)
        o_ref[0] = _rms(h2_ref[0] + moe, g_ref[...])


def _combine(idx, y, h2, g, n_body):
    b, n_e, cp = idx.shape
    _, l, d = h2.shape
    n_out = n_body // OUT_CHUNK
    kernel = functools.partial(_combine_kernel, n_experts=n_e)
    expert = lambda s: jnp.minimum(s, n_e - 1)
    chunk = lambda s: jnp.maximum(s - n_e, 0)
    return pl.pallas_call(
        kernel,
        grid=(b, n_e + n_out),
        in_specs=[
            pl.BlockSpec((1, 1, cp), lambda bi, s: (bi * n_e + expert(s), 0, 0), memory_space=pltpu.SMEM),
            pl.BlockSpec((1, cp * SUBLANES, LANES), lambda bi, s: (expert(s), bi, 0)),
            pl.BlockSpec((1, OUT_CHUNK, d), lambda bi, s: (bi, chunk(s), 0)),
            pl.BlockSpec((1, d), lambda bi, s: (0, 0)),
        ],
        out_specs=pl.BlockSpec((1, OUT_CHUNK, d), lambda bi, s: (bi, chunk(s), 0)),
        out_shape=jax.ShapeDtypeStruct((b, n_body, d), f32),
        scratch_shapes=[pltpu.VMEM(((l + 1) * SUBLANES, LANES), f32)],
        compiler_params=pltpu.CompilerParams(
            dimension_semantics=("arbitrary", "arbitrary"), vmem_limit_bytes=48 * MIB),
        name="combine",
    )(idx.reshape(b * n_e, 1, cp), y, h2, g)


def _round_up(n, k):
    return -(-n // k) * k


def kernel(x, meta_tokens, g_mix, w_in, lam_vecs, subln_gain, pool_w, pool_scale, w_o, g_ffn, router_w,
           w_gate, w_up, w_down, g_final):
    b, n_body, d = x.shape
    l = n_body + N_META
    qk = HEADS * D_QK
    attn_width = HEADS * D_V
    depth = g_mix.shape[0]
    cap = CAPACITY_FACTOR * l // N_EXPERTS
    cp = _round_up(cap, BF16_ROWS)

    assert depth == 1, "one layer: the combine kernel emits only the rows that are returned"
    meta = meta_tokens.astype(x.dtype)
    slopes = jnp.broadcast_to(
        jnp.array([_alibi_slope(i) * LOG2E for i in range(HEADS)], f32)[:, None, None], (HEADS, 1, LANES))

    def per_head_pairs(cols):
        return cols.reshape(d, 2, HEADS, D_QK).transpose(0, 2, 1, 3).reshape(d, 2 * qk)

    out = None
    for layer in range(depth):
        lam_init = 0.8 - 0.6 * math.exp(-0.3 * layer)
        w = w_in[layer]
        w_q = per_head_pairs(w[:, :2 * qk]) * (LOG2E / math.sqrt(D_QK))
        w_k = per_head_pairs(w[:, 2 * qk:4 * qk])
        w_v = w[:, 4 * qk:4 * qk + attn_width]
        w_u = w[:, 4 * qk + attn_width:]
        w_qv_t = jnp.concatenate([w_q, w_v], axis=1).T.astype(bf16)
        w_ku = jnp.concatenate([w_k, w_u], axis=1).astype(bf16)
        qvt, k, u = _norm_inproj(x, meta, g_mix[layer][None, :], w_qv_t, w_ku, 2 * qk)
        a = _diff_attention(qvt, k, lam_vecs[layer].astype(f32), subln_gain[layer][None, :],
                            slopes, n_body, lam_init)
        p = _multiscale_pool(u, pool_w[layer].astype(bf16), pool_scale[layer][None, :], n_body)
        wo = w_o[layer].astype(bf16)
        h2, hn_tiles, aff = _outproj_router(
            x, meta, a, p, wo[:attn_width], wo[attn_width:], g_ffn[layer][None, :],
            jnp.pad(router_w[layer], ((0, 0), (0, LANES - N_EXPERTS))).astype(bf16))

        lp = _round_up(l, LANES)
        aff_pos = jnp.concatenate(
            [aff[:, n_body:, :N_EXPERTS], aff[:, :n_body, :N_EXPERTS], jnp.full((b, lp - l, N_EXPERTS), -1.0, f32)],
            axis=1)
        idx = _topk_rows(jnp.swapaxes(aff_pos, 1, 2), cap, cp, n_body)
        idx = jnp.swapaxes(idx[:, :, :N_EXPERTS], 1, 2)
        real = jnp.arange(cp)[None, None, :] < cap
        idx_gather = jnp.where(real, idx, 0)
        idx_scatter = jnp.where(real, idx, l)

        xg, gate = _gather_tokens(idx_gather, hn_tiles, aff, d)
        y = _expert_ffn(xg, gate, w_gate[layer], w_up[layer], w_down[layer], cp)
        out = _combine(idx_scatter, y, h2, g_final[None, :], n_body)
    return out
```

```python
import functools
import math

import jax
import jax.numpy as jnp
from jax import lax
from jax.experimental import pallas as pl
from jax.experimental.pallas import tpu as pltpu

N_META = 16
HEADS = 4
D_QK = 64
D_V = 128
POOL_WINDOWS = (2, 4, 8, 16)
POOL_HALO = max(POOL_WINDOWS) // 2
N_EXPERTS = 16
CAPACITY_FACTOR = 2
EPS = 1e-6
LOG2E = 1.4426950408889634


def _alibi_slope(head):
    return 2.0 ** (-8.0 * (head + 1) / HEADS)


LANES = 128
SUBLANES = 8
BF16_ROWS = 16

ROW_TILE = 512
ATTN_TQ = 256
ATTN_TK = 512
POOL_CHUNK = 256
FF_TILE = 256
OUT_CHUNK = 1024
MIB = 1024 * 1024

f32 = jnp.float32
bf16 = jnp.bfloat16


def _rms(x, g):
    return x * lax.rsqrt(jnp.mean(x * x, axis=-1, keepdims=True) + EPS) * g


def _norm_inproj_kernel(x_ref, meta_ref, g_ref, wt_ref, w_ref, qvt_ref, k_ref, u_ref):
    n_k = k_ref.shape[-1]
    n_meta = meta_ref.shape[0]
    is_meta_tile = pl.program_id(1) == pl.num_programs(1) - 1

    def project(rows):
        n = _rms(rows, g_ref[...]).astype(bf16)
        qvt = lax.dot_general(wt_ref[...], n, (((1,), (1,)), ((), ())), preferred_element_type=f32)
        ku = jnp.dot(n, w_ref[...], preferred_element_type=f32)
        return qvt.astype(bf16), ku[:, :n_k].astype(bf16), ku[:, n_k:]

    @pl.when(jnp.logical_not(is_meta_tile))
    def _():
        qvt_ref[0, 0], k_ref[0], u_ref[0] = project(x_ref[0])

    @pl.when(is_meta_tile)
    def _():
        qvt, k, u = project(meta_ref[...])
        qvt_ref[0, 0] = jnp.zeros(qvt_ref.shape[2:], bf16)
        qvt_ref[0, 0, :, 0:n_meta] = qvt
        k_ref[0, 0:n_meta, :] = k
        u_ref[0, 0:n_meta, :] = u


def _norm_inproj(x, meta, g, w_qv_t, w_ku, n_k):
    b, n_body, d = x.shape
    assert n_body % ATTN_TK == 0 and meta.shape[0] <= ATTN_TK
    l = n_body + meta.shape[0]
    n_t = n_body // ATTN_TK + 1
    n_qv = w_qv_t.shape[0]
    n_u = w_ku.shape[1] - n_k
    return pl.pallas_call(
        _norm_inproj_kernel,
        grid=(b, n_t),
        in_specs=[
            pl.BlockSpec((1, ATTN_TK, d), lambda bi, i: (bi, jnp.minimum(i, n_t - 2), 0)),
            pl.BlockSpec(meta.shape, lambda bi, i: (0, 0)),
            pl.BlockSpec((1, d), lambda bi, i: (0, 0)),
            pl.BlockSpec(w_qv_t.shape, lambda bi, i: (0, 0)),
            pl.BlockSpec(w_ku.shape, lambda bi, i: (0, 0)),
        ],
        out_specs=[
            pl.BlockSpec((1, 1, n_qv, ATTN_TK), lambda bi, i: (bi, i, 0, 0)),
            pl.BlockSpec((1, ATTN_TK, n_k), lambda bi, i: (bi, i, 0)),
            pl.BlockSpec((1, ATTN_TK, n_u), lambda bi, i: (bi, i, 0)),
        ],
        out_shape=[
            jax.ShapeDtypeStruct((b, n_t, n_qv, ATTN_TK), bf16),
            jax.ShapeDtypeStruct((b, l, n_k), bf16),
            jax.ShapeDtypeStruct((b, l, n_u), f32),
        ],
        compiler_params=pltpu.CompilerParams(
            dimension_semantics=("arbitrary", "arbitrary"), vmem_limit_bytes=40 * MIB),
        name="norm_inproj",
    )(x, meta, g, w_qv_t, w_ku)


def _attn_kernel(lam_ref, gain_ref, slope_ref, q_ref, qe_ref, k_ref, ke_ref, vt_ref, o_ref,
                 qs_scr, bias_scr, tail_scr, vt_scr, *scr, n_body, lam_init):
    tq = ATTN_TQ
    tk = ATTN_TK
    n_diag = tk // tq
    n_sub = q_ref.shape[3] // tq
    i = pl.program_id(2)
    n_q = pl.num_programs(2)
    is_meta = i == n_q - 1
    slope = slope_ref[0][:, :1]

    @pl.when(i == 0)
    def _():
        d = (lax.broadcasted_iota(jnp.int32, (tk, tq), 0)
             - lax.broadcasted_iota(jnp.int32, (tk, tq), 1)).astype(f32)
        for v in range(n_diag):
            bias_scr[v] = -slope * jnp.abs(d - float(v * tq))
        tail_scr[0] = slope * d[0:N_META]
        tail_scr[1] = -slope * jnp.abs(d[0:N_META])
        ones_row = jnp.where(lax.broadcasted_iota(jnp.int32, (BF16_ROWS, tk), 0) == 0, 1.0, 0.0).astype(bf16)
        for c in range(vt_ref.shape[1]):
            vt_scr[c, 0:D_V, :] = vt_ref[0, c]
            vt_scr[c, D_V:, :] = ones_row
        width = qs_scr.shape[2]
        qs_scr[:, 2 * D_QK + BF16_ROWS:, :] = jnp.zeros((3, 2 * LANES - 2 * D_QK - BF16_ROWS, width), bf16)
        qs_scr[2, 2 * D_QK:2 * D_QK + BF16_ROWS, :] = jnp.zeros((BF16_ROWS, width), bf16)

    zero = jnp.zeros((D_QK, tq), bf16)
    for sub in range(n_sub):
        q = q_ref[0, 0, :, sub * tq:(sub + 1) * tq]
        cols = slice(2 * sub * tq, 2 * (sub + 1) * tq)
        top = jnp.concatenate([q[0:D_QK], zero], axis=1)
        bottom = jnp.concatenate([zero, q[D_QK:]], axis=1)
        for v in range(3):
            qs_scr[v, 0:D_QK, cols] = top
            qs_scr[v, D_QK:2 * D_QK, cols] = bottom
        for v in range(2):
            ext = qe_ref[0, v, :, sub * tq:(sub + 1) * tq]
            qs_scr[v, 2 * D_QK:2 * D_QK + BF16_ROWS, cols] = jnp.concatenate([ext, ext], axis=1)

    n_streams = 2 * n_sub
    per_stream = len(scr) // n_streams
    streams = [scr[st * per_stream:(st + 1) * per_stream] for st in range(n_streams)]
    for stream in streams:
        stream[6][...] = jnp.full(stream[6].shape, -jnp.inf, f32)
        stream[7][...] = jnp.zeros(stream[7].shape, f32)

    q_pos0 = jnp.where(is_meta, 0, N_META + i * n_sub * tq)
    q_valid = jnp.where(is_meta, N_META, n_sub * tq)
    n_slab = tq // LANES
    piece = BF16_ROWS

    s_at, p_at, a_at = (0, 1, 8), (2, 3, 9), (4, 5, 10)

    def issue_scores(k, version, slot):
        n = k.shape[0]
        for st, stream in enumerate(streams):
            stream[s_at[slot]][0:n, :] = jnp.dot(k, qs_scr[version, 0:k.shape[1], st * tq:(st + 1) * tq],
                                                 preferred_element_type=f32)

    def softmax(slot, n, bias_tile=None, consts=None):
        def scores(s_scr, r, slab):
            t = s_scr[r:r + piece, slab * LANES:(slab + 1) * LANES]
            return t if bias_tile is None else t + bias_tile(r, slab)

        stats = []
        for st, stream in enumerate(streams):
            s_scr, m_scr = stream[s_at[slot]], stream[6]
            const = None if consts is None else consts[st // 2]
            m_old = m_scr[...]
            maxes = []
            for slab in range(n_slab):
                mx = scores(s_scr, 0, slab)
                for r in range(piece, n, piece):
                    mx = jnp.maximum(mx, scores(s_scr, r, slab))
                maxes.append(jnp.max(mx, axis=0, keepdims=True))
            m_loc = jnp.concatenate(maxes, axis=1)
            m_new = jnp.maximum(m_old, m_loc if const is None else m_loc + const)
            stats.append((m_new, m_new if const is None else m_new - const, jnp.exp2(m_old - m_new)))

        for stream, (m_new, shift, alpha) in zip(streams, stats):
            s_scr, p_scr, a_scr, m_scr = stream[s_at[slot]], stream[p_at[slot]], stream[a_at[slot]], stream[6]
            for slab in range(n_slab):
                cols = slice(slab * LANES, (slab + 1) * LANES)
                sub_b = jnp.broadcast_to(shift[:, cols], (piece, LANES))
                for r in range(0, n, piece):
                    p_scr[r:r + piece, cols] = jnp.exp2(scores(s_scr, r, slab) - sub_b).astype(bf16)
            a_scr[...] = alpha
            m_scr[...] = m_new

    def apply_values(vt, slot):
        n = vt.shape[1]
        for stream in streams:
            p_scr, a_scr, acc_scr = stream[p_at[slot]], stream[a_at[slot]], stream[7]
            acc_scr[...] = a_scr[...] * acc_scr[...] + jnp.dot(vt, p_scr[0:n, :], preferred_element_type=f32)

    def keys_with_positions(c):
        rows = pl.ds(pl.multiple_of(c * tk, tk), tk)
        return jnp.concatenate([k_ref[0, rows, :], ke_ref[0, rows, :]], axis=1)

    n_chunks = n_body // tk

    def issue_chunk_scores(c, slot):
        k_pos0 = N_META + c * tk
        version = jnp.where(k_pos0 + tk <= q_pos0, 0, jnp.where(k_pos0 >= q_pos0 + q_valid, 1, 2))
        issue_scores(keys_with_positions(c), version, slot)

    def add_straddle_bias(c, slot):
        k_pos0 = N_META + c * tk

        @pl.when(jnp.logical_and(k_pos0 + tk > q_pos0, k_pos0 < q_pos0 + q_valid))
        def _():
            for st, stream in enumerate(streams):
                variant = (q_pos0 - k_pos0) // tq + st // 2
                for slab in range(n_slab):
                    cols = slice(slab * LANES, (slab + 1) * LANES)
                    for r in range(0, tk, piece):
                        stream[slot][r:r + piece, cols] = (stream[slot][r:r + piece, cols]
                                                           + bias_scr[variant, r:r + piece, cols])

    def step(t, slot, issue_next, apply_prev, meta_keys=None):
        add_straddle_bias(t, slot)
        if issue_next:
            issue_chunk_scores(t + 1, 1 - slot)
        if meta_keys is not None:
            softmax(2, N_META, *meta_keys)
        softmax(slot, tk)
        if apply_prev:
            apply_values(vt_scr[t - 1], 1 - slot)
        if meta_keys is not None:
            apply_values(vt_scr[n_chunks, :, 0:N_META], 2)

    def run():
        def pair(u, carry):
            step(2 * u, 0, True, True)
            step(2 * u + 1, 1, True, True)
            return carry

        tail = jnp.where(is_meta, 1, 0)
        consts = [-slope * jnp.full((1, 1), q_pos0 + sub * tq, jnp.int32).astype(f32) for sub in range(n_sub)]
        meta_keys = (lambda r, slab: tail_scr[tail, r:r + piece, slab * LANES:(slab + 1) * LANES], consts)
        issue_scores(k_ref[0, n_body:n_body + N_META, :], 2, 2)
        issue_chunk_scores(0, 0)
        step(0, 0, True, False, meta_keys)
        step(1, 1, True, True)
        lax.fori_loop(1, n_chunks // 2 - 1, pair, 0)
        step(n_chunks - 2, 0, True, True)
        step(n_chunks - 1, 1, False, True)
        apply_values(vt_scr[n_chunks - 1], 1)

        lv = lam_ref[...]
        lam = (jnp.exp(jnp.sum(lv[0:1] * lv[1:2], axis=-1, keepdims=True))
               - jnp.exp(jnp.sum(lv[2:3] * lv[3:4], axis=-1, keepdims=True)) + lam_init)
        for sub in range(n_sub):
            o1, o2 = [stream[7][0:D_V, :] / stream[7][D_V:D_V + 1, :] for stream in streams[2 * sub:2 * sub + 2]]
            att = (o1 - lam * o2).T
            o_ref[0, sub * tq:(sub + 1) * tq, :] = (_rms(att, gain_ref[...]) * (1.0 - lam_init)).astype(bf16)

    pl.when(jnp.logical_not(is_meta))(lambda: run())
    streams, n_sub = streams[:2], 1
    pl.when(is_meta)(lambda: run())


def _alibi_tables(n_body, n_q_tiles):
    l = n_body + N_META
    pos = jnp.concatenate([jnp.arange(n_body) + N_META, jnp.arange(N_META)]).astype(f32)
    q_pos = jnp.pad(pos, (0, n_q_tiles * ATTN_TQ - l))
    k_hi, k_lo = jnp.floor(pos / 64), jnp.mod(pos, 64)
    q_hi, q_lo = jnp.floor(q_pos / 64), jnp.mod(q_pos, 64)
    ke, qe = [], []
    for h in range(HEADS):
        c = jnp.float32(_alibi_slope(h) * LOG2E)
        c1 = c.astype(bf16).astype(f32)
        c2 = (c - c1).astype(bf16).astype(f32)
        c3 = (c - c1 - c2).astype(bf16).astype(f32)
        pieces = [c1, c2, c3]
        cols = [k_hi] * 3 + [k_lo] * 3 + [jnp.full_like(pos, p) for p in pieces] * 2
        ke.append(jnp.pad(jnp.stack(cols, axis=1), ((0, 0), (0, LANES - len(cols)))))
        rows = ([jnp.full_like(q_pos, 64 * p) for p in pieces] + [jnp.full_like(q_pos, p) for p in pieces]
                + [-64 * q_hi] * 3 + [-q_lo] * 3)
        ext = jnp.pad(jnp.stack(rows), ((0, BF16_ROWS - len(rows)), (0, 0)))
        qe.append(jnp.stack([ext, -ext]))
    return jnp.stack(ke).astype(bf16), jnp.stack(qe).astype(bf16)


def _diff_attention(qvt, k, lam_vecs, gain, slopes, n_body, lam_init):
    b, l, _ = k.shape
    n_t = qvt.shape[1]
    per_chunk = ATTN_TK // ATTN_TQ
    n_q = pl.cdiv(l, ATTN_TK)
    assert ATTN_TK % ATTN_TQ == 0 and n_body % (2 * ATTN_TK) == 0 and n_body >= 4 * ATTN_TK
    ke, qe = _alibi_tables(n_body, n_q * per_chunk)
    kernel = functools.partial(_attn_kernel, n_body=n_body, lam_init=lam_init)
    return pl.pallas_call(
        kernel,
        grid=(b, HEADS, n_q),
        in_specs=[
            pl.BlockSpec(lam_vecs.shape, lambda bi, h, i: (0, 0)),
            pl.BlockSpec((1, D_V), lambda bi, h, i: (0, 0)),
            pl.BlockSpec((1, 1, LANES), lambda bi, h, i: (h, 0, 0)),
            pl.BlockSpec((1, 1, 2 * D_QK, ATTN_TK), lambda bi, h, i: (bi, i, h, 0)),
            pl.BlockSpec((1, 2, BF16_ROWS, ATTN_TK), lambda bi, h, i: (h, 0, 0, i)),
            pl.BlockSpec((1, l, 2 * D_QK), lambda bi, h, i: (bi, 0, h)),
            pl.BlockSpec((1, l, LANES), lambda bi, h, i: (h, 0, 0)),
            pl.BlockSpec((1, n_t, D_V, ATTN_TK), lambda bi, h, i: (bi, 0, HEADS + h, 0)),
        ],
        out_specs=pl.BlockSpec((1, ATTN_TK, D_V), lambda bi, h, i: (bi, i, h)),
        out_shape=jax.ShapeDtypeStruct((b, l, HEADS * D_V), bf16),
        scratch_shapes=[
            pltpu.VMEM((3, 2 * LANES, 2 * ATTN_TK), bf16),
            pltpu.VMEM((per_chunk, ATTN_TK, ATTN_TQ), f32),
            pltpu.VMEM((2, N_META, ATTN_TQ), f32),
            pltpu.VMEM((n_t, D_V + BF16_ROWS, ATTN_TK), bf16),
        ] + 2 * per_chunk * [
            pltpu.VMEM((ATTN_TK, ATTN_TQ), f32), pltpu.VMEM((ATTN_TK, ATTN_TQ), f32),
            pltpu.VMEM((ATTN_TK, ATTN_TQ), bf16), pltpu.VMEM((ATTN_TK, ATTN_TQ), bf16),
            pltpu.VMEM((1, ATTN_TQ), f32), pltpu.VMEM((1, ATTN_TQ), f32),
            pltpu.VMEM((1, ATTN_TQ), f32),
            pltpu.VMEM((D_V + BF16_ROWS, ATTN_TQ), f32),
            pltpu.VMEM((N_META, ATTN_TQ), f32), pltpu.VMEM((N_META, ATTN_TQ), bf16), pltpu.VMEM((1, ATTN_TQ), f32),
        ],
        compiler_params=pltpu.CompilerParams(
            dimension_semantics=("arbitrary", "arbitrary", "arbitrary"), vmem_limit_bytes=40 * MIB),
        name="diff_attention",
    )(lam_vecs, gain, slopes, qvt, qe, k, ke, qvt)


def _pool_kernel(u_ref, pw_ref, ps_ref, o_ref, upad, *, n_body):
    l = n_body + N_META
    width = u_ref.shape[-1]
    upad[0:POOL_HALO, :] = jnp.zeros((POOL_HALO, width), f32)
    upad[POOL_HALO:POOL_HALO + N_META, :] = u_ref[0, n_body:l, :]
    upad[POOL_HALO + N_META:POOL_HALO + l, :] = u_ref[0, 0:n_body, :]
    upad[POOL_HALO + l:2 * POOL_HALO + l, :] = jnp.zeros((POOL_HALO, width), f32)

    def rows(pos0, n, out_row0):
        slab = upad[pl.ds(pos0, n + 2 * POOL_HALO), :]
        t = pos0 + lax.broadcasted_iota(jnp.int32, (n, 1), 0)
        for gi, w in enumerate(POOL_WINDOWS):
            cols = slice(gi * LANES, (gi + 1) * LANES)
            sums, size = slab[:, cols], 1
            while size < w:
                m = sums.shape[0] - size
                sums = sums[0:m] + sums[size:size + m]
                size *= 2
            total = sums[POOL_HALO - w // 2:POOL_HALO - w // 2 + n]
            cnt = (jnp.minimum(t + w // 2, l) - jnp.maximum(t - w // 2, 0)).astype(f32)
            pooled = total / cnt - slab[POOL_HALO:POOL_HALO + n, cols]
            y = jnp.dot(pooled.astype(bf16), pw_ref[gi], preferred_element_type=f32) * ps_ref[:, cols]
            o_ref[0, pl.ds(out_row0, n), cols] = y.astype(bf16)

    rows(0, N_META, n_body)

    def body(c, carry):
        r0 = pl.multiple_of(c * POOL_CHUNK, POOL_CHUNK)
        rows(N_META + r0, POOL_CHUNK, r0)
        return carry

    lax.fori_loop(0, n_body // POOL_CHUNK, body, 0)


def _multiscale_pool(u, pool_w, pool_scale, n_body):
    b, l, width = u.shape
    kernel = functools.partial(_pool_kernel, n_body=n_body)
    return pl.pallas_call(
        kernel,
        grid=(b,),
        in_specs=[
            pl.BlockSpec((1, l, width), lambda bi: (bi, 0, 0)),
            pl.BlockSpec(pool_w.shape, lambda bi: (0, 0, 0)),
            pl.BlockSpec((1, width), lambda bi: (0, 0)),
        ],
        out_specs=pl.BlockSpec((1, l, width), lambda bi: (bi, 0, 0)),
        out_shape=jax.ShapeDtypeStruct((b, l, width), bf16),
        scratch_shapes=[pltpu.VMEM((l + 2 * POOL_HALO, width), f32)],
        compiler_params=pltpu.CompilerParams(
            dimension_semantics=("arbitrary",), vmem_limit_bytes=48 * MIB),
        name="multiscale_pool",
    )(u, pool_w, pool_scale)


def _outproj_router_kernel(x_ref, meta_ref, a_ref, p_ref, woa_ref, wop_ref, g_ref, rw_ref,
                           h2_ref, hn_ref, aff_ref):
    is_meta_tile = pl.program_id(1) == pl.num_programs(1) - 1

    def rows(h, m):
        h2 = (h + jnp.dot(a_ref[0, 0:m, :], woa_ref[...], preferred_element_type=f32)
              + jnp.dot(p_ref[0, 0:m, :], wop_ref[...], preferred_element_type=f32))
        h2_ref[0, 0:m, :] = h2
        hn = _rms(h2, g_ref[...])
        logits = jnp.dot(hn.astype(bf16), rw_ref[...], preferred_element_type=f32)
        lane = lax.broadcasted_iota(jnp.int32, logits.shape, 1)
        logits = jnp.where(lane < N_EXPERTS, logits, -jnp.inf)
        z = jnp.exp(logits - jnp.max(logits, axis=-1, keepdims=True))
        aff_ref[0, 0:m, :] = z / jnp.sum(z, axis=-1, keepdims=True)
        for s in range(hn.shape[1] // LANES):
            hn_ref[0, pl.ds(s, m, stride=SUBLANES), :] = hn[:, s * LANES:(s + 1) * LANES]

    pl.when(jnp.logical_not(is_meta_tile))(lambda: rows(x_ref[0], x_ref.shape[1]))
    pl.when(is_meta_tile)(lambda: rows(meta_ref[...], meta_ref.shape[0]))


def _outproj_router(x, meta, a, p, wo_a, wo_p, g, rw):
    b, n_body, d = x.shape
    assert n_body % ROW_TILE == 0 and meta.shape[0] <= ROW_TILE
    l = n_body + meta.shape[0]
    n_t = n_body // ROW_TILE + 1
    half = a.shape[2]
    n_e = rw.shape[1]
    row = lambda bi, i: (bi, i, 0)
    fixed = lambda bi, i: (0, 0)
    return pl.pallas_call(
        _outproj_router_kernel,
        grid=(b, n_t),
        in_specs=[
            pl.BlockSpec((1, ROW_TILE, d), lambda bi, i: (bi, jnp.minimum(i, n_t - 2), 0)),
            pl.BlockSpec(meta.shape, fixed),
            pl.BlockSpec((1, ROW_TILE, half), row),
            pl.BlockSpec((1, ROW_TILE, half), row),
            pl.BlockSpec((half, d), fixed),
            pl.BlockSpec((half, d), fixed),
            pl.BlockSpec((1, d), fixed),
            pl.BlockSpec((d, n_e), fixed),
        ],
        out_specs=[
            pl.BlockSpec((1, ROW_TILE, d), row),
            pl.BlockSpec((1, ROW_TILE * SUBLANES, LANES), row),
            pl.BlockSpec((1, ROW_TILE, n_e), row),
        ],
        out_shape=[
            jax.ShapeDtypeStruct((b, l, d), f32),
            jax.ShapeDtypeStruct((b, l * SUBLANES, LANES), f32),
            jax.ShapeDtypeStruct((b, l, n_e), f32),
        ],
        compiler_params=pltpu.CompilerParams(
            dimension_semantics=("arbitrary", "arbitrary"), vmem_limit_bytes=40 * MIB),
        name="outproj_router",
    )(x, meta, a, p, wo_a, wo_p, g, rw)


def _block_counts(flags, w_scan):
    inside, before = [], []
    offset = jnp.zeros((flags.shape[0], LANES), f32)
    for blk in range(flags.shape[1] // LANES):
        r = jnp.dot(flags[:, blk * LANES:(blk + 1) * LANES].astype(bf16), w_scan, preferred_element_type=f32)
        inside.append(r[:, :LANES])
        before.append(offset)
        offset = offset + r[:, LANES:]
    return inside, before, offset


def _prefix_count(flags, w_scan):
    inside, before, _ = _block_counts(flags, w_scan)
    return jnp.concatenate([i + b for i, b in zip(inside, before)], axis=1)


def _topk_kernel(aff_ref, idx_ref, chosen_scr, first_scr, last_scr, inside_scr, *, cap, n_body):
    n_e, lp = aff_ref.shape[1], aff_ref.shape[2]
    cp = idx_ref.shape[1]
    n_blk = lp // LANES
    bits = lax.bitcast_convert_type(aff_ref[0], jnp.int32)

    def refine(step, thr):
        cand = thr | (1 << (30 - step))
        cnt = jnp.sum(jnp.where(bits >= cand, 1, 0), axis=1, keepdims=True)
        return jnp.where(cnt >= cap, cand, thr)

    thr = lax.fori_loop(0, 31, refine, jnp.zeros((n_e, 1), jnp.int32))
    aff = aff_ref[0]
    pivot = jnp.max(jnp.where(bits == thr, aff, -1.0), axis=1, keepdims=True)
    above = aff > pivot
    tie = aff == pivot
    need = (cap - jnp.sum(jnp.where(above, 1, 0), axis=1, keepdims=True)).astype(f32)

    row = lax.broadcasted_iota(jnp.int32, (LANES, 2 * LANES), 0)
    col = lax.broadcasted_iota(jnp.int32, (LANES, 2 * LANES), 1)
    w_scan = jnp.where((row <= col) | (col >= LANES), 1.0, 0.0).astype(bf16)

    tie_rank = _prefix_count(jnp.where(tie, 1.0, 0.0), w_scan)
    chosen_scr[...] = jnp.where(above | (tie & (tie_rank <= need)), 1.0, 0.0)
    pos_f = lax.broadcasted_iota(jnp.int32, (n_e, lp), 1).astype(f32)
    big = float(2 * lp)

    def excess(cnt):
        return jnp.sum(jnp.abs(cnt - float(cap))) > 0.0

    def repair(cnt):
        a = aff_ref[0]
        sel = chosen_scr[...] > 0.5
        lo = jnp.min(jnp.where(sel, a, jnp.inf), axis=1, keepdims=True)
        lo_at = jnp.max(jnp.where(sel & (a == lo), pos_f, -1.0), axis=1, keepdims=True)
        drop = (pos_f == lo_at) & (cnt > float(cap))
        free = (~sel) & (a >= 0.0)
        hi = jnp.max(jnp.where(free, a, -1.0), axis=1, keepdims=True)
        hi_at = jnp.min(jnp.where(free & (a == hi), pos_f, big), axis=1, keepdims=True)
        add = (pos_f == hi_at) & (cnt < float(cap))
        new = jnp.where(add, 1.0, jnp.where(drop, 0.0, chosen_scr[...]))
        chosen_scr[...] = new
        return jnp.sum(new, axis=1, keepdims=True)

    lax.while_loop(excess, repair, jnp.sum(chosen_scr[...], axis=1, keepdims=True))
    assert n_blk <= LANES
    inside, before, total = _block_counts(chosen_scr[...], w_scan)
    blk_lane = lax.broadcasted_iota(jnp.int32, (n_e, LANES), 1)
    first = jnp.zeros((n_e, LANES), f32)
    for t in range(n_blk):
        first = jnp.where(blk_lane == t, before[t], first)
    last = jnp.zeros((n_e, LANES), f32)
    for t in range(n_blk):
        last = jnp.where(blk_lane == t, before[t + 1] if t + 1 < n_blk else total, last)
    inside_scr[...] = jnp.zeros(inside_scr.shape, f32)
    for e in range(n_e):
        first_scr[e] = first[e:e + 1, :]
        last_scr[e] = last[e:e + 1, :]
        for t in range(n_blk):
            inside_scr[e, t:t + 1, :] = inside[t][e:e + 1, :]

    lane = lax.broadcasted_iota(jnp.int32, (SUBLANES, LANES), 1)
    sub = lax.broadcasted_iota(jnp.int32, (SUBLANES, LANES), 0)
    blk_f = lane.astype(f32)

    n_groups = cp // SUBLANES
    unroll = next(u for u in (33, 22, 16, 11, 8, 6, 5, 4, 3, 2, 1) if n_groups % u == 0)

    def per_expert(e, carry):
        inside_e = inside_scr[e].astype(bf16)
        first_e = jnp.broadcast_to(first_scr[e], (SUBLANES, LANES))
        last_e = jnp.broadcast_to(last_scr[e], (SUBLANES, LANES))

        def per_groups(gi, carry2):
            found = []
            for u in range(unroll):
                c = ((gi * unroll + u) * SUBLANES + sub).astype(f32)
                own = jnp.where((first_e <= c) & (c < last_e), 1.0, 0.0)
                in_block = jnp.dot(own.astype(bf16), inside_e, preferred_element_type=f32)
                start = jnp.sum(own * first_e, axis=1, keepdims=True)
                block = jnp.sum(own * blk_f, axis=1, keepdims=True)
                upto = jnp.sum(jnp.where(in_block <= c - start, 1.0, 0.0), axis=1, keepdims=True)
                pos = (block * LANES + upto).astype(jnp.int32)
                found.append(jnp.where(pos < N_META, pos + n_body, pos - N_META))
            for u, rows in enumerate(found):
                at = pl.ds(pl.multiple_of((gi * unroll + u) * SUBLANES, SUBLANES), SUBLANES)
                idx_ref[0, at, :] = jnp.where(lane == e, rows, idx_ref[0, at, :])
            return carry2

        return lax.fori_loop(0, n_groups // unroll, per_groups, carry)

    idx_ref[0] = jnp.zeros((cp, LANES), jnp.int32)
    lax.fori_loop(0, n_e, per_expert, 0)


def _topk_rows(aff_pos, cap, cp, n_body):
    b, n_e, lp = aff_pos.shape
    kernel = functools.partial(_topk_kernel, cap=cap, n_body=n_body)
    return pl.pallas_call(
        kernel,
        grid=(b,),
        in_specs=[pl.BlockSpec((1, n_e, lp), lambda bi: (bi, 0, 0))],
        out_specs=pl.BlockSpec((1, cp, LANES), lambda bi: (bi, 0, 0)),
        out_shape=jax.ShapeDtypeStruct((b, cp, LANES), jnp.int32),
        scratch_shapes=[pltpu.VMEM((n_e, lp), f32), pltpu.VMEM((n_e, 1, LANES), f32),
                        pltpu.VMEM((n_e, 1, LANES), f32), pltpu.VMEM((n_e, LANES, LANES), f32)],
        compiler_params=pltpu.CompilerParams(
            dimension_semantics=("arbitrary",), vmem_limit_bytes=32 * MIB),
        name="topk_rows",
    )(aff_pos)


def _gather_kernel(code_ref, src_ref, aff_ref, o_ref, gate_ref, tile):
    cp = o_ref.shape[1]
    sub = lax.broadcasted_iota(jnp.int32, (SUBLANES, LANES), 0)

    def group(g, carry):
        gates = jnp.zeros((SUBLANES, LANES), f32)
        for j in range(SUBLANES):
            slot = g * SUBLANES + j
            code = code_ref[0, 0, slot]
            tile[pl.ds(pl.multiple_of(slot * SUBLANES, SUBLANES), SUBLANES), :] = (
                src_ref[0, pl.ds(pl.multiple_of(code & -SUBLANES, SUBLANES), SUBLANES), :])
            aff8 = aff_ref[0, pl.ds(pl.multiple_of((code >> 6) << 3, SUBLANES), SUBLANES), :]
            gates = jnp.where(sub == j, pltpu.roll(aff8, code & (SUBLANES - 1), axis=0), gates)
        gate_ref[0, pl.ds(pl.multiple_of(g * SUBLANES, SUBLANES), SUBLANES), :] = gates
        return carry

    lax.fori_loop(0, cp // SUBLANES, group, 0)
    for s in range(o_ref.shape[2] // LANES):
        o_ref[0, :, s * LANES:(s + 1) * LANES] = tile[pl.ds(s, cp, stride=SUBLANES), :].astype(bf16)


def _gather_tokens(idx, hn_tiles, aff, d):
    b, n_e, cp = idx.shape
    rows = hn_tiles.shape[1]
    l = aff.shape[1]
    scalars = pl.BlockSpec((1, 1, cp), lambda bi, e: (bi * n_e + e, 0, 0), memory_space=pltpu.SMEM)
    flat = idx.reshape(b * n_e, 1, cp)
    slot = jnp.arange(cp, dtype=jnp.int32)[None, None, :]
    return pl.pallas_call(
        _gather_kernel,
        grid=(b, n_e),
        in_specs=[
            scalars,
            pl.BlockSpec((1, rows, LANES), lambda bi, e: (bi, 0, 0)),
            pl.BlockSpec((1, l, LANES), lambda bi, e: (bi, 0, 0)),
        ],
        out_specs=[
            pl.BlockSpec((1, cp, d), lambda bi, e: (e, bi, 0)),
            pl.BlockSpec((1, cp, LANES), lambda bi, e: (e, bi, 0)),
        ],
        out_shape=[
            jax.ShapeDtypeStruct((n_e, b * cp, d), bf16),
            jax.ShapeDtypeStruct((n_e, b * cp, LANES), f32),
        ],
        scratch_shapes=[pltpu.VMEM((cp * SUBLANES, LANES), f32)],
        compiler_params=pltpu.CompilerParams(
            dimension_semantics=("arbitrary", "arbitrary"), vmem_limit_bytes=48 * MIB),
        name="gather_tokens",
    )(flat * SUBLANES + (slot - flat) % SUBLANES, hn_tiles, aff)


def _moe_kernel(x_ref, gate_ref, wg_ref, wu_ref, wd_ref, y_ref, wg_b, wu_b, wd_b, acc, *, cp):
    f = pl.program_id(1)
    n_f = pl.num_programs(1)
    wg_b[...] = wg_ref[0].astype(bf16)
    wu_b[...] = wu_ref[0].astype(bf16)
    wd_b[...] = wd_ref[0].astype(bf16)

    @pl.when(f == 0)
    def _():
        acc[...] = jnp.zeros(acc.shape, f32)

    n_chunks = acc.shape[0] // cp
    for c in range(n_chunks):
        rows = slice(c * cp, (c + 1) * cp)
        x = x_ref[0, rows, :]
        g = jnp.dot(x, wg_b[...], preferred_element_type=f32)
        u = jnp.dot(x, wu_b[...], preferred_element_type=f32)
        hid = (g * jax.nn.sigmoid(g) * u).astype(bf16)
        acc[rows, :] += jnp.dot(hid, wd_b[...], preferred_element_type=f32)

    @pl.when(f == n_f - 1)
    def _():
        for c in range(n_chunks):
            rows = slice(c * cp, (c + 1) * cp)
            lane = lax.broadcasted_iota(jnp.int32, (cp, LANES), 1)
            gate = jnp.sum(jnp.where(lane == pl.program_id(0), gate_ref[0, rows, :], 0.0),
                           axis=1, keepdims=True)
            y = acc[rows, :] * gate
            for s in range(y.shape[1] // LANES):
                y_ref[0, pl.ds(c * cp * SUBLANES + s, cp, stride=SUBLANES), :] = (
                    y[:, s * LANES:(s + 1) * LANES])


def _expert_ffn(xg, gate, w_gate, w_up, w_down, cp):
    n_e, m, d = xg.shape
    d_ff = w_gate.shape[2]
    kernel = functools.partial(_moe_kernel, cp=cp)
    return pl.pallas_call(
        kernel,
        grid=(n_e, d_ff // FF_TILE),
        in_specs=[
            pl.BlockSpec((1, m, d), lambda e, f: (e, 0, 0)),
            pl.BlockSpec((1, m, LANES), lambda e, f: (e, 0, 0)),
            pl.BlockSpec((1, d, FF_TILE), lambda e, f: (e, 0, f)),
            pl.BlockSpec((1, d, FF_TILE), lambda e, f: (e, 0, f)),
            pl.BlockSpec((1, FF_TILE, d), lambda e, f: (e, f, 0)),
        ],
        out_specs=pl.BlockSpec((1, m * SUBLANES, LANES), lambda e, f: (e, 0, 0)),
        out_shape=jax.ShapeDtypeStruct((n_e, m * SUBLANES, LANES), f32),
        scratch_shapes=[
            pltpu.VMEM((d, FF_TILE), bf16),
            pltpu.VMEM((d, FF_TILE), bf16),
            pltpu.VMEM((FF_TILE, d), bf16),
            pltpu.VMEM((m, d), f32),
        ],
        compiler_params=pltpu.CompilerParams(
            dimension_semantics=("arbitrary", "arbitrary"), vmem_limit_bytes=56 * MIB),
        name="expert_ffn",
    )(xg, gate, w_gate, w_up, w_down)


def _combine_kernel(idx_ref, y_ref, h2_ref, g_ref, o_ref, acc, *, n_experts):
    step = pl.program_id(1)
    cp = idx_ref.shape[2]
    tc = o_ref.shape[1]

    @pl.when(step == 0)
    def _():
        acc[...] = jnp.zeros(acc.shape, f32)

    @pl.when(step < n_experts)
    def _():
        def group(g, carry):
            dst = []
            val = []
            for j in range(SUBLANES):
                slot = g * SUBLANES + j
                r = pl.multiple_of(idx_ref[0, 0, slot] * SUBLANES, SUBLANES)
                dst.append(r)
                val.append(acc[pl.ds(r, SUBLANES), :]
                           + y_ref[0, pl.ds(pl.multiple_of(slot * SUBLANES, SUBLANES), SUBLANES), :])
            for r, v in zip(dst, val):
                acc[pl.ds(r, SUBLANES), :] = v
            return carry

        lax.fori_loop(0, cp // SUBLANES, group, 0)

    @pl.when(step >= n_experts)
    def _():
        r0 = pl.multiple_of((step - n_experts) * tc * SUBLANES, SUBLANES)
        moe = jnp.concatenate(
            [acc[pl.ds(r0 + s, tc, stride=SUBLANES), :] for s in range(o_ref.shape[2] // LANES)], axis=-1)
        o_ref[0] = _rms(h2_ref[0] + moe, g_ref[...])


def _combine(idx, y, h2, g, n_body):
    b, n_e, cp = idx.shape
    _, l, d = h2.shape
    n_out = n_body // OUT_CHUNK
    kernel = functools.partial(_combine_kernel, n_experts=n_e)
    expert = lambda s: jnp.minimum(s, n_e - 1)
    chunk = lambda s: jnp.maximum(s - n_e, 0)
    return pl.pallas_call(
        kernel,
        grid=(b, n_e + n_out),
        in_specs=[
            pl.BlockSpec((1, 1, cp), lambda bi, s: (bi * n_e + expert(s), 0, 0), memory_space=pltpu.SMEM),
            pl.BlockSpec((1, cp * SUBLANES, LANES), lambda bi, s: (expert(s), bi, 0)),
            pl.BlockSpec((1, OUT_CHUNK, d), lambda bi, s: (bi, chunk(s), 0)),
            pl.BlockSpec((1, d), lambda bi, s: (0, 0)),
        ],
        out_specs=pl.BlockSpec((1, OUT_CHUNK, d), lambda bi, s: (bi, chunk(s), 0)),
        out_shape=jax.ShapeDtypeStruct((b, n_body, d), f32),
        scratch_shapes=[pltpu.VMEM(((l + 1) * SUBLANES, LANES), f32)],
        compiler_params=pltpu.CompilerParams(
            dimension_semantics=("arbitrary", "arbitrary"), vmem_limit_bytes=48 * MIB),
        name="combine",
    )(idx.reshape(b * n_e, 1, cp), y, h2, g)


def _round_up(n, k):
    return -(-n // k) * k


def kernel(x, meta_tokens, g_mix, w_in, lam_vecs, subln_gain, pool_w, pool_scale, w_o, g_ffn, router_w,
           w_gate, w_up, w_down, g_final):
    b, n_body, d = x.shape
    l = n_body + N_META
    qk = HEADS * D_QK
    attn_width = HEADS * D_V
    depth = g_mix.shape[0]
    cap = CAPACITY_FACTOR * l // N_EXPERTS
    cp = _round_up(cap, BF16_ROWS)

    assert depth == 1, "one layer: the combine kernel emits only the rows that are returned"
    meta = meta_tokens.astype(x.dtype)
    slopes = jnp.broadcast_to(
        jnp.array([_alibi_slope(i) * LOG2E for i in range(HEADS)], f32)[:, None, None], (HEADS, 1, LANES))

    def per_head_pairs(cols):
        return cols.reshape(d, 2, HEADS, D_QK).transpose(0, 2, 1, 3).reshape(d, 2 * qk)

    out = None
    for layer in range(depth):
        lam_init = 0.8 - 0.6 * math.exp(-0.3 * layer)
        w = w_in[layer]
        w_q = per_head_pairs(w[:, :2 * qk]) * (LOG2E / math.sqrt(D_QK))
        w_k = per_head_pairs(w[:, 2 * qk:4 * qk])
        w_v = w[:, 4 * qk:4 * qk + attn_width]
        w_u = w[:, 4 * qk + attn_width:]
        w_qv_t = jnp.concatenate([w_q, w_v], axis=1).T.astype(bf16)
        w_ku = jnp.concatenate([w_k, w_u], axis=1).astype(bf16)
        qvt, k, u = _norm_inproj(x, meta, g_mix[layer][None, :], w_qv_t, w_ku, 2 * qk)
        a = _diff_attention(qvt, k, lam_vecs[layer].astype(f32), subln_gain[layer][None, :],
                            slopes, n_body, lam_init)
        p = _multiscale_pool(u, pool_w[layer].astype(bf16), pool_scale[layer][None, :], n_body)
        wo = w_o[layer].astype(bf16)
        h2, hn_tiles, aff = _outproj_router(
            x, meta, a, p, wo[:attn_width], wo[attn_width:], g_ffn[layer][None, :],
            jnp.pad(router_w[layer], ((0, 0), (0, LANES - N_EXPERTS))).astype(bf16))

        lp = _round_up(l, LANES)
        aff_pos = jnp.concatenate(
            [aff[:, n_body:, :N_EXPERTS], aff[:, :n_body, :N_EXPERTS], jnp.full((b, lp - l, N_EXPERTS), -1.0, f32)],
            axis=1)
        idx = _topk_rows(jnp.swapaxes(aff_pos, 1, 2), cap, cp, n_body)
        idx = jnp.swapaxes(idx[:, :, :N_EXPERTS], 1, 2)
        real = jnp.arange(cp)[None, None, :] < cap
        idx_gather = jnp.where(real, idx, 0)
        idx_scatter = jnp.where(real, idx, l)

        xg, gate = _gather_tokens(idx_gather, hn_tiles, aff, d)
        y = _expert_ffn(xg, gate, w_gate[layer], w_up[layer], w_down[layer], cp)
        out = _combine(idx_scatter, y, h2, g_final[None, :], n_body)
    return out
```
